```python
import jax, jax.numpy as jnp
from jax import lax
import numpy as np

D_MODEL = 2048
BATCH = 4
SEQ = 4096
DEPTH = 2
DEC_BATCH = 8
DEC_SEQ = 16
PAST_LEN = 4096

CHUNK = 64
HEAD_DIM = 128
D_MIX = D_MODEL
GDN_WIDTH = D_MIX // 4
GDN_HEADS = GDN_WIDTH // HEAD_DIM
GDN_CONV = 4
FOX_WIDTH = D_MIX // 2
FOX_HEADS = FOX_WIDTH // HEAD_DIM
Q_BLOCK = 128
LRU_WIDTH = D_MIX - GDN_WIDTH - FOX_WIDTH
LRU_BLOCKS = 4
LRU_BLOCK_W = LRU_WIDTH // LRU_BLOCKS
LRU_CONV = 4
LRU_C = 8.0
D_FF = (D_MODEL * 11 // 4) // 128 * 128
FFN_CONV = 3
EPS = 1e-6
IN_SIZES = (GDN_WIDTH, GDN_WIDTH, GDN_WIDTH, GDN_WIDTH, GDN_HEADS, GDN_HEADS,
            FOX_WIDTH, FOX_WIDTH, FOX_WIDTH, FOX_HEADS, LRU_WIDTH, LRU_WIDTH)
IN_WIDTH = sum(IN_SIZES)

kernel_name = 'hybrid_stream_encoder_step'

F32 = jnp.float32


def rmsnorm(x, g):
    xf = x.astype(F32)
    y = xf * lax.rsqrt(jnp.mean(xf * xf, axis=-1, keepdims=True) + EPS)
    return (y * g.astype(F32)).astype(x.dtype)


def l2norm(t):
    return t * lax.rsqrt(jnp.sum(t * t, axis=-1, keepdims=True) + EPS)


def causal_conv(x, past, w):
    width = w.shape[0]
    L = x.shape[1]
    xp = jnp.concatenate([past.astype(x.dtype), x], axis=1)
    y = sum(xp[:, j:j + L] * w[j] for j in range(width))
    return y, xp[:, xp.shape[1] - (width - 1):]


def gdn_chunk(S, blk):
    q, k, v, g, beta = blk
    c = q.shape[-2]
    G = jnp.cumsum(g, axis=-1)
    pos = jnp.arange(c)
    diff = G[..., :, None] - G[..., None, :]
    d_strict = jnp.exp(jnp.where(pos[:, None] > pos[None, :], diff, -jnp.inf))
    d_incl = jnp.exp(jnp.where(pos[:, None] >= pos[None, :], diff, -jnp.inf))
    kb = k * beta[..., None]
    m = jnp.einsum('bhtd,bhsd->bhts', kb, k) * d_strict
    rhs = jnp.concatenate([v * beta[..., None], kb * jnp.exp(G)[..., None]], axis=-1)
    sol = lax.linalg.triangular_solve(m + jnp.eye(c, dtype=F32), rhs, left_side=True,
                                      lower=True, unit_diagonal=True)
    u, w = sol[..., :HEAD_DIM], sol[..., HEAD_DIM:]
    delta = u - jnp.einsum('bhtk,bhkv->bhtv', w, S)
    o = (jnp.einsum('bhtk,bhkv->bhtv', q * jnp.exp(G)[..., None], S)
         + jnp.einsum('bhts,bhsv->bhtv', jnp.einsum('bhtd,bhsd->bhts', q, k) * d_incl, delta))
    g_last = G[..., -1]
    S_new = (jnp.exp(g_last)[..., None, None] * S
             + jnp.einsum('bhtk,bhtv->bhkv', k * jnp.exp(g_last[..., None] - G)[..., None], delta))
    return S_new, o


def gdn_mixer(q, k, v, z, b, a, conv_past, S0, conv_w, a_log, dt_bias, norm_g):
    B, L, _ = q.shape
    dt = q.dtype
    qkv, conv_new = causal_conv(jnp.concatenate([q, k, v], axis=-1), conv_past, conv_w)
    qkv = jax.nn.silu(qkv.astype(F32))
    q, k, v = [t.reshape(B, L, GDN_HEADS, HEAD_DIM) for t in jnp.split(qkv, 3, axis=-1)]
    q = l2norm(q) * HEAD_DIM ** -0.5
    k = l2norm(k)
    beta = jax.nn.sigmoid(b.astype(F32))
    g = -jnp.exp(a_log.astype(F32)) * jax.nn.softplus(a.astype(F32) + dt_bias.astype(F32))
    c = min(L, CHUNK)
    n = L // c

    def to_chunks(t):
        t = jnp.moveaxis(t, 2, 1)
        t = t.reshape(t.shape[:2] + (n, c) + t.shape[3:])
        return jnp.moveaxis(t, 2, 0)

    S_fin, o = lax.scan(gdn_chunk, S0.astype(F32), [to_chunks(t) for t in (q, k, v, g, beta)])
    o = jnp.moveaxis(o, 0, 2).reshape(B, GDN_HEADS, L, HEAD_DIM).transpose(0, 2, 1, 3)
    o = rmsnorm(o, norm_g) * jax.nn.silu(z.astype(F32).reshape(B, L, GDN_HEADS, HEAD_DIM))
    return o.reshape(B, L, GDN_WIDTH).astype(dt), conv_new, S_fin


def fox_attend(q, fq, qpos, k, v, fk, kpos):
    s = jnp.einsum('bqhd,bkhd->bhqk', q, k) * HEAD_DIM ** -0.5
    s = s + jnp.moveaxis(fq, 1, 2)[..., :, None] - jnp.moveaxis(fk, 1, 2)[..., None, :]
    s = jnp.where(kpos[None, :] <= qpos[:, None], s, -jnp.inf)
    p = jax.nn.softmax(s, axis=-1)
    return jnp.einsum('bhqk,bkhd->bqhd', p, v)


def fox_prompt(q, k, v, logf):
    L = q.shape[1]
    qb = min(Q_BLOCK, L)
    nb = L // qb
    F = jnp.cumsum(logf, axis=1)
    kpos = jnp.arange(L)

    def block(i):
        st = i * qb
        qs = lax.dynamic_slice_in_dim(q, st, qb, axis=1)
        fq = lax.dynamic_slice_in_dim(F, st, qb, axis=1)
        return fox_attend(qs, fq, st + jnp.arange(qb), k, v, F, kpos)

    o = lax.map(block, jnp.arange(nb))
    return jnp.moveaxis(o, 0, 1).reshape(q.shape)


def fox_sample(q, k, v, logf, ck, cv, clogf):
    P = ck.shape[1]
    L = q.shape[1]
    k_all = jnp.concatenate([ck, k], axis=1)
    v_all = jnp.concatenate([cv, v], axis=1)
    F = jnp.cumsum(jnp.concatenate([clogf, logf], axis=1), axis=1)
    return fox_attend(q, F[:, P:], P + jnp.arange(L), k_all, v_all, F, jnp.arange(P + L))


def lru_mixer(xb, gate, conv_past, h0, conv_w, conv_b, w_a, b_a, w_x, b_x, lam, norm_g):
    B, L, _ = xb.shape
    dt = xb.dtype
    xc, conv_new = causal_conv(xb, conv_past, conv_w)
    xc = (xc + conv_b).astype(F32)
    xblk = xc.reshape(B, L, LRU_BLOCKS, LRU_BLOCK_W)
    r = jax.nn.sigmoid(jnp.einsum('blnc,ncd->blnd', xblk, w_a.astype(F32)).reshape(B, L, LRU_WIDTH) + b_a)
    i = jax.nn.sigmoid(jnp.einsum('blnc,ncd->blnd', xblk, w_x.astype(F32)).reshape(B, L, LRU_WIDTH) + b_x)
    log_a = -LRU_C * r * jax.nn.softplus(-lam.astype(F32))
    a = jnp.exp(log_a)
    u = jnp.sqrt(-jnp.expm1(2.0 * log_a)) * (i * xc)
    u = u.at[:, 0].add(a[:, 0] * h0.astype(F32))

    def combine(e1, e2):
        a1, b1 = e1
        a2, b2 = e2
        return a1 * a2, a2 * b1 + b2

    _, h = lax.associative_scan(combine, (a, u), axis=1)
    y = rmsnorm(h.reshape(B, L, LRU_BLOCKS, LRU_BLOCK_W), norm_g.reshape(LRU_BLOCKS, LRU_BLOCK_W))
    y = y.reshape(B, L, LRU_WIDTH) * jax.nn.gelu(gate.astype(F32))
    return y.astype(dt), conv_new, h[:, -1]


def conv_ffn(h, conv_past, w_up, conv_w, w_down):
    gp, val = jnp.split(h @ w_up, 2, axis=-1)
    gc, conv_new = causal_conv(gp, conv_past, conv_w)
    return (jax.nn.silu(gc) * val) @ w_down, conv_new


def layer(x, l, p, fox_cache, gdn_S, gdn_conv, lru_h, lru_conv, ffn_conv):
    B, L, _ = x.shape
    dt = x.dtype
    h = rmsnorm(x, p['norm_mix_g'][l])
    proj = h @ p['w_in'][l]
    gq, gk, gv, gz, gb, ga, fq, fk, fv, ff, lx, lg = jnp.split(
        proj, np.cumsum(IN_SIZES)[:-1].tolist(), axis=-1)
    o_gdn, gdn_conv_new, gdn_S_new = gdn_mixer(
        gq, gk, gv, gz, gb, ga, gdn_conv, gdn_S, p['gdn_conv_w'][l], p['gdn_a_log'][l],
        p['gdn_dt_bias'][l], p['gdn_norm_g'][l])
    q4 = fq.reshape(B, L, FOX_HEADS, HEAD_DIM).astype(F32)
    k4 = fk.reshape(B, L, FOX_HEADS, HEAD_DIM).astype(F32)
    v4 = fv.reshape(B, L, FOX_HEADS, HEAD_DIM).astype(F32)
    logf = jax.nn.log_sigmoid(ff.astype(F32) + p['fox_f_bias'][l])
    if fox_cache is None:
        o = fox_prompt(q4, k4, v4, logf)
    else:
        ck, cv, cl = fox_cache
        o = fox_sample(q4, k4, v4, logf, ck.astype(F32), cv.astype(F32), cl.astype(F32))
    o_fox = rmsnorm(o, p['fox_norm_g'][l]).reshape(B, L, FOX_WIDTH).astype(dt)
    o_lru, lru_conv_new, lru_h_new = lru_mixer(
        lx, lg, lru_conv, lru_h, p['lru_conv_w'][l], p['lru_conv_b'][l], p['lru_w_a'][l],
        p['lru_b_a'][l], p['lru_w_x'][l], p['lru_b_x'][l], p['lru_lambda'][l], p['lru_norm_g'][l])
    x = x + jnp.concatenate([o_gdn, o_fox, o_lru], axis=-1) @ p['w_out'][l]
    f_out, ffn_conv_new = conv_ffn(rmsnorm(x, p['norm_ffn_g'][l]), ffn_conv, p['ffn_w_up'][l],
                                   p['ffn_conv_w'][l], p['ffn_w_down'][l])
    x = x + f_out
    new = [k4.astype(dt), v4.astype(dt), logf.astype(dt), gdn_S_new.astype(dt), gdn_conv_new,
           lru_h_new.astype(dt), lru_conv_new, ffn_conv_new]
    return x, new


def run_group(x, p, fox_k, fox_v, fox_logf, gdn_S, gdn_conv, lru_h, lru_conv, ffn_conv):
    outs = [[] for _ in range(8)]
    for l in range(DEPTH):
        fox_cache = None if fox_k is None else (fox_k[l], fox_v[l], fox_logf[l])
        x, new = layer(x, l, p, fox_cache, gdn_S[l], gdn_conv[l], lru_h[l], lru_conv[l], ffn_conv[l])
        for lst, arr in zip(outs, new):
            lst.append(arr)
    return rmsnorm(x, p['final_norm_g']), [jnp.stack(lst) for lst in outs]


def setup_inputs(seed: int = 0) -> dict:
    key = jax.random.key(seed)
    ks = jax.random.split(key, 40)
    nrm = lambda k, s, sc=1.0: sc * jax.random.normal(k, s, F32)
    unif = lambda k, s, lo, hi: jax.random.uniform(k, s, F32, lo, hi)
    dt = jnp.exp(unif(ks[14], (DEPTH, GDN_HEADS), float(np.log(1e-3)), float(np.log(1e-1))))
    lru_base = unif(ks[20], (DEPTH, LRU_WIDTH), 0.9, 0.999) ** (1.0 / LRU_C)
    return {
        'x_prompt': nrm(ks[0], (BATCH, SEQ, D_MODEL)),
        'x_sample': nrm(ks[1], (DEC_BATCH, DEC_SEQ, D_MODEL)),
        'cache_fox_k': nrm(ks[2], (DEPTH, DEC_BATCH, PAST_LEN, FOX_HEADS, HEAD_DIM)),
        'cache_fox_v': nrm(ks[3], (DEPTH, DEC_BATCH, PAST_LEN, FOX_HEADS, HEAD_DIM)),
        'cache_fox_logf': jax.nn.log_sigmoid(2.5 + nrm(ks[4], (DEPTH, DEC_BATCH, PAST_LEN, FOX_HEADS))),
        'state_gdn': nrm(ks[5], (DEPTH, DEC_BATCH, GDN_HEADS, HEAD_DIM, HEAD_DIM), 0.1),
        'state_gdn_conv': nrm(ks[6], (DEPTH, DEC_BATCH, GDN_CONV - 1, 3 * GDN_WIDTH)),
        'state_lru': nrm(ks[7], (DEPTH, DEC_BATCH, LRU_WIDTH), 0.5),
        'state_lru_conv': nrm(ks[8], (DEPTH, DEC_BATCH, LRU_CONV - 1, LRU_WIDTH)),
        'state_ffn_conv': nrm(ks[9], (DEPTH, DEC_BATCH, FFN_CONV - 1, D_FF)),
        'norm_mix_g': 1.0 + nrm(ks[10], (DEPTH, D_MODEL), 0.02),
        'w_in': nrm(ks[11], (DEPTH, D_MODEL, IN_WIDTH), D_MODEL ** -0.5),
        'gdn_conv_w': nrm(ks[12], (DEPTH, GDN_CONV, 3 * GDN_WIDTH), GDN_CONV ** -0.5),
        'gdn_a_log': jnp.log(unif(ks[13], (DEPTH, GDN_HEADS), 1.0, 16.0)),
        'gdn_dt_bias': dt + jnp.log(-jnp.expm1(-dt)),
        'gdn_norm_g': 1.0 + nrm(ks[15], (DEPTH, HEAD_DIM), 0.02),
        'fox_f_bias': unif(ks[16], (DEPTH, FOX_HEADS), 1.0, 4.0),
        'fox_norm_g': 1.0 + nrm(ks[17], (DEPTH, FOX_HEADS, HEAD_DIM), 0.02),
        'lru_conv_w': nrm(ks[18], (DEPTH, LRU_CONV, LRU_WIDTH), LRU_CONV ** -0.5),
        'lru_conv_b': nrm(ks[19], (DEPTH, LRU_WIDTH), 0.02),
        'lru_w_a': nrm(ks[21], (DEPTH, LRU_BLOCKS, LRU_BLOCK_W, LRU_BLOCK_W), LRU_BLOCK_W ** -0.5),
        'lru_b_a': nrm(ks[22], (DEPTH, LRU_WIDTH), 0.02),
        'lru_w_x': nrm(ks[23], (DEPTH, LRU_BLOCKS, LRU_BLOCK_W, LRU_BLOCK_W), LRU_BLOCK_W ** -0.5),
        'lru_b_x': nrm(ks[24], (DEPTH, LRU_WIDTH), 0.02),
        'lru_lambda': jnp.log(lru_base / (1.0 - lru_base)),
        'lru_norm_g': 1.0 + nrm(ks[25], (DEPTH, LRU_WIDTH), 0.02),
        'w_out': nrm(ks[26], (DEPTH, D_MIX, D_MODEL), D_MIX ** -0.5),
        'norm_ffn_g': 1.0 + nrm(ks[27], (DEPTH, D_MODEL), 0.02),
        'ffn_w_up': nrm(ks[28], (DEPTH, D_MODEL, 2 * D_FF), D_MODEL ** -0.5),
        'ffn_conv_w': nrm(ks[29], (DEPTH, FFN_CONV, D_FF), FFN_CONV ** -0.5),
        'ffn_w_down': nrm(ks[30], (DEPTH, D_FF, D_MODEL), D_FF ** -0.5),
        'final_norm_g': 1.0 + nrm(ks[31], (D_MODEL,), 0.02),
    }


def reference(x_prompt, x_sample, cache_fox_k, cache_fox_v, cache_fox_logf, state_gdn, state_gdn_conv,
              state_lru, state_lru_conv, state_ffn_conv, norm_mix_g, w_in, gdn_conv_w, gdn_a_log,
              gdn_dt_bias, gdn_norm_g, fox_f_bias, fox_norm_g, lru_conv_w, lru_conv_b, lru_w_a, lru_b_a,
              lru_w_x, lru_b_x, lru_lambda, lru_norm_g, w_out, norm_ffn_g, ffn_w_up, ffn_conv_w,
              ffn_w_down, final_norm_g):
    p = dict(norm_mix_g=norm_mix_g, w_in=w_in, gdn_conv_w=gdn_conv_w, gdn_a_log=gdn_a_log,
             gdn_dt_bias=gdn_dt_bias, gdn_norm_g=gdn_norm_g, fox_f_bias=fox_f_bias, fox_norm_g=fox_norm_g,
             lru_conv_w=lru_conv_w, lru_conv_b=lru_conv_b, lru_w_a=lru_w_a, lru_b_a=lru_b_a,
             lru_w_x=lru_w_x, lru_b_x=lru_b_x, lru_lambda=lru_lambda, lru_norm_g=lru_norm_g,
             w_out=w_out, norm_ffn_g=norm_ffn_g, ffn_w_up=ffn_w_up, ffn_conv_w=ffn_conv_w,
             ffn_w_down=ffn_w_down, final_norm_g=final_norm_g)
    bp = x_prompt.shape[0]
    dtp = x_prompt.dtype
    y_prompt, sp = run_group(
        x_prompt, p, None, None, None,
        jnp.zeros((DEPTH, bp, GDN_HEADS, HEAD_DIM, HEAD_DIM), dtp),
        jnp.zeros((DEPTH, bp, GDN_CONV - 1, 3 * GDN_WIDTH), dtp),
        jnp.zeros((DEPTH, bp, LRU_WIDTH), dtp),
        jnp.zeros((DEPTH, bp, LRU_CONV - 1, LRU_WIDTH), dtp),
        jnp.zeros((DEPTH, bp, FFN_CONV - 1, D_FF), dtp))
    y_sample, ss = run_group(x_sample, p, cache_fox_k, cache_fox_v, cache_fox_logf, state_gdn,
                             state_gdn_conv, state_lru, state_lru_conv, state_ffn_conv)
    fk_p, fv_p, fl_p, gs_p, gc_p, lh_p, lc_p, fc_p = sp
    fk_s, fv_s, fl_s, gs_s, gc_s, lh_s, lc_s, fc_s = ss
    return (y_prompt, y_sample, fk_p, fv_p, fl_p, gs_p, gc_p, lh_p, lc_p, fc_p,
            fk_s, fv_s, fl_s, gs_s, gc_s, lh_s, lc_s, fc_s)
```

```python
import functools
import math

import jax
import jax.numpy as jnp
from jax import lax
from jax.experimental import pallas as pl
from jax.experimental.pallas import tpu as pltpu

F32 = jnp.float32
BF16 = jnp.bfloat16

HEAD_DIM = 128
CHUNK = 64
LRU_C = 8.0
EPS = 1e-6
LANES = 128
SUBLANES = 8
VMEM_LIMIT = 56 * 1024 * 1024

GATE_COLS = LANES
GATE_F = 0
NEG_INF = float("-inf")


def _params(n_axes):
    return pltpu.CompilerParams(dimension_semantics=("arbitrary",) * n_axes,
                                vmem_limit_bytes=VMEM_LIMIT)


def _resident(block_shape, index_map):
    return pl.BlockSpec(block_shape, index_map, pipeline_mode=pl.Buffered(1))


def _dot(a, b):
    return jnp.dot(a.astype(BF16), b.astype(BF16), preferred_element_type=F32)


def _dot_nt(a, b):
    return lax.dot_general(a.astype(BF16), b.astype(BF16), (((1,), (1,)), ((), ())),
                           preferred_element_type=F32)


def _dot_tn(a, b):
    return lax.dot_general(a.astype(BF16), b.astype(BF16), (((0,), (0,)), ((), ())),
                           preferred_element_type=F32)


def _dot_f32(a, b):
    return jnp.dot(a, b, preferred_element_type=F32, precision=lax.Precision.HIGHEST)


def _rms(x, g):
    return x * lax.rsqrt(jnp.mean(x * x, axis=-1, keepdims=True) + EPS) * g


def _expm1(x):
    u = jnp.exp(x)
    um1 = u - 1.0
    near = jnp.where(u == 1.0, x, um1 * x / jnp.log(u))
    return jnp.where(jnp.abs(x) > 0.5, um1, near)


def _cumsum_lanes(x, seg=None):
    n = x.shape[-1]
    lane = lax.broadcasted_iota(jnp.int32, x.shape, x.ndim - 1)
    pos = lane if seg is None else lane % seg
    s = 1
    while s < (n if seg is None else seg):
        x = x + jnp.where(pos >= s, pltpu.roll(x, s, x.ndim - 1), 0.0)
        s *= 2
    return x


def _cumsum_rows(x):
    n = x.shape[0]
    row = lax.broadcasted_iota(jnp.int32, x.shape, 0)
    s = 1
    while s < n:
        x = x + jnp.where(row >= s, pltpu.roll(x, s, 0), 0.0)
        s *= 2
    return x


def _causal_conv(buf, x, w_ref, width, rows):
    lo = SUBLANES - (width - 1)
    buf[SUBLANES:SUBLANES + rows, :] = x
    y = w_ref[width - 1:width, :] * x
    for j in range(width - 1):
        y = y + w_ref[j:j + 1, :] * buf[lo + j:lo + j + rows, :]
    return y


def _inproj_kernel(x_ref, g_ref, w_ref, main_ref, small_ref, fk_ref, fv_ref, qkv_ref, h_scr, *,
                   n_main, n_fox, seg):
    h_scr[...] = _rms(x_ref[...], g_ref[...]).astype(BF16)

    def cols(c0, width):
        return jnp.dot(h_scr[...], w_ref[:, c0:c0 + width], preferred_element_type=F32)

    for c0 in range(0, n_main, seg):
        main_ref[:, c0:c0 + seg] = cols(c0, seg)
    for j in range(3):
        for c in range(0, n_fox, seg):
            p = cols(n_main + j * n_fox + c, seg)
            qkv_ref[:, j * n_fox + c:j * n_fox + c + seg] = p.astype(BF16)
            if j == 1:
                fk_ref[:, c:c + seg] = p
            if j == 2:
                fv_ref[:, c:c + seg] = p
    small_ref[...] = cols(n_main + 3 * n_fox, GATE_COLS)


def _inproj(x2d, g, w, *, n_main, n_fox, tm):
    m, d = x2d.shape
    n_all = w.shape[1]
    row = lambda i: (i, 0)
    fixed = lambda i: (0, 0)
    return pl.pallas_call(
        functools.partial(_inproj_kernel, n_main=n_main, n_fox=n_fox, seg=512),
        grid=(m // tm,),
        in_specs=[pl.BlockSpec((tm, d), row), _resident((1, d), fixed), _resident((d, n_all), fixed)],
        out_specs=[pl.BlockSpec((tm, n_main), row), pl.BlockSpec((tm, GATE_COLS), row),
                   pl.BlockSpec((tm, n_fox), row), pl.BlockSpec((tm, n_fox), row),
                   pl.BlockSpec((tm, 3 * n_fox), row)],
        out_shape=[jax.ShapeDtypeStruct((m, n_main), F32), jax.ShapeDtypeStruct((m, GATE_COLS), F32),
                   jax.ShapeDtypeStruct((m, n_fox), F32), jax.ShapeDtypeStruct((m, n_fox), F32),
                   jax.ShapeDtypeStruct((m, 3 * n_fox), BF16)],
        scratch_shapes=[pltpu.VMEM((tm, d), BF16)],
        compiler_params=_params(1), name="inproj")(x2d, g, w)


def _gate_kernel(sm_ref, b_ref, logf_ref, *rest, heads, cumulative):
    ls = jax.nn.log_sigmoid(sm_ref[0] + b_ref[...])
    logf_ref[0] = ls[:, GATE_F:GATE_F + heads]
    if cumulative:
        ft_ref, = rest
        ft_ref[0] = _cumsum_lanes(ls.T[GATE_F:GATE_F + heads, :])


def _gate(small3d, bias_row, *, heads, cumulative):
    b, rows, _ = small3d.shape
    out_specs = [pl.BlockSpec((1, rows, heads), lambda i: (i, 0, 0))]
    out_shape = [jax.ShapeDtypeStruct((b, rows, heads), F32)]
    if cumulative:
        out_specs.append(pl.BlockSpec((1, heads, rows), lambda i: (i, 0, 0)))
        out_shape.append(jax.ShapeDtypeStruct((b, heads, rows), F32))
    return pl.pallas_call(
        functools.partial(_gate_kernel, heads=heads, cumulative=cumulative),
        grid=(b,),
        in_specs=[pl.BlockSpec((1, rows, GATE_COLS), lambda i: (i, 0, 0)),
                  pl.BlockSpec((1, GATE_COLS), lambda i: (0, 0))],
        out_specs=out_specs, out_shape=out_shape,
        compiler_params=_params(1), name="fox_gate")(small3d, bias_row)


def _unit_lower_inverse(a):
    c = a.shape[0]
    n = -a
    row = lax.broadcasted_iota(jnp.int32, (c, c), 0)
    col = lax.broadcasted_iota(jnp.int32, (c, c), 1)
    t = jnp.where(row == col, 1.0, 0.0) + n
    p = n
    for _ in range(int(math.log2(c)) - 1):
        p = _dot_f32(p, p)
        t = t + _dot_f32(t, p)
    return t


def _gdn_kernel(qkv_ref, z_ref, sm_ref, conv0_ref, s0_ref, cw_ref, alog_ref, dt_ref, ng_ref,
                o_ref, s_ref, conv_ref, buf, *, c, heads, conv_w, beta_col, a_col):
    ci = pl.program_id(1)
    width = heads * HEAD_DIM
    hist = conv_w - 1
    lo = SUBLANES - hist

    @pl.when(ci == 0)
    def _():
        buf[lo:SUBLANES, :] = conv0_ref[0]
        s_ref[0] = s0_ref[0]

    y = _causal_conv(buf, qkv_ref[0], cw_ref, conv_w, c)
    new_hist = buf[lo + c:SUBLANES + c, :]
    conv_ref[0] = new_hist
    buf[lo:SUBLANES, :] = new_hist
    y = jax.nn.silu(y)

    sm = sm_ref[0]
    beta = jax.nn.sigmoid(sm)
    g = -jnp.exp(alog_ref[...]) * jax.nn.softplus(sm + dt_ref[...])
    gcum = _cumsum_rows(g)
    gcum_t = jnp.concatenate([gcum, jnp.zeros((LANES - c, GATE_COLS), F32)], axis=0).T

    row = lax.broadcasted_iota(jnp.int32, (c, c), 0)
    col = lax.broadcasted_iota(jnp.int32, (c, c), 1)
    z = z_ref[0]
    for h in range(heads):
        q = y[:, h * HEAD_DIM:(h + 1) * HEAD_DIM]
        k = y[:, width + h * HEAD_DIM:width + (h + 1) * HEAD_DIM]
        v = y[:, 2 * width + h * HEAD_DIM:2 * width + (h + 1) * HEAD_DIM]
        q = q * lax.rsqrt(jnp.sum(q * q, axis=-1, keepdims=True) + EPS) * HEAD_DIM ** -0.5
        k = k * lax.rsqrt(jnp.sum(k * k, axis=-1, keepdims=True) + EPS)
        bh = beta[:, beta_col + h:beta_col + h + 1]
        gc = gcum[:, a_col + h:a_col + h + 1]
        gr = gcum_t[a_col + h:a_col + h + 1, :c]
        decay = jnp.exp(jnp.where(row >= col, gc - gr, NEG_INF))
        d_strict = jnp.where(row > col, decay, 0.0)
        kb = k * bh
        eg = jnp.exp(gc)
        m = _dot_nt(kb, k) * d_strict
        t = _unit_lower_inverse(m)
        sol = _dot_f32(t, jnp.concatenate([v * bh, kb * eg], axis=-1))
        u, w = sol[:, :HEAD_DIM], sol[:, HEAD_DIM:]
        s = s_ref[0, h]
        delta = u - _dot(w, s)
        o = _dot(q * eg, s) + _dot(_dot_nt(q, k) * decay, delta)
        g_last = gc[c - 1:c, :]
        s_ref[0, h] = jnp.exp(g_last) * s + _dot_tn(k * jnp.exp(g_last - gc), delta)
        zh = z[:, h * HEAD_DIM:(h + 1) * HEAD_DIM]
        o_ref[0, :, h * HEAD_DIM:(h + 1) * HEAD_DIM] = (
            _rms(o, ng_ref[...]) * jax.nn.silu(zh)).astype(o_ref.dtype)


def _gdn(main3d, small3d, conv0, s0, conv_w, alog_row, dt_row, norm_g, *, c, heads, beta_col, a_col):
    b, l, _ = main3d.shape
    width = heads * HEAD_DIM
    kw = conv_w.shape[0]
    fixed2 = lambda i, j: (0, 0)
    return pl.pallas_call(
        functools.partial(_gdn_kernel, c=c, heads=heads, conv_w=kw, beta_col=beta_col, a_col=a_col),
        grid=(b, l // c),
        in_specs=[pl.BlockSpec((1, c, 3 * width), lambda i, j: (i, j, 0)),
                  pl.BlockSpec((1, c, width), lambda i, j: (i, j, 3)),
                  pl.BlockSpec((1, c, GATE_COLS), lambda i, j: (i, j, 0)),
                  pl.BlockSpec((1, kw - 1, 3 * width), lambda i, j: (i, 0, 0)),
                  pl.BlockSpec((1, heads, HEAD_DIM, HEAD_DIM), lambda i, j: (i, 0, 0, 0)),
                  pl.BlockSpec((kw, 3 * width), fixed2),
                  pl.BlockSpec((1, GATE_COLS), fixed2), pl.BlockSpec((1, GATE_COLS), fixed2),
                  pl.BlockSpec((1, HEAD_DIM), fixed2)],
        out_specs=[pl.BlockSpec((1, c, width), lambda i, j: (i, j, 0)),
                   pl.BlockSpec((1, heads, HEAD_DIM, HEAD_DIM), lambda i, j: (i, 0, 0, 0)),
                   pl.BlockSpec((1, kw - 1, 3 * width), lambda i, j: (i, 0, 0))],
        out_shape=[jax.ShapeDtypeStruct((b, l, width), BF16),
                   jax.ShapeDtypeStruct((b, heads, HEAD_DIM, HEAD_DIM), F32),
                   jax.ShapeDtypeStruct((b, kw - 1, 3 * width), F32)],
        scratch_shapes=[pltpu.VMEM((SUBLANES + c, 3 * width), F32)],
        compiler_params=_params(2), name="gdn")(
            main3d, main3d, small3d, conv0, s0, conv_w, alog_row, dt_row, norm_g)


def _fox_prompt_kernel(q_ref, k_ref, v_ref, ft_ref, g_ref, o_ref, m_scr, l_scr, acc_scr, *, tq):
    qi = pl.program_id(2)
    q = q_ref[0]
    m_scr[...] = jnp.full(m_scr.shape, NEG_INF, F32)
    l_scr[...] = jnp.zeros(l_scr.shape, F32)
    acc_scr[...] = jnp.zeros(acc_scr.shape, F32)
    scale = HEAD_DIM ** -0.5

    def block(j, masked):
        start = pl.multiple_of(j * tq, tq)
        ks = k_ref[0, pl.ds(start, tq), :]
        vs = v_ref[0, pl.ds(start, tq), :]
        s = _dot_nt(q, ks) * scale - ft_ref[0, 0, pl.ds(j, 1), :]
        if masked:
            row = lax.broadcasted_iota(jnp.int32, (tq, tq), 0)
            col = lax.broadcasted_iota(jnp.int32, (tq, tq), 1)
            s = jnp.where(col <= row, s, NEG_INF)
        m_prev = m_scr[...]
        m_new = jnp.maximum(m_prev, jnp.max(s, axis=-1, keepdims=True))
        alpha = jnp.exp(m_prev - m_new)
        p = jnp.exp(s - m_new)
        l_scr[...] = alpha * l_scr[...] + jnp.sum(p, axis=-1, keepdims=True)
        acc_scr[...] = alpha * acc_scr[...] + _dot(p, vs)
        m_scr[...] = m_new

    def body(j, carry):
        block(j, False)
        return carry

    lax.fori_loop(0, qi, body, 0)
    block(qi, True)
    o = acc_scr[...] / l_scr[...]
    o_ref[0] = _rms(o, g_ref[0]).astype(o_ref.dtype)


def _fox_prompt(qkv3d, ft, norm_g, *, heads, tq):
    b, l, _ = qkv3d.shape
    ft4 = ft.reshape(b, heads, l // tq, tq)
    g3 = norm_g.reshape(heads, 1, HEAD_DIM)
    return pl.pallas_call(
        functools.partial(_fox_prompt_kernel, tq=tq),
        grid=(b, heads, l // tq),
        in_specs=[pl.BlockSpec((1, tq, HEAD_DIM), lambda i, h, j: (i, j, h)),
                  pl.BlockSpec((1, l, HEAD_DIM), lambda i, h, j: (i, 0, heads + h)),
                  pl.BlockSpec((1, l, HEAD_DIM), lambda i, h, j: (i, 0, 2 * heads + h)),
                  pl.BlockSpec((1, 1, l // tq, tq), lambda i, h, j: (i, h, 0, 0)),
                  pl.BlockSpec((1, 1, HEAD_DIM), lambda i, h, j: (h, 0, 0))],
        out_specs=pl.BlockSpec((1, tq, HEAD_DIM), lambda i, h, j: (i, j, h)),
        out_shape=jax.ShapeDtypeStruct((b, l, heads * HEAD_DIM), BF16),
        scratch_shapes=[pltpu.VMEM((tq, 1), F32), pltpu.VMEM((tq, 1), F32),
                        pltpu.VMEM((tq, HEAD_DIM), F32)],
        compiler_params=_params(3), name="fox_prompt")(qkv3d, qkv3d, qkv3d, ft4, g3)


def _fox_sample_kernel(q_ref, kn_ref, vn_ref, ck_ref, cv_ref, clt_ref, lnt_ref, g_ref, o_ref, *, ls):
    p_len = ck_ref.shape[2]
    scale = HEAD_DIM ** -0.5
    q = q_ref[0]
    fc = _cumsum_lanes(clt_ref[0, 0])
    fn = fc[:, p_len - 1:p_len] + _cumsum_lanes(lnt_ref[0, 0])[:, :ls]
    s_c = _dot_nt(q, ck_ref[0, 0]) * scale - fc
    s_n = _dot_nt(q, kn_ref[0]) * scale - fn
    row = lax.broadcasted_iota(jnp.int32, (ls, ls), 0)
    col = lax.broadcasted_iota(jnp.int32, (ls, ls), 1)
    s_n = jnp.where(col <= row, s_n, NEG_INF)
    m = jnp.maximum(jnp.max(s_c, axis=-1, keepdims=True), jnp.max(s_n, axis=-1, keepdims=True))
    p_c = jnp.exp(s_c - m)
    p_n = jnp.exp(s_n - m)
    denom = jnp.sum(p_c, axis=-1, keepdims=True) + jnp.sum(p_n, axis=-1, keepdims=True)
    o = (_dot(p_c, cv_ref[0, 0]) + _dot(p_n, vn_ref[0])) / denom
    o_ref[0] = _rms(o, g_ref[0]).astype(o_ref.dtype)


def _fox_sample(qkv3d, cache_k, cache_v, cache_logf_t, logf_new_t, norm_g, *, heads):
    b, ls, _ = qkv3d.shape
    p_len = cache_k.shape[1]
    ck = cache_k.reshape(b, 1, p_len, heads * HEAD_DIM)
    cv = cache_v.reshape(b, 1, p_len, heads * HEAD_DIM)
    g3 = norm_g.reshape(heads, 1, HEAD_DIM)
    return pl.pallas_call(
        functools.partial(_fox_sample_kernel, ls=ls),
        grid=(b, heads),
        in_specs=[pl.BlockSpec((1, ls, HEAD_DIM), lambda i, h: (i, 0, h)),
                  pl.BlockSpec((1, ls, HEAD_DIM), lambda i, h: (i, 0, heads + h)),
                  pl.BlockSpec((1, ls, HEAD_DIM), lambda i, h: (i, 0, 2 * heads + h)),
                  pl.BlockSpec((1, 1, p_len, HEAD_DIM), lambda i, h: (i, 0, 0, h)),
                  pl.BlockSpec((1, 1, p_len, HEAD_DIM), lambda i, h: (i, 0, 0, h)),
                  pl.BlockSpec((1, 1, 1, p_len), lambda i, h: (i, h, 0, 0)),
                  pl.BlockSpec((1, 1, 1, LANES), lambda i, h: (i, h, 0, 0)),
                  pl.BlockSpec((1, 1, HEAD_DIM), lambda i, h: (h, 0, 0))],
        out_specs=pl.BlockSpec((1, ls, HEAD_DIM), lambda i, h: (i, 0, h)),
        out_shape=jax.ShapeDtypeStruct((b, ls, heads * HEAD_DIM), BF16),
        compiler_params=_params(2), name="fox_sample")(
            qkv3d, qkv3d, qkv3d, ck, cv, cache_logf_t, logf_new_t, g3)


def _lru_kernel(x_ref, gate_ref, conv0_ref, h0_ref, cw_ref, cb_ref, wa_ref, wx_ref, ba_ref, bx_ref,
                lam_ref, ng_ref, o_ref, h_ref, conv_ref, buf, a_scr, hs_scr, *, tl, blocks, conv_w):
    ti = pl.program_id(1)
    hist = conv_w - 1
    lo = SUBLANES - hist
    bw = x_ref.shape[2] // blocks

    @pl.when(ti == 0)
    def _():
        buf[lo:SUBLANES, :] = conv0_ref[0]
        h_ref[0] = h0_ref[0]

    xc = _causal_conv(buf, x_ref[0], cw_ref, conv_w, tl) + cb_ref[...]
    new_hist = buf[lo + tl:SUBLANES + tl, :]
    conv_ref[0] = new_hist
    buf[lo:SUBLANES, :] = new_hist

    for n in range(blocks):
        sl = slice(n * bw, (n + 1) * bw)
        xb = xc[:, sl]
        r = jax.nn.sigmoid(_dot(xb, wa_ref[n]) + ba_ref[:, sl])
        i = jax.nn.sigmoid(_dot(xb, wx_ref[n]) + bx_ref[:, sl])
        log_a = -LRU_C * r * jax.nn.softplus(-lam_ref[:, sl])
        a_scr[:, sl] = jnp.exp(log_a)
        hs_scr[:, sl] = jnp.sqrt(-_expm1(2.0 * log_a)) * (i * xb)

    def step(t, h):
        h = a_scr[pl.ds(t, 1), :] * h + hs_scr[pl.ds(t, 1), :]
        hs_scr[pl.ds(t, 1), :] = h
        return h

    h_ref[0] = lax.fori_loop(0, tl, step, h_ref[0], unroll=8)

    gate = gate_ref[0]
    for n in range(blocks):
        sl = slice(n * bw, (n + 1) * bw)
        o_ref[0, :, sl] = (_rms(hs_scr[:, sl], ng_ref[:, sl])
                           * jax.nn.gelu(gate[:, sl])).astype(o_ref.dtype)


def _lru(main3d, conv0, h0, cw, cb, wa, wx, ba, bx, lam, ng, *, tl, x_block, gate_block):
    b, l, _ = main3d.shape
    blocks, bw, _ = wa.shape
    width = blocks * bw
    kw = cw.shape[0]
    fixed2 = lambda i, j: (0, 0)
    fixed3 = lambda i, j: (0, 0, 0)
    vec = pl.BlockSpec((1, width), fixed2)
    return pl.pallas_call(
        functools.partial(_lru_kernel, tl=tl, blocks=blocks, conv_w=kw),
        grid=(b, l // tl),
        in_specs=[pl.BlockSpec((1, tl, width), lambda i, j: (i, j, x_block)),
                  pl.BlockSpec((1, tl, width), lambda i, j: (i, j, gate_block)),
                  pl.BlockSpec((1, kw - 1, width), lambda i, j: (i, 0, 0)),
                  pl.BlockSpec((1, 1, width), lambda i, j: (i, 0, 0)),
                  pl.BlockSpec((kw, width), fixed2), vec,
                  pl.BlockSpec((blocks, bw, bw), fixed3), pl.BlockSpec((blocks, bw, bw), fixed3),
                  vec, vec, vec, vec],
        out_specs=[pl.BlockSpec((1, tl, width), lambda i, j: (i, j, 0)),
                   pl.BlockSpec((1, 1, width), lambda i, j: (i, 0, 0)),
                   pl.BlockSpec((1, kw - 1, width), lambda i, j: (i, 0, 0))],
        out_shape=[jax.ShapeDtypeStruct((b, l, width), BF16),
                   jax.ShapeDtypeStruct((b, 1, width), F32),
                   jax.ShapeDtypeStruct((b, kw - 1, width), F32)],
        scratch_shapes=[pltpu.VMEM((SUBLANES + tl, width), F32), pltpu.VMEM((tl, width), F32),
                        pltpu.VMEM((tl, width), F32)],
        compiler_params=_params(2), name="rglru")(
            main3d, main3d, conv0, h0, cw, cb, wa, wx, ba, bx, lam, ng)


def _outproj_kernel(x_ref, og_ref, of_ref, ol_ref, w_ref, y_ref):
    wg, wf = og_ref.shape[1], of_ref.shape[1]
    y = x_ref[...] + jnp.dot(og_ref[...], w_ref[0:wg, :], preferred_element_type=F32)
    y = y + jnp.dot(of_ref[...], w_ref[wg:wg + wf, :], preferred_element_type=F32)
    y_ref[...] = y + jnp.dot(ol_ref[...], w_ref[wg + wf:, :], preferred_element_type=F32)


def _outproj(x2d, o_gdn, o_fox, o_lru, w, *, tm):
    m, d = x2d.shape
    row = lambda i: (i, 0)
    return pl.pallas_call(
        _outproj_kernel, grid=(m // tm,),
        in_specs=[pl.BlockSpec((tm, d), row), pl.BlockSpec((tm, o_gdn.shape[1]), row),
                  pl.BlockSpec((tm, o_fox.shape[1]), row), pl.BlockSpec((tm, o_lru.shape[1]), row),
                  _resident(w.shape, lambda i: (0, 0))],
        out_specs=pl.BlockSpec((tm, d), row),
        out_shape=jax.ShapeDtypeStruct((m, d), F32),
        compiler_params=_params(1), name="outproj")(x2d, o_gdn, o_fox, o_lru, w)


def _ffn_kernel(x_ref, g_ref, wg_ref, wv_ref, wd_ref, cw_ref, c0_ref, fg_ref, y_ref, cnew_ref,
                h_scr, gp_scr, *, nb, tiles_per_batch, tm, conv_w, final_norm):
    gi = pl.program_id(0)
    f = pl.program_id(1)
    hist = conv_w - 1
    lo = SUBLANES - hist

    @pl.when(f == 0)
    def _():
        for j in range(nb):
            h_scr[j] = _rms(x_ref[0, j], g_ref[...]).astype(BF16)

    for j in range(nb):
        h = h_scr[j]
        gp = jnp.dot(h, wg_ref[...], preferred_element_type=F32)
        val = jnp.dot(h, wv_ref[...], preferred_element_type=F32)
        if tiles_per_batch == 1:
            gp_scr[lo:SUBLANES, :] = c0_ref[j]
        else:
            first = gi % tiles_per_batch == 0

            @pl.when(first)
            def _():
                gp_scr[lo:SUBLANES, :] = c0_ref[0]

            @pl.when(jnp.logical_not(first))
            def _():
                gp_scr[lo:SUBLANES, :] = cnew_ref[0, f]

        gc = _causal_conv(gp_scr, gp, cw_ref, conv_w, tm)
        cnew_ref[j, f] = gp_scr[lo + tm:SUBLANES + tm, :]
        act = (jax.nn.silu(gc) * val).astype(BF16)
        contrib = jnp.dot(act, wd_ref[...], preferred_element_type=F32)

        @pl.when(f == 0)
        def _():
            y_ref[0, j] = x_ref[0, j] + contrib

        @pl.when(f > 0)
        def _():
            y_ref[0, j] += contrib

    if final_norm:
        @pl.when(f == pl.num_programs(1) - 1)
        def _():
            for j in range(nb):
                y_ref[0, j] = _rms(y_ref[0, j], fg_ref[...])


def _ffn(x3d, g, wg, wv, wd, cw, c0, fg, *, tm, tf, final_norm):
    b, l, d = x3d.shape
    dff = wg.shape[1]
    kw = cw.shape[0]
    tiles_per_batch = l // tm
    nb = b if tiles_per_batch == 1 else 1
    groups = b * l // (nb * tm)
    x4 = x3d.reshape(groups, nb, tm, d)
    nf = dff // tf
    if tiles_per_batch == 1:
        c_idx = lambda i, f: (0, 0, f)
        cnew_idx = lambda i, f: (0, 0, 0, 0)
    else:
        c_idx = lambda i, f: (i // tiles_per_batch, 0, f)
        cnew_idx = lambda i, f: (i // tiles_per_batch, 0, 0, 0)
    y, cnew = pl.pallas_call(
        functools.partial(_ffn_kernel, nb=nb, tiles_per_batch=tiles_per_batch, tm=tm, conv_w=kw,
                          final_norm=final_norm),
        grid=(groups, dff // tf),
        in_specs=[pl.BlockSpec((1, nb, tm, d), lambda i, f: (i, 0, 0, 0)),
                  pl.BlockSpec((1, d), lambda i, f: (0, 0)),
                  pl.BlockSpec((d, tf), lambda i, f: (0, f)),
                  pl.BlockSpec((d, tf), lambda i, f: (0, f)),
                  pl.BlockSpec((tf, d), lambda i, f: (f, 0)),
                  pl.BlockSpec((kw, tf), lambda i, f: (0, f)),
                  pl.BlockSpec((nb, kw - 1, tf), c_idx),
                  pl.BlockSpec((1, d), lambda i, f: (0, 0))],
        out_specs=[pl.BlockSpec((1, nb, tm, d), lambda i, f: (i, 0, 0, 0)),
                   pl.BlockSpec((nb, nf, kw - 1, tf), cnew_idx)],
        out_shape=[jax.ShapeDtypeStruct((groups, nb, tm, d), F32),
                   jax.ShapeDtypeStruct((b, nf, kw - 1, tf), F32)],
        scratch_shapes=[pltpu.VMEM((nb, tm, d), BF16), pltpu.VMEM((SUBLANES + tm, tf), F32)],
        compiler_params=_params(2), name="ffn")(x4, g, wg, wv, wd, cw, c0, fg)
    return y.reshape(b, l, d), jnp.swapaxes(cnew, 1, 2).reshape(b, kw - 1, dff)


def _pad_cols(a, width):
    return jnp.pad(a, ((0, 0), (0, width - a.shape[1])))


def _layer_weights(l, w_in, w_out, ffn_w_up, ffn_w_down, sizes):
    gdn_w, gdn_h, fox_w, fox_h, lru_w = sizes
    w = w_in[l]
    offs = [0]
    for s in (gdn_w, gdn_w, gdn_w, gdn_w, gdn_h, gdn_h, fox_w, fox_w, fox_w, fox_h, lru_w, lru_w):
        offs.append(offs[-1] + s)
    gq, gk, gv, gz, gb, ga, fq, fk, fv, ff, lx, lg = [w[:, offs[i]:offs[i + 1]] for i in range(12)]
    small = _pad_cols(jnp.concatenate([ff, gb, ga], axis=1), GATE_COLS)
    packed = jnp.concatenate([gq, gk, gv, gz, lx, lg, fq, fk, fv, small], axis=1).astype(BF16)
    dff = ffn_w_down.shape[1]
    return dict(w_in=packed, w_out=w_out[l].astype(BF16), w_gate=ffn_w_up[l, :, :dff].astype(BF16),
                w_val=ffn_w_up[l, :, dff:].astype(BF16), w_down=ffn_w_down[l].astype(BF16))


def _run_group(x, cache, states, p, weights, dims, tiles):
    gdn_w, gdn_h, fox_w, fox_h, lru_w = dims
    b, l, d = x.shape
    depth = len(weights)
    n_main = 4 * gdn_w + 2 * lru_w
    gdn_s, gdn_conv, lru_h, lru_conv, ffn_conv = states
    beta_col, a_col = GATE_F + fox_h, GATE_F + fox_h + gdn_h
    outs = [[] for _ in range(8)]
    for li in range(depth):
        wl = weights[li]
        main, small, fk, fv, qkv = _inproj(
            x.reshape(b * l, d), p['norm_mix_g'][li][None], wl['w_in'],
            n_main=n_main, n_fox=fox_w, tm=tiles['proj'])
        main3 = main.reshape(b, l, n_main)
        small3 = small.reshape(b, l, GATE_COLS)
        qkv3 = qkv.reshape(b, l, 3 * fox_w)
        bias_row = _pad_cols(p['fox_f_bias'][li][None], GATE_COLS)
        if cache is None:
            logf, ft = _gate(small3, bias_row, heads=fox_h, cumulative=True)
            o_fox = _fox_prompt(qkv3, ft, p['fox_norm_g'][li], heads=fox_h, tq=tiles['attn'])
        else:
            logf, = _gate(small3, bias_row, heads=fox_h, cumulative=False)
            ck, cv, cl = cache[0][li], cache[1][li], cache[2][li]
            clt = jnp.swapaxes(cl, 1, 2)[:, :, None, :]
            lnt = jnp.pad(jnp.swapaxes(logf, 1, 2), ((0, 0), (0, 0), (0, LANES - l)))[:, :, None, :]
            o_fox = _fox_sample(qkv3, ck, cv, clt, lnt, p['fox_norm_g'][li], heads=fox_h)
        row = lambda v, c0: jnp.pad(v[None], ((0, 0), (c0, GATE_COLS - c0 - v.shape[0])))
        o_gdn, s_new, gconv_new = _gdn(
            main3, small3, gdn_conv[li], gdn_s[li], p['gdn_conv_w'][li],
            row(p['gdn_a_log'][li], a_col), row(p['gdn_dt_bias'][li], a_col),
            p['gdn_norm_g'][li][None], c=min(l, CHUNK), heads=gdn_h, beta_col=beta_col, a_col=a_col)
        o_lru, h_new, lconv_new = _lru(
            main3, lru_conv[li], lru_h[li][:, None, :], p['lru_conv_w'][li], p['lru_conv_b'][li][None],
            p['lru_w_a'][li].astype(BF16), p['lru_w_x'][li].astype(BF16), p['lru_b_a'][li][None],
            p['lru_b_x'][li][None], p['lru_lambda'][li][None], p['lru_norm_g'][li][None],
            tl=tiles['lru'], x_block=4 * gdn_w // lru_w, gate_block=4 * gdn_w // lru_w + 1)
        x1 = _outproj(x.reshape(b * l, d), o_gdn.reshape(b * l, gdn_w), o_fox.reshape(b * l, fox_w),
                      o_lru.reshape(b * l, lru_w), wl['w_out'], tm=tiles['proj'])
        x, fconv_new = _ffn(
            x1.reshape(b, l, d), p['norm_ffn_g'][li][None], wl['w_gate'], wl['w_val'], wl['w_down'],
            p['ffn_conv_w'][li], ffn_conv[li], p['final_norm_g'][None],
            tm=tiles['ffn'], tf=512, final_norm=li == depth - 1)
        new = [fk.reshape(b, l, fox_h, HEAD_DIM), fv.reshape(b, l, fox_h, HEAD_DIM), logf, s_new,
               gconv_new, h_new[:, 0, :], lconv_new, fconv_new]
        for lst, arr in zip(outs, new):
            lst.append(arr)
    return x, [jnp.stack(lst) for lst in outs]


def kernel(x_prompt, x_sample, cache_fox_k, cache_fox_v, cache_fox_logf, state_gdn, state_gdn_conv,
           state_lru, state_lru_conv, state_ffn_conv, norm_mix_g, w_in, gdn_conv_w, gdn_a_log,
           gdn_dt_bias, gdn_norm_g, fox_f_bias, fox_norm_g, lru_conv_w, lru_conv_b, lru_w_a, lru_b_a,
           lru_w_x, lru_b_x, lru_lambda, lru_norm_g, w_out, norm_ffn_g, ffn_w_up, ffn_conv_w,
           ffn_w_down, final_norm_g):
    p = dict(norm_mix_g=norm_mix_g, gdn_conv_w=gdn_conv_w, gdn_a_log=gdn_a_log,
             gdn_dt_bias=gdn_dt_bias, gdn_norm_g=gdn_norm_g, fox_f_bias=fox_f_bias,
             fox_norm_g=fox_norm_g, lru_conv_w=lru_conv_w, lru_conv_b=lru_conv_b, lru_w_a=lru_w_a,
             lru_b_a=lru_b_a, lru_w_x=lru_w_x, lru_b_x=lru_b_x, lru_lambda=lru_lambda,
             lru_norm_g=lru_norm_g, norm_ffn_g=norm_ffn_g, ffn_conv_w=ffn_conv_w,
             final_norm_g=final_norm_g)
    depth = w_in.shape[0]
    bp, lp, _ = x_prompt.shape
    gdn_h = state_gdn.shape[2]
    fox_h = cache_fox_k.shape[3]
    lru_w = state_lru.shape[2]
    dims = (gdn_h * HEAD_DIM, gdn_h, fox_h * HEAD_DIM, fox_h, lru_w)
    dff = ffn_w_down.shape[1]
    weights = [_layer_weights(l, w_in, w_out, ffn_w_up, ffn_w_down, dims) for l in range(depth)]

    zeros = lambda *s: jnp.zeros((depth, bp) + s, F32)
    prompt_states = (zeros(gdn_h, HEAD_DIM, HEAD_DIM), zeros(gdn_conv_w.shape[1] - 1, 3 * dims[0]),
                     zeros(lru_w), zeros(lru_conv_w.shape[1] - 1, lru_w),
                     zeros(ffn_conv_w.shape[1] - 1, dff))
    prompt_tiles = dict(proj=min(256, bp * lp), attn=min(512, lp), lru=min(256, lp), ffn=min(512, lp))
    y_prompt, sp = _run_group(x_prompt, None, prompt_states, p, weights, dims, prompt_tiles)

    bs, ls, _ = x_sample.shape
    sample_tiles = dict(proj=bs * ls, lru=ls, ffn=ls)
    y_sample, ss = _run_group(
        x_sample, (cache_fox_k, cache_fox_v, cache_fox_logf),
        (state_gdn, state_gdn_conv, state_lru, state_lru_conv, state_ffn_conv), p, weights, dims,
        sample_tiles)
    return (y_prompt, y_sample, *sp, *ss)
```

```python
import functools
import math

import jax
import jax.numpy as jnp
from jax import lax
from jax.experimental import pallas as pl
from jax.experimental.pallas import tpu as pltpu

F32 = jnp.float32
BF16 = jnp.bfloat16

HEAD_DIM = 128
CHUNK = 64
LRU_C = 8.0
EPS = 1e-6
LANES = 128
SUBLANES = 8
VMEM_LIMIT = 56 * 1024 * 1024

GATE_COLS = LANES
GATE_F = 0
NEG_INF = float("-inf")
LOG2E = math.log2(math.e)


def _params(n_axes):
    return pltpu.CompilerParams(dimension_semantics=("arbitrary",) * n_axes,
                                vmem_limit_bytes=VMEM_LIMIT)


def _resident(block_shape, index_map):
    return pl.BlockSpec(block_shape, index_map, pipeline_mode=pl.Buffered(1))


def _dot(a, b):
    return jnp.dot(a.astype(BF16), b.astype(BF16), preferred_element_type=F32)


def _dot_nt(a, b):
    return lax.dot_general(a.astype(BF16), b.astype(BF16), (((1,), (1,)), ((), ())),
                           preferred_element_type=F32)


def _dot_tn(a, b):
    return lax.dot_general(a.astype(BF16), b.astype(BF16), (((0,), (0,)), ((), ())),
                           preferred_element_type=F32)


def _dot_f32(a, b):
    return jnp.dot(a, b, preferred_element_type=F32, precision=lax.Precision.HIGHEST)


def _rms(x, g):
    return x * lax.rsqrt(jnp.mean(x * x, axis=-1, keepdims=True) + EPS) * g


def _expm1(x):
    u = jnp.exp(x)
    um1 = u - 1.0
    near = jnp.where(u == 1.0, x, um1 * x / jnp.log(u))
    return jnp.where(jnp.abs(x) > 0.5, um1, near)


def _cumsum_lanes(x, seg=None):
    n = x.shape[-1]
    lane = lax.broadcasted_iota(jnp.int32, x.shape, x.ndim - 1)
    pos = lane if seg is None else lane % seg
    s = 1
    while s < (n if seg is None else seg):
        x = x + jnp.where(pos >= s, pltpu.roll(x, s, x.ndim - 1), 0.0)
        s *= 2
    return x


def _cumsum_rows(x):
    n = x.shape[0]
    row = lax.broadcasted_iota(jnp.int32, x.shape, 0)
    s = 1
    while s < n:
        x = x + jnp.where(row >= s, pltpu.roll(x, s, 0), 0.0)
        s *= 2
    return x


def _causal_conv(buf, x, w_ref, width, rows):
    lo = SUBLANES - (width - 1)
    buf[SUBLANES:SUBLANES + rows, :] = x
    y = w_ref[width - 1:width, :] * x
    for j in range(width - 1):
        y = y + w_ref[j:j + 1, :] * buf[lo + j:lo + j + rows, :]
    return y


def _inproj_kernel(x_ref, g_ref, w_ref, main_ref, small_ref, fk_ref, fv_ref, qkv_ref, h_scr, *,
                   n_main, n_fox, seg):
    h_scr[...] = _rms(x_ref[...], g_ref[...]).astype(BF16)

    def cols(c0, width):
        return jnp.dot(h_scr[...], w_ref[:, c0:c0 + width], preferred_element_type=F32)

    for c0 in range(0, n_main, seg):
        main_ref[:, c0:c0 + seg] = cols(c0, seg)
    for j in range(3):
        for c in range(0, n_fox, seg):
            p = cols(n_main + j * n_fox + c, seg)
            qkv_ref[:, j * n_fox + c:j * n_fox + c + seg] = p.astype(BF16)
            if j == 1:
                fk_ref[:, c:c + seg] = p
            if j == 2:
                fv_ref[:, c:c + seg] = p
    small_ref[...] = cols(n_main + 3 * n_fox, GATE_COLS)


def _inproj(x2d, g, w, *, n_main, n_fox, tm):
    m, d = x2d.shape
    n_all = w.shape[1]
    row = lambda i: (i, 0)
    fixed = lambda i: (0, 0)
    return pl.pallas_call(
        functools.partial(_inproj_kernel, n_main=n_main, n_fox=n_fox, seg=512),
        grid=(m // tm,),
        in_specs=[pl.BlockSpec((tm, d), row), _resident((1, d), fixed), _resident((d, n_all), fixed)],
        out_specs=[pl.BlockSpec((tm, n_main), row), pl.BlockSpec((tm, GATE_COLS), row),
                   pl.BlockSpec((tm, n_fox), row), pl.BlockSpec((tm, n_fox), row),
                   pl.BlockSpec((tm, 3 * n_fox), row)],
        out_shape=[jax.ShapeDtypeStruct((m, n_main), F32), jax.ShapeDtypeStruct((m, GATE_COLS), F32),
                   jax.ShapeDtypeStruct((m, n_fox), F32), jax.ShapeDtypeStruct((m, n_fox), F32),
                   jax.ShapeDtypeStruct((m, 3 * n_fox), BF16)],
        scratch_shapes=[pltpu.VMEM((tm, d), BF16)],
        compiler_params=_params(1), name="inproj")(x2d, g, w)


def _gate_kernel(sm_ref, b_ref, logf_ref, *rest, heads, cumulative):
    ls = jax.nn.log_sigmoid(sm_ref[0] + b_ref[...])
    logf_ref[0] = ls[:, GATE_F:GATE_F + heads]
    if cumulative:
        ft_ref, = rest
        ft_ref[0] = _cumsum_lanes(ls.T[GATE_F:GATE_F + heads, :])


def _gate(small3d, bias_row, *, heads, cumulative):
    b, rows, _ = small3d.shape
    out_specs = [pl.BlockSpec((1, rows, heads), lambda i: (i, 0, 0))]
    out_shape = [jax.ShapeDtypeStruct((b, rows, heads), F32)]
    if cumulative:
        out_specs.append(pl.BlockSpec((1, heads, rows), lambda i: (i, 0, 0)))
        out_shape.append(jax.ShapeDtypeStruct((b, heads, rows), F32))
    return pl.pallas_call(
        functools.partial(_gate_kernel, heads=heads, cumulative=cumulative),
        grid=(b,),
        in_specs=[pl.BlockSpec((1, rows, GATE_COLS), lambda i: (i, 0, 0)),
                  pl.BlockSpec((1, GATE_COLS), lambda i: (0, 0))],
        out_specs=out_specs, out_shape=out_shape,
        compiler_params=_params(1), name="fox_gate")(small3d, bias_row)


def _unit_lower_inverse(a):
    c = a.shape[0]
    n = -a
    row = lax.broadcasted_iota(jnp.int32, (c, c), 0)
    col = lax.broadcasted_iota(jnp.int32, (c, c), 1)
    t = jnp.where(row == col, 1.0, 0.0) + n
    p = n
    for _ in range(int(math.log2(c)) - 1):
        p = _dot_f32(p, p)
        t = t + _dot_f32(t, p)
    return t


def _gdn_kernel(qkv_ref, z_ref, sm_ref, conv0_ref, s0_ref, cw_ref, alog_ref, dt_ref, ng_ref,
                o_ref, s_ref, conv_ref, buf, *, c, heads, conv_w, beta_col, a_col):
    ci = pl.program_id(1)
    width = heads * HEAD_DIM
    hist = conv_w - 1
    lo = SUBLANES - hist

    @pl.when(ci == 0)
    def _():
        buf[lo:SUBLANES, :] = conv0_ref[0]
        s_ref[0] = s0_ref[0]

    y = _causal_conv(buf, qkv_ref[0], cw_ref, conv_w, c)
    new_hist = buf[lo + c:SUBLANES + c, :]
    conv_ref[0] = new_hist
    buf[lo:SUBLANES, :] = new_hist
    y = jax.nn.silu(y)

    sm = sm_ref[0]
    beta = jax.nn.sigmoid(sm)
    g = -jnp.exp(alog_ref[...]) * jax.nn.softplus(sm + dt_ref[...])
    gcum = _cumsum_rows(g)
    gcum_t = jnp.concatenate([gcum, jnp.zeros((LANES - c, GATE_COLS), F32)], axis=0).T

    row = lax.broadcasted_iota(jnp.int32, (c, c), 0)
    col = lax.broadcasted_iota(jnp.int32, (c, c), 1)
    z = z_ref[0]
    for h in range(heads):
        q = y[:, h * HEAD_DIM:(h + 1) * HEAD_DIM]
        k = y[:, width + h * HEAD_DIM:width + (h + 1) * HEAD_DIM]
        v = y[:, 2 * width + h * HEAD_DIM:2 * width + (h + 1) * HEAD_DIM]
        q = q * lax.rsqrt(jnp.sum(q * q, axis=-1, keepdims=True) + EPS) * HEAD_DIM ** -0.5
        k = k * lax.rsqrt(jnp.sum(k * k, axis=-1, keepdims=True) + EPS)
        bh = beta[:, beta_col + h:beta_col + h + 1]
        gc = gcum[:, a_col + h:a_col + h + 1]
        gr = gcum_t[a_col + h:a_col + h + 1, :c]
        decay = jnp.exp(jnp.where(row >= col, gc - gr, NEG_INF))
        d_strict = jnp.where(row > col, decay, 0.0)
        kb = k * bh
        eg = jnp.exp(gc)
        m = _dot_nt(kb, k) * d_strict
        t = _unit_lower_inverse(m)
        sol = _dot_f32(t, jnp.concatenate([v * bh, kb * eg], axis=-1))
        u, w = sol[:, :HEAD_DIM], sol[:, HEAD_DIM:]
        s = s_ref[0, h]
        delta = u - _dot(w, s)
        o = _dot(q * eg, s) + _dot(_dot_nt(q, k) * decay, delta)
        g_last = gc[c - 1:c, :]
        s_ref[0, h] = jnp.exp(g_last) * s + _dot_tn(k * jnp.exp(g_last - gc), delta)
        zh = z[:, h * HEAD_DIM:(h + 1) * HEAD_DIM]
        o_ref[0, :, h * HEAD_DIM:(h + 1) * HEAD_DIM] = (
            _rms(o, ng_ref[...]) * jax.nn.silu(zh)).astype(o_ref.dtype)


def _gdn(main3d, small3d, conv0, s0, conv_w, alog_row, dt_row, norm_g, *, c, heads, beta_col, a_col):
    b, l, _ = main3d.shape
    width = heads * HEAD_DIM
    kw = conv_w.shape[0]
    fixed2 = lambda i, j: (0, 0)
    return pl.pallas_call(
        functools.partial(_gdn_kernel, c=c, heads=heads, conv_w=kw, beta_col=beta_col, a_col=a_col),
        grid=(b, l // c),
        in_specs=[pl.BlockSpec((1, c, 3 * width), lambda i, j: (i, j, 0)),
                  pl.BlockSpec((1, c, width), lambda i, j: (i, j, 3)),
                  pl.BlockSpec((1, c, GATE_COLS), lambda i, j: (i, j, 0)),
                  pl.BlockSpec((1, kw - 1, 3 * width), lambda i, j: (i, 0, 0)),
                  pl.BlockSpec((1, heads, HEAD_DIM, HEAD_DIM), lambda i, j: (i, 0, 0, 0)),
                  pl.BlockSpec((kw, 3 * width), fixed2),
                  pl.BlockSpec((1, GATE_COLS), fixed2), pl.BlockSpec((1, GATE_COLS), fixed2),
                  pl.BlockSpec((1, HEAD_DIM), fixed2)],
        out_specs=[pl.BlockSpec((1, c, width), lambda i, j: (i, j, 0)),
                   pl.BlockSpec((1, heads, HEAD_DIM, HEAD_DIM), lambda i, j: (i, 0, 0, 0)),
                   pl.BlockSpec((1, kw - 1, 3 * width), lambda i, j: (i, 0, 0))],
        out_shape=[jax.ShapeDtypeStruct((b, l, width), BF16),
                   jax.ShapeDtypeStruct((b, heads, HEAD_DIM, HEAD_DIM), F32),
                   jax.ShapeDtypeStruct((b, kw - 1, 3 * width), F32)],
        scratch_shapes=[pltpu.VMEM((SUBLANES + c, 3 * width), F32)],
        compiler_params=_params(2), name="gdn")(
            main3d, main3d, small3d, conv0, s0, conv_w, alog_row, dt_row, norm_g)


def _fox_prompt_kernel(q_ref, k_ref, v_ref, ft_ref, g_ref, o_ref, vaug, m_scr, acc_scr, s_a, s_b,
                       *, tq):
    qi = pl.program_id(2)

    @pl.when(qi == 0)
    def _():
        vaug[:, :HEAD_DIM] = v_ref[0]
        vaug[:, HEAD_DIM:] = jnp.ones((vaug.shape[0], HEAD_DIM), vaug.dtype)

    m_scr[...] = jnp.full(m_scr.shape, NEG_INF, F32)
    acc_scr[...] = jnp.zeros(acc_scr.shape, F32)
    scale = HEAD_DIM ** -0.5
    q = q_ref[0]

    def scores(buf, j):
        start = pl.multiple_of(j * tq, tq)
        buf[...] = _dot_nt(q, k_ref[0, pl.ds(start, tq), :])

    def accumulate(buf, j, masked):
        start = pl.multiple_of(j * tq, tq)
        s = buf[...] * (scale * LOG2E) - ft_ref[0, 0, pl.ds(j, 1), :] * LOG2E
        if masked:
            row = lax.broadcasted_iota(jnp.int32, (tq, tq), 0)
            col = lax.broadcasted_iota(jnp.int32, (tq, tq), 1)
            s = jnp.where(col <= row, s, NEG_INF)
        m_prev = m_scr[...]
        m_new = jnp.maximum(m_prev, jnp.max(s, axis=-1, keepdims=True))
        alpha = jnp.exp2(m_prev - m_new)
        p = jnp.exp2(s - pltpu.repeat(m_new, tq // LANES, axis=1))
        pv = _dot(p, vaug[pl.ds(start, tq), :])
        acc_scr[...] = pltpu.repeat(alpha, 2 * HEAD_DIM // LANES, axis=1) * acc_scr[...] + pv
        m_scr[...] = m_new

    scores(s_a, 0)

    def pair(jj, carry):
        j = 2 * jj
        scores(s_b, j + 1)
        accumulate(s_a, j, False)
        scores(s_a, j + 2)
        accumulate(s_b, j + 1, False)
        return carry

    lax.fori_loop(0, qi // 2, pair, 0)

    @pl.when(qi % 2 == 0)
    def _():
        accumulate(s_a, qi, True)

    @pl.when(qi % 2 == 1)
    def _():
        scores(s_b, qi)
        accumulate(s_a, qi - 1, False)
        accumulate(s_b, qi, True)

    o = acc_scr[:, :HEAD_DIM] / acc_scr[:, HEAD_DIM:]
    o_ref[0] = _rms(o, g_ref[0]).astype(o_ref.dtype)


def _fox_prompt(qkv3d, ft, norm_g, *, heads, tq):
    b, l, _ = qkv3d.shape
    ft4 = ft.reshape(b, heads, l // tq, tq)
    g3 = norm_g.reshape(heads, 1, HEAD_DIM)
    return pl.pallas_call(
        functools.partial(_fox_prompt_kernel, tq=tq),
        grid=(b, heads, l // tq),
        in_specs=[pl.BlockSpec((1, tq, HEAD_DIM), lambda i, h, j: (i, j, h)),
                  pl.BlockSpec((1, l, HEAD_DIM), lambda i, h, j: (i, 0, heads + h)),
                  pl.BlockSpec((1, l, HEAD_DIM), lambda i, h, j: (i, 0, 2 * heads + h)),
                  pl.BlockSpec((1, 1, l // tq, tq), lambda i, h, j: (i, h, 0, 0)),
                  pl.BlockSpec((1, 1, HEAD_DIM), lambda i, h, j: (h, 0, 0))],
        out_specs=pl.BlockSpec((1, tq, HEAD_DIM), lambda i, h, j: (i, j, h)),
        out_shape=jax.ShapeDtypeStruct((b, l, heads * HEAD_DIM), BF16),
        scratch_shapes=[pltpu.VMEM((l, 2 * HEAD_DIM), BF16), pltpu.VMEM((tq, LANES), F32),
                        pltpu.VMEM((tq, 2 * HEAD_DIM), F32), pltpu.VMEM((tq, tq), F32),
                        pltpu.VMEM((tq, tq), F32)],
        compiler_params=_params(3), name="fox_prompt")(qkv3d, qkv3d, qkv3d, ft4, g3)


def _fox_sample_kernel(q_ref, kn_ref, vn_ref, ck_ref, cv_ref, clt_ref, lnt_ref, g_ref, o_ref, *, ls):
    p_len = ck_ref.shape[2]
    scale = HEAD_DIM ** -0.5
    q = q_ref[0]
    fc = _cumsum_lanes(clt_ref[0, 0])
    fn = fc[:, p_len - 1:p_len] + _cumsum_lanes(lnt_ref[0, 0])[:, :ls]
    s_c = _dot_nt(q, ck_ref[0, 0]) * scale - fc
    s_n = _dot_nt(q, kn_ref[0]) * scale - fn
    row = lax.broadcasted_iota(jnp.int32, (ls, ls), 0)
    col = lax.broadcasted_iota(jnp.int32, (ls, ls), 1)
    s_n = jnp.where(col <= row, s_n, NEG_INF)
    m = jnp.maximum(jnp.max(s_c, axis=-1, keepdims=True), jnp.max(s_n, axis=-1, keepdims=True))
    p_c = jnp.exp(s_c - m)
    p_n = jnp.exp(s_n - m)
    denom = jnp.sum(p_c, axis=-1, keepdims=True) + jnp.sum(p_n, axis=-1, keepdims=True)
    o = (_dot(p_c, cv_ref[0, 0]) + _dot(p_n, vn_ref[0])) / denom
    o_ref[0] = _rms(o, g_ref[0]).astype(o_ref.dtype)


def _fox_sample(qkv3d, cache_k, cache_v, cache_logf_t, logf_new_t, norm_g, *, heads, layer):
    b, ls, _ = qkv3d.shape
    depth, _, p_len = cache_k.shape[:3]
    ck = cache_k.reshape(depth * b, 1, p_len, heads * HEAD_DIM)
    cv = cache_v.reshape(depth * b, 1, p_len, heads * HEAD_DIM)
    cache_idx = lambda i, h: (layer * b + i, 0, 0, h)
    g3 = norm_g.reshape(heads, 1, HEAD_DIM)
    return pl.pallas_call(
        functools.partial(_fox_sample_kernel, ls=ls),
        grid=(b, heads),
        in_specs=[pl.BlockSpec((1, ls, HEAD_DIM), lambda i, h: (i, 0, h)),
                  pl.BlockSpec((1, ls, HEAD_DIM), lambda i, h: (i, 0, heads + h)),
                  pl.BlockSpec((1, ls, HEAD_DIM), lambda i, h: (i, 0, 2 * heads + h)),
                  pl.BlockSpec((1, 1, p_len, HEAD_DIM), cache_idx),
                  pl.BlockSpec((1, 1, p_len, HEAD_DIM), cache_idx),
                  pl.BlockSpec((1, 1, 1, p_len), lambda i, h: (i, h, 0, 0)),
                  pl.BlockSpec((1, 1, 1, LANES), lambda i, h: (i, h, 0, 0)),
                  pl.BlockSpec((1, 1, HEAD_DIM), lambda i, h: (h, 0, 0))],
        out_specs=pl.BlockSpec((1, ls, HEAD_DIM), lambda i, h: (i, 0, h)),
        out_shape=jax.ShapeDtypeStruct((b, ls, heads * HEAD_DIM), BF16),
        compiler_params=_params(2), name="fox_sample")(
            qkv3d, qkv3d, qkv3d, ck, cv, cache_logf_t, logf_new_t, g3)


def _lru_kernel(x_ref, gate_ref, conv0_ref, h0_ref, cw_ref, cb_ref, wa_ref, wx_ref, ba_ref, bx_ref,
                lam_ref, ng_ref, o_ref, h_ref, conv_ref, buf, a_scr, hs_scr, *, tl, blocks, conv_w):
    ti = pl.program_id(1)
    hist = conv_w - 1
    lo = SUBLANES - hist
    bw = x_ref.shape[2] // blocks

    @pl.when(ti == 0)
    def _():
        buf[lo:SUBLANES, :] = conv0_ref[0]
        h_ref[0] = h0_ref[0]

    xc = _causal_conv(buf, x_ref[0], cw_ref, conv_w, tl) + cb_ref[...]
    new_hist = buf[lo + tl:SUBLANES + tl, :]
    conv_ref[0] = new_hist
    buf[lo:SUBLANES, :] = new_hist

    for n in range(blocks):
        sl = slice(n * bw, (n + 1) * bw)
        xb = xc[:, sl]
        r = jax.nn.sigmoid(_dot(xb, wa_ref[n]) + ba_ref[:, sl])
        i = jax.nn.sigmoid(_dot(xb, wx_ref[n]) + bx_ref[:, sl])
        log_a = -LRU_C * r * jax.nn.softplus(-lam_ref[:, sl])
        a_scr[:, sl] = jnp.exp(log_a)
        hs_scr[:, sl] = jnp.sqrt(-_expm1(2.0 * log_a)) * (i * xb)

    def step(t, h):
        h = a_scr[pl.ds(t, 1), :] * h + hs_scr[pl.ds(t, 1), :]
        hs_scr[pl.ds(t, 1), :] = h
        return h

    h_ref[0] = lax.fori_loop(0, tl, step, h_ref[0], unroll=8)

    gate = gate_ref[0]
    for n in range(blocks):
        sl = slice(n * bw, (n + 1) * bw)
        o_ref[0, :, sl] = (_rms(hs_scr[:, sl], ng_ref[:, sl])
                           * jax.nn.gelu(gate[:, sl])).astype(o_ref.dtype)


def _lru(main3d, conv0, h0, cw, cb, wa, wx, ba, bx, lam, ng, *, tl, x_block, gate_block):
    b, l, _ = main3d.shape
    blocks, bw, _ = wa.shape
    width = blocks * bw
    kw = cw.shape[0]
    fixed2 = lambda i, j: (0, 0)
    fixed3 = lambda i, j: (0, 0, 0)
    vec = pl.BlockSpec((1, width), fixed2)
    return pl.pallas_call(
        functools.partial(_lru_kernel, tl=tl, blocks=blocks, conv_w=kw),
        grid=(b, l // tl),
        in_specs=[pl.BlockSpec((1, tl, width), lambda i, j: (i, j, x_block)),
                  pl.BlockSpec((1, tl, width), lambda i, j: (i, j, gate_block)),
                  pl.BlockSpec((1, kw - 1, width), lambda i, j: (i, 0, 0)),
                  pl.BlockSpec((1, 1, width), lambda i, j: (i, 0, 0)),
                  pl.BlockSpec((kw, width), fixed2), vec,
                  pl.BlockSpec((blocks, bw, bw), fixed3), pl.BlockSpec((blocks, bw, bw), fixed3),
                  vec, vec, vec, vec],
        out_specs=[pl.BlockSpec((1, tl, width), lambda i, j: (i, j, 0)),
                   pl.BlockSpec((1, 1, width), lambda i, j: (i, 0, 0)),
                   pl.BlockSpec((1, kw - 1, width), lambda i, j: (i, 0, 0))],
        out_shape=[jax.ShapeDtypeStruct((b, l, width), BF16),
                   jax.ShapeDtypeStruct((b, 1, width), F32),
                   jax.ShapeDtypeStruct((b, kw - 1, width), F32)],
        scratch_shapes=[pltpu.VMEM((SUBLANES + tl, width), F32), pltpu.VMEM((tl, width), F32),
                        pltpu.VMEM((tl, width), F32)],
        compiler_params=_params(2), name="rglru")(
            main3d, main3d, conv0, h0, cw, cb, wa, wx, ba, bx, lam, ng)


def _outproj_kernel(x_ref, og_ref, of_ref, ol_ref, w_ref, y_ref):
    wg, wf = og_ref.shape[1], of_ref.shape[1]
    y = x_ref[...] + jnp.dot(og_ref[...], w_ref[0:wg, :], preferred_element_type=F32)
    y = y + jnp.dot(of_ref[...], w_ref[wg:wg + wf, :], preferred_element_type=F32)
    y_ref[...] = y + jnp.dot(ol_ref[...], w_ref[wg + wf:, :], preferred_element_type=F32)


def _outproj(x2d, o_gdn, o_fox, o_lru, w, *, tm):
    m, d = x2d.shape
    row = lambda i: (i, 0)
    return pl.pallas_call(
        _outproj_kernel, grid=(m // tm,),
        in_specs=[pl.BlockSpec((tm, d), row), pl.BlockSpec((tm, o_gdn.shape[1]), row),
                  pl.BlockSpec((tm, o_fox.shape[1]), row), pl.BlockSpec((tm, o_lru.shape[1]), row),
                  _resident(w.shape, lambda i: (0, 0))],
        out_specs=pl.BlockSpec((tm, d), row),
        out_shape=jax.ShapeDtypeStruct((m, d), F32),
        compiler_params=_params(1), name="outproj")(x2d, o_gdn, o_fox, o_lru, w)


def _ffn_kernel(x_ref, g_ref, wg_ref, wv_ref, wd_ref, cw_ref, c0_ref, fg_ref, y_ref, cnew_ref,
                h_scr, gp_scr, *, nb, tiles_per_batch, tm, conv_w, final_norm):
    gi = pl.program_id(0)
    f = pl.program_id(1)
    hist = conv_w - 1
    lo = SUBLANES - hist

    @pl.when(f == 0)
    def _():
        for j in range(nb):
            h_scr[j] = _rms(x_ref[0, j], g_ref[...]).astype(BF16)

    for j in range(nb):
        h = h_scr[j]
        gp = jnp.dot(h, wg_ref[...], preferred_element_type=F32)
        val = jnp.dot(h, wv_ref[...], preferred_element_type=F32)
        if tiles_per_batch == 1:
            gp_scr[lo:SUBLANES, :] = c0_ref[j]
        else:
            first = gi % tiles_per_batch == 0

            @pl.when(first)
            def _():
                gp_scr[lo:SUBLANES, :] = c0_ref[0]

            @pl.when(jnp.logical_not(first))
            def _():
                gp_scr[lo:SUBLANES, :] = cnew_ref[0, f]

        gc = _causal_conv(gp_scr, gp, cw_ref, conv_w, tm)
        cnew_ref[j, f] = gp_scr[lo + tm:SUBLANES + tm, :]
        act = (jax.nn.silu(gc) * val).astype(BF16)
        contrib = jnp.dot(act, wd_ref[...], preferred_element_type=F32)

        @pl.when(f == 0)
        def _():
            y_ref[0, j] = x_ref[0, j] + contrib

        @pl.when(f > 0)
        def _():
            y_ref[0, j] += contrib

    if final_norm:
        @pl.when(f == pl.num_programs(1) - 1)
        def _():
            for j in range(nb):
                y_ref[0, j] = _rms(y_ref[0, j], fg_ref[...])


def _ffn(x3d, g, wg, wv, wd, cw, c0, fg, *, tm, tf, final_norm):
    b, l, d = x3d.shape
    dff = wg.shape[1]
    kw = cw.shape[0]
    tiles_per_batch = l // tm
    nb = b if tiles_per_batch == 1 else 1
    groups = b * l // (nb * tm)
    x4 = x3d.reshape(groups, nb, tm, d)
    nf = dff // tf
    if tiles_per_batch == 1:
        c_idx = lambda i, f: (0, 0, f)
        cnew_idx = lambda i, f: (0, 0, 0, 0)
    else:
        c_idx = lambda i, f: (i // tiles_per_batch, 0, f)
        cnew_idx = lambda i, f: (i // tiles_per_batch, 0, 0, 0)
    y, cnew = pl.pallas_call(
        functools.partial(_ffn_kernel, nb=nb, tiles_per_batch=tiles_per_batch, tm=tm, conv_w=kw,
                          final_norm=final_norm),
        grid=(groups, dff // tf),
        in_specs=[pl.BlockSpec((1, nb, tm, d), lambda i, f: (i, 0, 0, 0)),
                  pl.BlockSpec((1, d), lambda i, f: (0, 0)),
                  pl.BlockSpec((d, tf), lambda i, f: (0, f)),
                  pl.BlockSpec((d, tf), lambda i, f: (0, f)),
                  pl.BlockSpec((tf, d), lambda i, f: (f, 0)),
                  pl.BlockSpec((kw, tf), lambda i, f: (0, f)),
                  pl.BlockSpec((nb, kw - 1, tf), c_idx),
                  pl.BlockSpec((1, d), lambda i, f: (0, 0))],
        out_specs=[pl.BlockSpec((1, nb, tm, d), lambda i, f: (i, 0, 0, 0)),
                   pl.BlockSpec((nb, nf, kw - 1, tf), cnew_idx)],
        out_shape=[jax.ShapeDtypeStruct((groups, nb, tm, d), F32),
                   jax.ShapeDtypeStruct((b, nf, kw - 1, tf), F32)],
        scratch_shapes=[pltpu.VMEM((nb, tm, d), BF16), pltpu.VMEM((SUBLANES + tm, tf), F32)],
        compiler_params=_params(2), name="ffn")(x4, g, wg, wv, wd, cw, c0, fg)
    return y.reshape(b, l, d), jnp.swapaxes(cnew, 1, 2).reshape(b, kw - 1, dff)


def _pad_cols(a, width):
    return jnp.pad(a, ((0, 0), (0, width - a.shape[1])))


def _layer_weights(l, w_in, w_out, ffn_w_up, ffn_w_down, sizes):
    gdn_w, gdn_h, fox_w, fox_h, lru_w = sizes
    w = w_in[l]
    offs = [0]
    for s in (gdn_w, gdn_w, gdn_w, gdn_w, gdn_h, gdn_h, fox_w, fox_w, fox_w, fox_h, lru_w, lru_w):
        offs.append(offs[-1] + s)
    gq, gk, gv, gz, gb, ga, fq, fk, fv, ff, lx, lg = [w[:, offs[i]:offs[i + 1]] for i in range(12)]
    small = _pad_cols(jnp.concatenate([ff, gb, ga], axis=1), GATE_COLS)
    packed = jnp.concatenate([gq, gk, gv, gz, lx, lg, fq, fk, fv, small], axis=1).astype(BF16)
    dff = ffn_w_down.shape[1]
    return dict(w_in=packed, w_out=w_out[l].astype(BF16), w_gate=ffn_w_up[l, :, :dff].astype(BF16),
                w_val=ffn_w_up[l, :, dff:].astype(BF16), w_down=ffn_w_down[l].astype(BF16))


def _run_group(x, cache, states, p, weights, dims, tiles):
    gdn_w, gdn_h, fox_w, fox_h, lru_w = dims
    b, l, d = x.shape
    depth = len(weights)
    n_main = 4 * gdn_w + 2 * lru_w
    gdn_s, gdn_conv, lru_h, lru_conv, ffn_conv = states
    beta_col, a_col = GATE_F + fox_h, GATE_F + fox_h + gdn_h
    outs = [[] for _ in range(8)]
    for li in range(depth):
        wl = weights[li]
        main, small, fk, fv, qkv = _inproj(
            x.reshape(b * l, d), p['norm_mix_g'][li][None], wl['w_in'],
            n_main=n_main, n_fox=fox_w, tm=tiles['proj'])
        main3 = main.reshape(b, l, n_main)
        small3 = small.reshape(b, l, GATE_COLS)
        qkv3 = qkv.reshape(b, l, 3 * fox_w)
        bias_row = _pad_cols(p['fox_f_bias'][li][None], GATE_COLS)
        if cache is None:
            logf, ft = _gate(small3, bias_row, heads=fox_h, cumulative=True)
            o_fox = _fox_prompt(qkv3, ft, p['fox_norm_g'][li], heads=fox_h, tq=tiles['attn'])
        else:
            logf, = _gate(small3, bias_row, heads=fox_h, cumulative=False)
            clt = jnp.swapaxes(cache[2][li], 1, 2)[:, :, None, :]
            lnt = jnp.pad(jnp.swapaxes(logf, 1, 2), ((0, 0), (0, 0), (0, LANES - l)))[:, :, None, :]
            o_fox = _fox_sample(qkv3, cache[0], cache[1], clt, lnt, p['fox_norm_g'][li],
                                heads=fox_h, layer=li)
        row = lambda v, c0: jnp.pad(v[None], ((0, 0), (c0, GATE_COLS - c0 - v.shape[0])))
        o_gdn, s_new, gconv_new = _gdn(
            main3, small3, gdn_conv[li], gdn_s[li], p['gdn_conv_w'][li],
            row(p['gdn_a_log'][li], a_col), row(p['gdn_dt_bias'][li], a_col),
            p['gdn_norm_g'][li][None], c=min(l, CHUNK), heads=gdn_h, beta_col=beta_col, a_col=a_col)
        o_lru, h_new, lconv_new = _lru(
            main3, lru_conv[li], lru_h[li][:, None, :], p['lru_conv_w'][li], p['lru_conv_b'][li][None],
            p['lru_w_a'][li].astype(BF16), p['lru_w_x'][li].astype(BF16), p['lru_b_a'][li][None],
            p['lru_b_x'][li][None], p['lru_lambda'][li][None], p['lru_norm_g'][li][None],
            tl=tiles['lru'], x_block=4 * gdn_w // lru_w, gate_block=4 * gdn_w // lru_w + 1)
        x1 = _outproj(x.reshape(b * l, d), o_gdn.reshape(b * l, gdn_w), o_fox.reshape(b * l, fox_w),
                      o_lru.reshape(b * l, lru_w), wl['w_out'], tm=tiles['proj'])
        x, fconv_new = _ffn(
            x1.reshape(b, l, d), p['norm_ffn_g'][li][None], wl['w_gate'], wl['w_val'], wl['w_down'],
            p['ffn_conv_w'][li], ffn_conv[li], p['final_norm_g'][None],
            tm=tiles['ffn'], tf=512, final_norm=li == depth - 1)
        new = [fk.reshape(b, l, fox_h, HEAD_DIM), fv.reshape(b, l, fox_h, HEAD_DIM), logf, s_new,
               gconv_new, h_new[:, 0, :], lconv_new, fconv_new]
        for lst, arr in zip(outs, new):
            lst.append(arr)
    return x, [jnp.stack(lst) for lst in outs]


def kernel(x_prompt, x_sample, cache_fox_k, cache_fox_v, cache_fox_logf, state_gdn, state_gdn_conv,
           state_lru, state_lru_conv, state_ffn_conv, norm_mix_g, w_in, gdn_conv_w, gdn_a_log,
           gdn_dt_bias, gdn_norm_g, fox_f_bias, fox_norm_g, lru_conv_w, lru_conv_b, lru_w_a, lru_b_a,
           lru_w_x, lru_b_x, lru_lambda, lru_norm_g, w_out, norm_ffn_g, ffn_w_up, ffn_conv_w,
           ffn_w_down, final_norm_g):
    p = dict(norm_mix_g=norm_mix_g, gdn_conv_w=gdn_conv_w, gdn_a_log=gdn_a_log,
             gdn_dt_bias=gdn_dt_bias, gdn_norm_g=gdn_norm_g, fox_f_bias=fox_f_bias,
             fox_norm_g=fox_norm_g, lru_conv_w=lru_conv_w, lru_conv_b=lru_conv_b, lru_w_a=lru_w_a,
             lru_b_a=lru_b_a, lru_w_x=lru_w_x, lru_b_x=lru_b_x, lru_lambda=lru_lambda,
             lru_norm_g=lru_norm_g, norm_ffn_g=norm_ffn_g, ffn_conv_w=ffn_conv_w,
             final_norm_g=final_norm_g)
    depth = w_in.shape[0]
    bp, lp, _ = x_prompt.shape
    gdn_h = state_gdn.shape[2]
    fox_h = cache_fox_k.shape[3]
    lru_w = state_lru.shape[2]
    dims = (gdn_h * HEAD_DIM, gdn_h, fox_h * HEAD_DIM, fox_h, lru_w)
    dff = ffn_w_down.shape[1]
    weights = [_layer_weights(l, w_in, w_out, ffn_w_up, ffn_w_down, dims) for l in range(depth)]

    zeros = lambda *s: jnp.zeros((depth, bp) + s, F32)
    prompt_states = (zeros(gdn_h, HEAD_DIM, HEAD_DIM), zeros(gdn_conv_w.shape[1] - 1, 3 * dims[0]),
                     zeros(lru_w), zeros(lru_conv_w.shape[1] - 1, lru_w),
                     zeros(ffn_conv_w.shape[1] - 1, dff))
    prompt_tiles = dict(proj=min(256, bp * lp), attn=min(512, lp), lru=min(256, lp), ffn=min(512, lp))
    y_prompt, sp = _run_group(x_prompt, None, prompt_states, p, weights, dims, prompt_tiles)

    bs, ls, _ = x_sample.shape
    sample_tiles = dict(proj=bs * ls, lru=ls, ffn=ls)
    y_sample, ss = _run_group(
        x_sample, (cache_fox_k, cache_fox_v, cache_fox_logf),
        (state_gdn, state_gdn_conv, state_lru, state_lru_conv, state_ffn_conv), p, weights, dims,
        sample_tiles)
    return (y_prompt, y_sample, *sp, *ss)
```

```python
import functools
import math

import jax
import jax.numpy as jnp
from jax import lax
from jax.experimental import pallas as pl
from jax.experimental.pallas import tpu as pltpu

F32 = jnp.float32
BF16 = jnp.bfloat16

HEAD_DIM = 128
CHUNK = 64
LRU_C = 8.0
EPS = 1e-6
LANES = 128
SUBLANES = 8
VMEM_LIMIT = 56 * 1024 * 1024

GATE_COLS = LANES
GATE_F = 0
FFN_SPLIT = 2
NEG_INF = float("-inf")
LOG2E = math.log2(math.e)


def _params(n_axes):
    return pltpu.CompilerParams(dimension_semantics=("arbitrary",) * n_axes,
                                vmem_limit_bytes=VMEM_LIMIT)


def _resident(block_shape, index_map):
    return pl.BlockSpec(block_shape, index_map, pipeline_mode=pl.Buffered(1))


def _dot(a, b):
    return jnp.dot(a.astype(BF16), b.astype(BF16), preferred_element_type=F32)


def _dot_nt(a, b):
    return lax.dot_general(a.astype(BF16), b.astype(BF16), (((1,), (1,)), ((), ())),
                           preferred_element_type=F32)


def _dot_tn(a, b):
    return lax.dot_general(a.astype(BF16), b.astype(BF16), (((0,), (0,)), ((), ())),
                           preferred_element_type=F32)


def _dot_f32(a, b):
    return jnp.dot(a, b, preferred_element_type=F32, precision=lax.Precision.HIGHEST)


def _rms(x, g):
    return x * lax.rsqrt(jnp.mean(x * x, axis=-1, keepdims=True) + EPS) * g


def _expm1(x):
    u = jnp.exp(x)
    um1 = u - 1.0
    near = jnp.where(u == 1.0, x, um1 * x / jnp.log(u))
    return jnp.where(jnp.abs(x) > 0.5, um1, near)


def _cumsum_lanes(x, seg=None):
    n = x.shape[-1]
    lane = lax.broadcasted_iota(jnp.int32, x.shape, x.ndim - 1)
    pos = lane if seg is None else lane % seg
    s = 1
    while s < (n if seg is None else seg):
        x = x + jnp.where(pos >= s, pltpu.roll(x, s, x.ndim - 1), 0.0)
        s *= 2
    return x


def _cumsum_rows(x, seg):
    pos = lax.broadcasted_iota(jnp.int32, x.shape, 0) % seg
    s = 1
    while s < seg:
        x = x + jnp.where(pos >= s, pltpu.roll(x, s, 0), 0.0)
        s *= 2
    return x


def _causal_conv(buf, x, w_ref, width, rows, cols=slice(None)):
    lo = SUBLANES - (width - 1)
    buf[SUBLANES:SUBLANES + rows, cols] = x
    y = w_ref[width - 1:width, cols] * x
    for j in range(width - 1):
        y = y + w_ref[j:j + 1, cols] * buf[lo + j:lo + j + rows, cols]
    return y


def _inproj_kernel(x_ref, g_ref, w_ref, main_ref, small_ref, fk_ref, fv_ref, qkv_ref, h_scr, *,
                   n_main, n_fox, seg):
    h_scr[...] = _rms(x_ref[...], g_ref[...]).astype(BF16)

    def cols(c0, width):
        return jnp.dot(h_scr[...], w_ref[:, c0:c0 + width], preferred_element_type=F32)

    for c0 in range(0, n_main, seg):
        main_ref[:, c0:c0 + seg] = cols(c0, seg)
    for j in range(3):
        for c in range(0, n_fox, seg):
            p = cols(n_main + j * n_fox + c, seg)
            qkv_ref[:, j * n_fox + c:j * n_fox + c + seg] = p.astype(BF16)
            if j == 1:
                fk_ref[:, c:c + seg] = p
            if j == 2:
                fv_ref[:, c:c + seg] = p
    small_ref[...] = cols(n_main + 3 * n_fox, GATE_COLS)


def _inproj(x2d, g, w, *, n_main, n_fox, tm):
    m, d = x2d.shape
    n_all = w.shape[1]
    row = lambda i: (i, 0)
    fixed = lambda i: (0, 0)
    return pl.pallas_call(
        functools.partial(_inproj_kernel, n_main=n_main, n_fox=n_fox, seg=512),
        grid=(m // tm,),
        in_specs=[pl.BlockSpec((tm, d), row), _resident((1, d), fixed), _resident((d, n_all), fixed)],
        out_specs=[pl.BlockSpec((tm, n_main), row), pl.BlockSpec((tm, GATE_COLS), row),
                   pl.BlockSpec((tm, n_fox), row), pl.BlockSpec((tm, n_fox), row),
                   pl.BlockSpec((tm, 3 * n_fox), row)],
        out_shape=[jax.ShapeDtypeStruct((m, n_main), F32), jax.ShapeDtypeStruct((m, GATE_COLS), F32),
                   jax.ShapeDtypeStruct((m, n_fox), F32), jax.ShapeDtypeStruct((m, n_fox), F32),
                   jax.ShapeDtypeStruct((m, 3 * n_fox), BF16)],
        scratch_shapes=[pltpu.VMEM((tm, d), BF16)],
        compiler_params=_params(1), name="inproj")(x2d, g, w)


def _gate_kernel(sm_ref, b_ref, logf_ref, *rest, heads, cumulative):
    ls = jax.nn.log_sigmoid(sm_ref[0] + b_ref[...])
    logf_ref[0] = ls[:, GATE_F:GATE_F + heads]
    if cumulative:
        ft_ref, = rest
        ft_ref[0] = _cumsum_lanes(ls.T[GATE_F:GATE_F + heads, :])


def _gate(small3d, bias_row, *, heads, cumulative):
    b, rows, _ = small3d.shape
    out_specs = [pl.BlockSpec((1, rows, heads), lambda i: (i, 0, 0))]
    out_shape = [jax.ShapeDtypeStruct((b, rows, heads), F32)]
    if cumulative:
        out_specs.append(pl.BlockSpec((1, heads, rows), lambda i: (i, 0, 0)))
        out_shape.append(jax.ShapeDtypeStruct((b, heads, rows), F32))
    return pl.pallas_call(
        functools.partial(_gate_kernel, heads=heads, cumulative=cumulative),
        grid=(b,),
        in_specs=[pl.BlockSpec((1, rows, GATE_COLS), lambda i: (i, 0, 0)),
                  pl.BlockSpec((1, GATE_COLS), lambda i: (0, 0))],
        out_specs=out_specs, out_shape=out_shape,
        compiler_params=_params(1), name="fox_gate")(small3d, bias_row)


def _split_bf16(x):
    hi = x.astype(BF16)
    return hi, (x - hi.astype(F32)).astype(BF16)


def _dot_split(a, b):
    a_hi, a_lo = _split_bf16(a)
    b_hi, b_lo = _split_bf16(b)
    mm = lambda x, y: jnp.dot(x, y, preferred_element_type=F32)
    return mm(a_hi, b_hi) + (mm(a_lo, b_hi) + mm(a_hi, b_lo))


def _unit_lower_solve(a, rhs, c):
    r = a[0].shape[0]
    row = lax.broadcasted_iota(jnp.int32, (r, r), 0)
    col = lax.broadcasted_iota(jnp.int32, (r, r), 1)
    eye = jnp.where(row == col, 1.0, 0.0)
    p = [-x for x in a]
    t = [eye + x for x in p]
    for _ in range(int(math.log2(c)) - 1):
        p = [_dot(x, x) for x in p]
        t = [ti + _dot(ti, pi) for ti, pi in zip(t, p)]
    x = [_dot(ti, bi) for ti, bi in zip(t, rhs)]
    resid = [bi - xi - _dot_split(ai, xi) for ai, bi, xi in zip(a, rhs, x)]
    return [xi + _dot(ti, ri) for ti, xi, ri in zip(t, x, resid)]


def _gdn_kernel(qkv_ref, z_ref, sm_ref, conv0_ref, s0_ref, cw_ref, alog_ref, dt_ref, ng_ref,
                o_ref, s_ref, conv_ref, buf, *, rows, c, heads, conv_w, beta_col, a_col):
    ti = pl.program_id(1)
    nc = rows // c
    width = heads * HEAD_DIM
    hist = conv_w - 1
    lo = SUBLANES - hist

    @pl.when(ti == 0)
    def _():
        buf[lo:SUBLANES, :] = conv0_ref[0]
        s_ref[0] = s0_ref[0]

    y = _causal_conv(buf, qkv_ref[0], cw_ref, conv_w, rows)
    new_hist = buf[lo + rows:SUBLANES + rows, :]
    conv_ref[0] = new_hist
    buf[lo:SUBLANES, :] = new_hist
    y = jax.nn.silu(y)

    sm = sm_ref[0]
    beta = jax.nn.sigmoid(sm)
    g = -jnp.exp(alog_ref[...]) * jax.nn.softplus(sm + dt_ref[...])
    gcum = _cumsum_rows(g, c)
    g_end = jnp.concatenate(
        [jnp.broadcast_to(gcum[(i + 1) * c - 1:(i + 1) * c, :], (c, GATE_COLS)) for i in range(nc)],
        axis=0)
    pad = max(LANES - rows, 0)
    gcum_t = jnp.concatenate([gcum, jnp.zeros((pad, GATE_COLS), F32)], axis=0).T if pad else gcum.T

    row = lax.broadcasted_iota(jnp.int32, (rows, rows), 0)
    col = lax.broadcasted_iota(jnp.int32, (rows, rows), 1)
    lower = row >= col
    if nc > 1:
        lower = lower & (row // c == col // c)
    strict = lower & (row != col)
    z = z_ref[0]
    hs = range(heads)
    head = lambda base, h: y[:, base + h * HEAD_DIM:base + (h + 1) * HEAD_DIM]
    l2 = lambda t: t * lax.rsqrt(jnp.sum(t * t, axis=-1, keepdims=True) + EPS)
    q = [l2(head(0, h)) * HEAD_DIM ** -0.5 for h in hs]
    k = [l2(head(width, h)) for h in hs]
    v = [head(2 * width, h) for h in hs]
    bh = [beta[:, beta_col + h:beta_col + h + 1] for h in hs]
    gc = [gcum[:, a_col + h:a_col + h + 1] for h in hs]
    ge = [g_end[:, a_col + h:a_col + h + 1] for h in hs]
    gr = [gcum_t[a_col + h:a_col + h + 1, :rows] for h in hs]
    decay = [jnp.exp(jnp.where(lower, gc[h] - gr[h], NEG_INF)) for h in hs]
    kb = [k[h] * bh[h] for h in hs]
    eg = [jnp.exp(gc[h]) for h in hs]
    a = [_dot_nt(kb[h], k[h]) * jnp.where(strict, decay[h], 0.0) for h in hs]
    attn = [_dot_nt(q[h], k[h]) * decay[h] for h in hs]
    sol = _unit_lower_solve(a, [jnp.concatenate([v[h] * bh[h], kb[h] * eg[h]], axis=-1) for h in hs], c)
    qe = [q[h] * eg[h] for h in hs]
    kx = [k[h] * jnp.exp(ge[h] - gc[h]) for h in hs]
    s = [s_ref[0, h] for h in hs]
    deltas = [[] for _ in hs]
    inter = [[] for _ in hs]
    for i in range(nc):
        rs = slice(i * c, (i + 1) * c)
        d_i = [sol[h][rs, :HEAD_DIM] - _dot(sol[h][rs, HEAD_DIM:], s[h]) for h in hs]
        for h in hs:
            inter[h].append(_dot(qe[h][rs], s[h]))
            deltas[h].append(d_i[h])
        s = [jnp.exp(ge[h][(i + 1) * c - 1:(i + 1) * c, :]) * s[h] + _dot_tn(kx[h][rs], d_i[h])
             for h in hs]
    cat = lambda parts: parts[0] if nc == 1 else jnp.concatenate(parts, axis=0)
    for h in hs:
        s_ref[0, h] = s[h]
        o = cat(inter[h]) + _dot(attn[h], cat(deltas[h]))
        zh = z[:, h * HEAD_DIM:(h + 1) * HEAD_DIM]
        o_ref[0, :, h * HEAD_DIM:(h + 1) * HEAD_DIM] = (
            _rms(o, ng_ref[...]) * jax.nn.silu(zh)).astype(o_ref.dtype)


def _gdn(main3d, small3d, conv0, s0, conv_w, alog_row, dt_row, norm_g, *, rows, c, heads, beta_col,
         a_col):
    b, l, _ = main3d.shape
    width = heads * HEAD_DIM
    kw = conv_w.shape[0]
    fixed2 = lambda i, j: (0, 0)
    return pl.pallas_call(
        functools.partial(_gdn_kernel, rows=rows, c=c, heads=heads, conv_w=kw, beta_col=beta_col,
                          a_col=a_col),
        grid=(b, l // rows),
        in_specs=[pl.BlockSpec((1, rows, 3 * width), lambda i, j: (i, j, 0)),
                  pl.BlockSpec((1, rows, width), lambda i, j: (i, j, 3)),
                  pl.BlockSpec((1, rows, GATE_COLS), lambda i, j: (i, j, 0)),
                  pl.BlockSpec((1, kw - 1, 3 * width), lambda i, j: (i, 0, 0)),
                  pl.BlockSpec((1, heads, HEAD_DIM, HEAD_DIM), lambda i, j: (i, 0, 0, 0)),
                  pl.BlockSpec((kw, 3 * width), fixed2),
                  pl.BlockSpec((1, GATE_COLS), fixed2), pl.BlockSpec((1, GATE_COLS), fixed2),
                  pl.BlockSpec((1, HEAD_DIM), fixed2)],
        out_specs=[pl.BlockSpec((1, rows, width), lambda i, j: (i, j, 0)),
                   pl.BlockSpec((1, heads, HEAD_DIM, HEAD_DIM), lambda i, j: (i, 0, 0, 0)),
                   pl.BlockSpec((1, kw - 1, 3 * width), lambda i, j: (i, 0, 0))],
        out_shape=[jax.ShapeDtypeStruct((b, l, width), BF16),
                   jax.ShapeDtypeStruct((b, heads, HEAD_DIM, HEAD_DIM), F32),
                   jax.ShapeDtypeStruct((b, kw - 1, 3 * width), F32)],
        scratch_shapes=[pltpu.VMEM((SUBLANES + rows, 3 * width), F32)],
        compiler_params=_params(2), name="gdn")(
            main3d, main3d, small3d, conv0, s0, conv_w, alog_row, dt_row, norm_g)


def _fox_prompt_kernel(q_ref, k_ref, v_ref, ft_ref, g_ref, o_ref, vaug, m_scr, acc_scr, s_a, s_b,
                       *, tq):
    qi = pl.program_id(2)

    @pl.when(qi == 0)
    def _():
        vaug[:, :HEAD_DIM] = v_ref[0]
        vaug[:, HEAD_DIM:] = jnp.ones((vaug.shape[0], HEAD_DIM), vaug.dtype)

    m_scr[...] = jnp.full(m_scr.shape, NEG_INF, F32)
    acc_scr[...] = jnp.zeros(acc_scr.shape, F32)
    scale = HEAD_DIM ** -0.5
    q = q_ref[0]

    def scores(buf, j):
        start = pl.multiple_of(j * tq, tq)
        buf[...] = _dot_nt(q, k_ref[0, pl.ds(start, tq), :])

    def accumulate(buf, j, masked):
        start = pl.multiple_of(j * tq, tq)
        s = buf[...] * (scale * LOG2E) - ft_ref[0, 0, pl.ds(j, 1), :] * LOG2E
        if masked:
            row = lax.broadcasted_iota(jnp.int32, (tq, tq), 0)
            col = lax.broadcasted_iota(jnp.int32, (tq, tq), 1)
            s = jnp.where(col <= row, s, NEG_INF)
        m_prev = m_scr[...]
        m_new = jnp.maximum(m_prev, jnp.max(s, axis=-1, keepdims=True))
        alpha = jnp.exp2(m_prev - m_new)
        p = jnp.exp2(s - jnp.concatenate([m_new] * (tq // LANES), axis=1))
        pv = _dot(p, vaug[pl.ds(start, tq), :])
        acc_scr[...] = jnp.concatenate([alpha] * (2 * HEAD_DIM // LANES), axis=1) * acc_scr[...] + pv
        m_scr[...] = m_new

    scores(s_a, 0)

    def pair(jj, carry):
        j = 2 * jj
        scores(s_b, j + 1)
        accumulate(s_a, j, False)
        scores(s_a, j + 2)
        accumulate(s_b, j + 1, False)
        return carry

    lax.fori_loop(0, qi // 2, pair, 0)

    @pl.when(qi % 2 == 0)
    def _():
        accumulate(s_a, qi, True)

    @pl.when(qi % 2 == 1)
    def _():
        scores(s_b, qi)
        accumulate(s_a, qi - 1, False)
        accumulate(s_b, qi, True)

    o = acc_scr[:, :HEAD_DIM] / acc_scr[:, HEAD_DIM:]
    o_ref[0] = _rms(o, g_ref[0]).astype(o_ref.dtype)


def _fox_prompt(qkv3d, ft, norm_g, *, heads, tq):
    b, l, _ = qkv3d.shape
    ft4 = ft.reshape(b, heads, l // tq, tq)
    g3 = norm_g.reshape(heads, 1, HEAD_DIM)
    return pl.pallas_call(
        functools.partial(_fox_prompt_kernel, tq=tq),
        grid=(b, heads, l // tq),
        in_specs=[pl.BlockSpec((1, tq, HEAD_DIM), lambda i, h, j: (i, j, h)),
                  pl.BlockSpec((1, l, HEAD_DIM), lambda i, h, j: (i, 0, heads + h)),
                  pl.BlockSpec((1, l, HEAD_DIM), lambda i, h, j: (i, 0, 2 * heads + h)),
                  pl.BlockSpec((1, 1, l // tq, tq), lambda i, h, j: (i, h, 0, 0)),
                  pl.BlockSpec((1, 1, HEAD_DIM), lambda i, h, j: (h, 0, 0))],
        out_specs=pl.BlockSpec((1, tq, HEAD_DIM), lambda i, h, j: (i, j, h)),
        out_shape=jax.ShapeDtypeStruct((b, l, heads * HEAD_DIM), BF16),
        scratch_shapes=[pltpu.VMEM((l, 2 * HEAD_DIM), BF16), pltpu.VMEM((tq, LANES), F32),
                        pltpu.VMEM((tq, 2 * HEAD_DIM), F32), pltpu.VMEM((tq, tq), F32),
                        pltpu.VMEM((tq, tq), F32)],
        compiler_params=_params(3), name="fox_prompt")(qkv3d, qkv3d, qkv3d, ft4, g3)


def _fox_sample_kernel(q_ref, kn_ref, vn_ref, ck_ref, cv_ref, clt_ref, lnt_ref, g_ref, o_ref, *, ls):
    p_len = ck_ref.shape[2]
    scale = HEAD_DIM ** -0.5
    q = q_ref[0]
    fc = _cumsum_lanes(clt_ref[0, 0])
    fn = fc[:, p_len - 1:p_len] + _cumsum_lanes(lnt_ref[0, 0])[:, :ls]
    s_c = _dot_nt(q, ck_ref[0, 0]) * scale - fc
    s_n = _dot_nt(q, kn_ref[0]) * scale - fn
    row = lax.broadcasted_iota(jnp.int32, (ls, ls), 0)
    col = lax.broadcasted_iota(jnp.int32, (ls, ls), 1)
    s_n = jnp.where(col <= row, s_n, NEG_INF)
    m = jnp.maximum(jnp.max(s_c, axis=-1, keepdims=True), jnp.max(s_n, axis=-1, keepdims=True))
    p_c = jnp.exp(s_c - m)
    p_n = jnp.exp(s_n - m)
    denom = jnp.sum(p_c, axis=-1, keepdims=True) + jnp.sum(p_n, axis=-1, keepdims=True)
    o = (_dot(p_c, cv_ref[0, 0]) + _dot(p_n, vn_ref[0])) / denom
    o_ref[0] = _rms(o, g_ref[0]).astype(o_ref.dtype)


def _fox_sample(qkv3d, cache_k, cache_v, cache_logf_t, logf_new_t, norm_g, *, heads, layer):
    b, ls, _ = qkv3d.shape
    depth, _, p_len = cache_k.shape[:3]
    ck = cache_k.reshape(depth * b, 1, p_len, heads * HEAD_DIM)
    cv = cache_v.reshape(depth * b, 1, p_len, heads * HEAD_DIM)
    cache_idx = lambda i, h: (layer * b + i, 0, 0, h)
    g3 = norm_g.reshape(heads, 1, HEAD_DIM)
    return pl.pallas_call(
        functools.partial(_fox_sample_kernel, ls=ls),
        grid=(b, heads),
        in_specs=[pl.BlockSpec((1, ls, HEAD_DIM), lambda i, h: (i, 0, h)),
                  pl.BlockSpec((1, ls, HEAD_DIM), lambda i, h: (i, 0, heads + h)),
                  pl.BlockSpec((1, ls, HEAD_DIM), lambda i, h: (i, 0, 2 * heads + h)),
                  pl.BlockSpec((1, 1, p_len, HEAD_DIM), cache_idx),
                  pl.BlockSpec((1, 1, p_len, HEAD_DIM), cache_idx),
                  pl.BlockSpec((1, 1, 1, p_len), lambda i, h: (i, h, 0, 0)),
                  pl.BlockSpec((1, 1, 1, LANES), lambda i, h: (i, h, 0, 0)),
                  pl.BlockSpec((1, 1, HEAD_DIM), lambda i, h: (h, 0, 0))],
        out_specs=pl.BlockSpec((1, ls, HEAD_DIM), lambda i, h: (i, 0, h)),
        out_shape=jax.ShapeDtypeStruct((b, ls, heads * HEAD_DIM), BF16),
        compiler_params=_params(2), name="fox_sample")(
            qkv3d, qkv3d, qkv3d, ck, cv, cache_logf_t, logf_new_t, g3)


def _lru_kernel(x_ref, gate_ref, conv0_ref, h0_ref, cw_ref, cb_ref, wa_ref, wx_ref, ba_ref, bx_ref,
                lam_ref, ng_ref, o_ref, h_ref, conv_ref, buf, a_scr, hs_scr, *, tl, blocks, conv_w):
    ti = pl.program_id(1)
    hist = conv_w - 1
    lo = SUBLANES - hist
    bw = x_ref.shape[2] // blocks

    @pl.when(ti == 0)
    def _():
        buf[lo:SUBLANES, :] = conv0_ref[0]
        h_ref[0] = h0_ref[0]

    xc = _causal_conv(buf, x_ref[0], cw_ref, conv_w, tl) + cb_ref[...]
    new_hist = buf[lo + tl:SUBLANES + tl, :]
    conv_ref[0] = new_hist
    buf[lo:SUBLANES, :] = new_hist

    for n in range(blocks):
        sl = slice(n * bw, (n + 1) * bw)
        xb = xc[:, sl]
        r = jax.nn.sigmoid(_dot(xb, wa_ref[n]) + ba_ref[:, sl])
        i = jax.nn.sigmoid(_dot(xb, wx_ref[n]) + bx_ref[:, sl])
        log_a = -LRU_C * r * jax.nn.softplus(-lam_ref[:, sl])
        a_scr[:, sl] = jnp.exp(log_a)
        hs_scr[:, sl] = jnp.sqrt(-_expm1(2.0 * log_a)) * (i * xb)

    def step(t, h):
        h = a_scr[pl.ds(t, 1), :] * h + hs_scr[pl.ds(t, 1), :]
        hs_scr[pl.ds(t, 1), :] = h
        return h

    h_ref[0] = lax.fori_loop(0, tl, step, h_ref[0], unroll=8)

    gate = gate_ref[0]
    for n in range(blocks):
        sl = slice(n * bw, (n + 1) * bw)
        o_ref[0, :, sl] = (_rms(hs_scr[:, sl], ng_ref[:, sl])
                           * jax.nn.gelu(gate[:, sl])).astype(o_ref.dtype)


def _lru(main3d, conv0, h0, cw, cb, wa, wx, ba, bx, lam, ng, *, tl, x_block, gate_block):
    b, l, _ = main3d.shape
    blocks, bw, _ = wa.shape
    width = blocks * bw
    kw = cw.shape[0]
    fixed2 = lambda i, j: (0, 0)
    fixed3 = lambda i, j: (0, 0, 0)
    vec = pl.BlockSpec((1, width), fixed2)
    return pl.pallas_call(
        functools.partial(_lru_kernel, tl=tl, blocks=blocks, conv_w=kw),
        grid=(b, l // tl),
        in_specs=[pl.BlockSpec((1, tl, width), lambda i, j: (i, j, x_block)),
                  pl.BlockSpec((1, tl, width), lambda i, j: (i, j, gate_block)),
                  pl.BlockSpec((1, kw - 1, width), lambda i, j: (i, 0, 0)),
                  pl.BlockSpec((1, 1, width), lambda i, j: (i, 0, 0)),
                  pl.BlockSpec((kw, width), fixed2), vec,
                  pl.BlockSpec((blocks, bw, bw), fixed3), pl.BlockSpec((blocks, bw, bw), fixed3),
                  vec, vec, vec, vec],
        out_specs=[pl.BlockSpec((1, tl, width), lambda i, j: (i, j, 0)),
                   pl.BlockSpec((1, 1, width), lambda i, j: (i, 0, 0)),
                   pl.BlockSpec((1, kw - 1, width), lambda i, j: (i, 0, 0))],
        out_shape=[jax.ShapeDtypeStruct((b, l, width), BF16),
                   jax.ShapeDtypeStruct((b, 1, width), F32),
                   jax.ShapeDtypeStruct((b, kw - 1, width), F32)],
        scratch_shapes=[pltpu.VMEM((SUBLANES + tl, width), F32), pltpu.VMEM((tl, width), F32),
                        pltpu.VMEM((tl, width), F32)],
        compiler_params=_params(2), name="rglru")(
            main3d, main3d, conv0, h0, cw, cb, wa, wx, ba, bx, lam, ng)


def _outproj_kernel(x_ref, og_ref, of_ref, ol_ref, w_ref, y_ref):
    wg, wf = og_ref.shape[1], of_ref.shape[1]
    y = x_ref[...] + jnp.dot(og_ref[...], w_ref[0:wg, :], preferred_element_type=F32)
    y = y + jnp.dot(of_ref[...], w_ref[wg:wg + wf, :], preferred_element_type=F32)
    y_ref[...] = y + jnp.dot(ol_ref[...], w_ref[wg + wf:, :], preferred_element_type=F32)


def _outproj(x2d, o_gdn, o_fox, o_lru, w, *, tm):
    m, d = x2d.shape
    row = lambda i: (i, 0)
    return pl.pallas_call(
        _outproj_kernel, grid=(m // tm,),
        in_specs=[pl.BlockSpec((tm, d), row), pl.BlockSpec((tm, o_gdn.shape[1]), row),
                  pl.BlockSpec((tm, o_fox.shape[1]), row), pl.BlockSpec((tm, o_lru.shape[1]), row),
                  _resident(w.shape, lambda i: (0, 0))],
        out_specs=pl.BlockSpec((tm, d), row),
        out_shape=jax.ShapeDtypeStruct((m, d), F32),
        compiler_params=_params(1), name="outproj")(x2d, o_gdn, o_fox, o_lru, w)


def _ffn_kernel(x_ref, g_ref, wg_ref, wv_ref, wd_ref, cw_ref, c0_ref, fg_ref, y_ref, cnew_ref,
                h_scr, gp_scr, *, nb, tiles_per_batch, tm, conv_w, final_norm):
    gi = pl.program_id(0)
    f = pl.program_id(1)
    hist = conv_w - 1
    lo = SUBLANES - hist

    @pl.when(f == 0)
    def _():
        for j in range(nb):
            x = x_ref[0, j]
            h_scr[j] = _rms(x, g_ref[...]).astype(BF16)
            y_ref[0, j] = x

    if tiles_per_batch > 1:
        first = gi % tiles_per_batch == 0

        @pl.when(first)
        def _():
            gp_scr[lo:SUBLANES, :] = c0_ref[0]

        @pl.when(jnp.logical_not(first))
        def _():
            gp_scr[lo:SUBLANES, :] = cnew_ref[0, f]

    tf = wg_ref.shape[1]
    half = tf // FFN_SPLIT
    for j in range(nb):
        h = h_scr[j]
        if tiles_per_batch == 1:
            gp_scr[lo:SUBLANES, :] = c0_ref[j]
        ups = []
        for c in range(FFN_SPLIT):
            cs = slice(c * half, (c + 1) * half)
            ups.append((jnp.dot(h, wg_ref[:, cs], preferred_element_type=F32),
                        jnp.dot(h, wv_ref[:, cs], preferred_element_type=F32)))
        contrib = None
        for c in range(FFN_SPLIT):
            cs = slice(c * half, (c + 1) * half)
            gp, val = ups[c]
            gc = _causal_conv(gp_scr, gp, cw_ref, conv_w, tm, cs)
            act = (jax.nn.silu(gc) * val).astype(BF16)
            part = jnp.dot(act, wd_ref[cs, :], preferred_element_type=F32)
            contrib = part if contrib is None else contrib + part
        cnew_ref[j, f] = gp_scr[lo + tm:SUBLANES + tm, :]
        y_ref[0, j] += contrib

    if final_norm:
        @pl.when(f == pl.num_programs(1) - 1)
        def _():
            for j in range(nb):
                y_ref[0, j] = _rms(y_ref[0, j], fg_ref[...])


def _ffn(x3d, g, wg, wv, wd, cw, c0, fg, *, tm, tf, final_norm):
    b, l, d = x3d.shape
    dff = wg.shape[1]
    kw = cw.shape[0]
    tiles_per_batch = l // tm
    nb = b if tiles_per_batch == 1 else 1
    groups = b * l // (nb * tm)
    x4 = x3d.reshape(groups, nb, tm, d)
    nf = dff // tf
    if tiles_per_batch == 1:
        c_idx = lambda i, f: (0, 0, f)
        cnew_idx = lambda i, f: (0, 0, 0, 0)
    else:
        c_idx = lambda i, f: (i // tiles_per_batch, 0, f)
        cnew_idx = lambda i, f: (i // tiles_per_batch, 0, 0, 0)
    y, cnew = pl.pallas_call(
        functools.partial(_ffn_kernel, nb=nb, tiles_per_batch=tiles_per_batch, tm=tm, conv_w=kw,
                          final_norm=final_norm),
        grid=(groups, dff // tf),
        in_specs=[pl.BlockSpec((1, nb, tm, d), lambda i, f: (i, 0, 0, 0)),
                  pl.BlockSpec((1, d), lambda i, f: (0, 0)),
                  pl.BlockSpec((d, tf), lambda i, f: (0, f)),
                  pl.BlockSpec((d, tf), lambda i, f: (0, f)),
                  pl.BlockSpec((tf, d), lambda i, f: (f, 0)),
                  pl.BlockSpec((kw, tf), lambda i, f: (0, f)),
                  pl.BlockSpec((nb, kw - 1, tf), c_idx),
                  pl.BlockSpec((1, d), lambda i, f: (0, 0))],
        out_specs=[pl.BlockSpec((1, nb, tm, d), lambda i, f: (i, 0, 0, 0)),
                   pl.BlockSpec((nb, nf, kw - 1, tf), cnew_idx)],
        out_shape=[jax.ShapeDtypeStruct((groups, nb, tm, d), F32),
                   jax.ShapeDtypeStruct((b, nf, kw - 1, tf), F32)],
        scratch_shapes=[pltpu.VMEM((nb, tm, d), BF16), pltpu.VMEM((SUBLANES + tm, tf), F32)],
        compiler_params=_params(2), name="ffn")(x4, g, wg, wv, wd, cw, c0, fg)
    return y.reshape(b, l, d), jnp.swapaxes(cnew, 1, 2).reshape(b, kw - 1, dff)


def _pad_cols(a, width):
    return jnp.pad(a, ((0, 0), (0, width - a.shape[1])))


def _layer_weights(l, w_in, w_out, ffn_w_up, ffn_w_down, sizes):
    gdn_w, gdn_h, fox_w, fox_h, lru_w = sizes
    w = w_in[l]
    offs = [0]
    for s in (gdn_w, gdn_w, gdn_w, gdn_w, gdn_h, gdn_h, fox_w, fox_w, fox_w, fox_h, lru_w, lru_w):
        offs.append(offs[-1] + s)
    gq, gk, gv, gz, gb, ga, fq, fk, fv, ff, lx, lg = [w[:, offs[i]:offs[i + 1]] for i in range(12)]
    small = _pad_cols(jnp.concatenate([ff, gb, ga], axis=1), GATE_COLS)
    packed = jnp.concatenate([gq, gk, gv, gz, lx, lg, fq, fk, fv, small], axis=1).astype(BF16)
    dff = ffn_w_down.shape[1]
    return dict(w_in=packed, w_out=w_out[l].astype(BF16), w_gate=ffn_w_up[l, :, :dff].astype(BF16),
                w_val=ffn_w_up[l, :, dff:].astype(BF16), w_down=ffn_w_down[l].astype(BF16))


def _run_group(x, cache, states, p, weights, dims, tiles):
    gdn_w, gdn_h, fox_w, fox_h, lru_w = dims
    b, l, d = x.shape
    depth = len(weights)
    n_main = 4 * gdn_w + 2 * lru_w
    gdn_s, gdn_conv, lru_h, lru_conv, ffn_conv = states
    beta_col, a_col = GATE_F + fox_h, GATE_F + fox_h + gdn_h
    outs = [[] for _ in range(8)]
    for li in range(depth):
        wl = weights[li]
        main, small, fk, fv, qkv = _inproj(
            x.reshape(b * l, d), p['norm_mix_g'][li][None], wl['w_in'],
            n_main=n_main, n_fox=fox_w, tm=tiles['proj'])
        main3 = main.reshape(b, l, n_main)
        small3 = small.reshape(b, l, GATE_COLS)
        qkv3 = qkv.reshape(b, l, 3 * fox_w)
        bias_row = _pad_cols(p['fox_f_bias'][li][None], GATE_COLS)
        if cache is None:
            logf, ft = _gate(small3, bias_row, heads=fox_h, cumulative=True)
            o_fox = _fox_prompt(qkv3, ft, p['fox_norm_g'][li], heads=fox_h, tq=tiles['attn'])
        else:
            logf, = _gate(small3, bias_row, heads=fox_h, cumulative=False)
            clt = jnp.swapaxes(cache[2][li], 1, 2)[:, :, None, :]
            lnt = jnp.pad(jnp.swapaxes(logf, 1, 2), ((0, 0), (0, 0), (0, LANES - l)))[:, :, None, :]
            o_fox = _fox_sample(qkv3, cache[0], cache[1], clt, lnt, p['fox_norm_g'][li],
                                heads=fox_h, layer=li)
        row = lambda v, c0: jnp.pad(v[None], ((0, 0), (c0, GATE_COLS - c0 - v.shape[0])))
        o_gdn, s_new, gconv_new = _gdn(
            main3, small3, gdn_conv[li], gdn_s[li], p['gdn_conv_w'][li],
            row(p['gdn_a_log'][li], a_col), row(p['gdn_dt_bias'][li], a_col),
            p['gdn_norm_g'][li][None], rows=tiles['gdn'], c=min(l, CHUNK), heads=gdn_h,
            beta_col=beta_col, a_col=a_col)
        o_lru, h_new, lconv_new = _lru(
            main3, lru_conv[li], lru_h[li][:, None, :], p['lru_conv_w'][li], p['lru_conv_b'][li][None],
            p['lru_w_a'][li].astype(BF16), p['lru_w_x'][li].astype(BF16), p['lru_b_a'][li][None],
            p['lru_b_x'][li][None], p['lru_lambda'][li][None], p['lru_norm_g'][li][None],
            tl=tiles['lru'], x_block=4 * gdn_w // lru_w, gate_block=4 * gdn_w // lru_w + 1)
        x1 = _outproj(x.reshape(b * l, d), o_gdn.reshape(b * l, gdn_w), o_fox.reshape(b * l, fox_w),
                      o_lru.reshape(b * l, lru_w), wl['w_out'], tm=tiles['proj'])
        x, fconv_new = _ffn(
            x1.reshape(b, l, d), p['norm_ffn_g'][li][None], wl['w_gate'], wl['w_val'], wl['w_down'],
            p['ffn_conv_w'][li], ffn_conv[li], p['final_norm_g'][None],
            tm=tiles['ffn'], tf=512, final_norm=li == depth - 1)
        new = [fk.reshape(b, l, fox_h, HEAD_DIM), fv.reshape(b, l, fox_h, HEAD_DIM), logf, s_new,
               gconv_new, h_new[:, 0, :], lconv_new, fconv_new]
        for lst, arr in zip(outs, new):
            lst.append(arr)
    return x, [jnp.stack(lst) for lst in outs]


def kernel(x_prompt, x_sample, cache_fox_k, cache_fox_v, cache_fox_logf, state_gdn, state_gdn_conv,
           state_lru, state_lru_conv, state_ffn_conv, norm_mix_g, w_in, gdn_conv_w, gdn_a_log,
           gdn_dt_bias, gdn_norm_g, fox_f_bias, fox_norm_g, lru_conv_w, lru_conv_b, lru_w_a, lru_b_a,
           lru_w_x, lru_b_x, lru_lambda, lru_norm_g, w_out, norm_ffn_g, ffn_w_up, ffn_conv_w,
           ffn_w_down, final_norm_g):
    p = dict(norm_mix_g=norm_mix_g, gdn_conv_w=gdn_conv_w, gdn_a_log=gdn_a_log,
             gdn_dt_bias=gdn_dt_bias, gdn_norm_g=gdn_norm_g, fox_f_bias=fox_f_bias,
             fox_norm_g=fox_norm_g, lru_conv_w=lru_conv_w, lru_conv_b=lru_conv_b, lru_w_a=lru_w_a,
             lru_b_a=lru_b_a, lru_w_x=lru_w_x, lru_b_x=lru_b_x, lru_lambda=lru_lambda,
             lru_norm_g=lru_norm_g, norm_ffn_g=norm_ffn_g, ffn_conv_w=ffn_conv_w,
             final_norm_g=final_norm_g)
    depth = w_in.shape[0]
    bp, lp, _ = x_prompt.shape
    gdn_h = state_gdn.shape[2]
    fox_h = cache_fox_k.shape[3]
    lru_w = state_lru.shape[2]
    dims = (gdn_h * HEAD_DIM, gdn_h, fox_h * HEAD_DIM, fox_h, lru_w)
    dff = ffn_w_down.shape[1]
    weights = [_layer_weights(l, w_in, w_out, ffn_w_up, ffn_w_down, dims) for l in range(depth)]

    zeros = lambda *s: jnp.zeros((depth, bp) + s, F32)
    prompt_states = (zeros(gdn_h, HEAD_DIM, HEAD_DIM), zeros(gdn_conv_w.shape[1] - 1, 3 * dims[0]),
                     zeros(lru_w), zeros(lru_conv_w.shape[1] - 1, lru_w),
                     zeros(ffn_conv_w.shape[1] - 1, dff))
    prompt_tiles = dict(proj=min(256, bp * lp), attn=min(512, lp), lru=min(256, lp), ffn=min(512, lp),
                        gdn=min(256, lp))
    y_prompt, sp = _run_group(x_prompt, None, prompt_states, p, weights, dims, prompt_tiles)

    bs, ls, _ = x_sample.shape
    sample_tiles = dict(proj=bs * ls, lru=ls, ffn=ls, gdn=ls)
    y_sample, ss = _run_group(
        x_sample, (cache_fox_k, cache_fox_v, cache_fox_logf),
        (state_gdn, state_gdn_conv, state_lru, state_lru_conv, state_ffn_conv), p, weights, dims,
        sample_tiles)
    return (y_prompt, y_sample, *sp, *ss)
```

```python
import functools
import math

import jax
import jax.numpy as jnp
from jax import lax
from jax.experimental import pallas as pl
from jax.experimental.pallas import tpu as pltpu

F32 = jnp.float32
BF16 = jnp.bfloat16

HEAD_DIM = 128
CHUNK = 64
LRU_C = 8.0
EPS = 1e-6
LANES = 128
SUBLANES = 8
VMEM_LIMIT = 56 * 1024 * 1024

GATE_COLS = LANES
GATE_F = 0
FFN_SPLIT = 2
NEG_INF = float("-inf")
LOG2E = math.log2(math.e)


def _params(n_axes):
    return pltpu.CompilerParams(dimension_semantics=("arbitrary",) * n_axes,
                                vmem_limit_bytes=VMEM_LIMIT)


def _resident(block_shape, index_map):
    return pl.BlockSpec(block_shape, index_map, pipeline_mode=pl.Buffered(1))


def _dot(a, b):
    return jnp.dot(a.astype(BF16), b.astype(BF16), preferred_element_type=F32)


def _dot_nt(a, b):
    return lax.dot_general(a.astype(BF16), b.astype(BF16), (((1,), (1,)), ((), ())),
                           preferred_element_type=F32)


def _dot_tn(a, b):
    return lax.dot_general(a.astype(BF16), b.astype(BF16), (((0,), (0,)), ((), ())),
                           preferred_element_type=F32)


def _dot_f32(a, b):
    return jnp.dot(a, b, preferred_element_type=F32, precision=lax.Precision.HIGHEST)


def _rms(x, g):
    return x * lax.rsqrt(jnp.mean(x * x, axis=-1, keepdims=True) + EPS) * g


def _expm1(x):
    u = jnp.exp(x)
    um1 = u - 1.0
    near = jnp.where(u == 1.0, x, um1 * x / jnp.log(u))
    return jnp.where(jnp.abs(x) > 0.5, um1, near)


def _cumsum_lanes(x, seg=None):
    n = x.shape[-1]
    lane = lax.broadcasted_iota(jnp.int32, x.shape, x.ndim - 1)
    pos = lane if seg is None else lane % seg
    s = 1
    while s < (n if seg is None else seg):
        x = x + jnp.where(pos >= s, pltpu.roll(x, s, x.ndim - 1), 0.0)
        s *= 2
    return x


def _cumsum_rows(x, seg):
    pos = lax.broadcasted_iota(jnp.int32, x.shape, 0) % seg
    s = 1
    while s < seg:
        x = x + jnp.where(pos >= s, pltpu.roll(x, s, 0), 0.0)
        s *= 2
    return x


def _causal_conv(buf, x, w_ref, width, rows, cols=slice(None)):
    lo = SUBLANES - (width - 1)
    buf[SUBLANES:SUBLANES + rows, cols] = x
    y = w_ref[width - 1:width, cols] * x
    for j in range(width - 1):
        y = y + w_ref[j:j + 1, cols] * buf[lo + j:lo + j + rows, cols]
    return y


def _inproj_kernel(*refs, n_main, n_fox, seg, n_in):
    x_ref, g_ref, w_ref = refs[:3]
    main_ref, small_ref, fk_ref, fv_ref, qkv_ref, h_scr = refs[n_in:]
    h_scr[...] = _rms(x_ref[...], g_ref[...]).astype(BF16)

    def cols(c0, width):
        return jnp.dot(h_scr[...], w_ref[:, c0:c0 + width], preferred_element_type=F32)

    for c0 in range(0, n_main, seg):
        main_ref[:, c0:c0 + seg] = cols(c0, seg)
    for j in range(3):
        for c in range(0, n_fox, seg):
            p = cols(n_main + j * n_fox + c, seg)
            qkv_ref[:, j * n_fox + c:j * n_fox + c + seg] = p.astype(BF16)
            if j == 1:
                fk_ref[:, c:c + seg] = p
            if j == 2:
                fv_ref[:, c:c + seg] = p
    small_ref[...] = cols(n_main + 3 * n_fox, GATE_COLS)


def _inproj(x2d, g, w, kv_prev, *, n_main, n_fox, tm, layer, depth):
    m, d = x2d.shape
    n_all = w.shape[1]
    row = lambda i: (i, 0)
    kv_row = lambda i: (layer * (m // tm) + i, 0)
    fixed = lambda i: (0, 0)
    in_specs = [pl.BlockSpec((tm, d), row), _resident((1, d), fixed), _resident((d, n_all), fixed)]
    operands = [x2d, g, w]
    aliases = {}
    if kv_prev is not None:
        in_specs += [pl.BlockSpec(memory_space=pl.ANY)] * 2
        operands += list(kv_prev)
        aliases = {3: 2, 4: 3}
    return pl.pallas_call(
        functools.partial(_inproj_kernel, n_main=n_main, n_fox=n_fox, seg=512,
                          n_in=len(operands)),
        grid=(m // tm,),
        in_specs=in_specs,
        out_specs=[pl.BlockSpec((tm, n_main), row), pl.BlockSpec((tm, GATE_COLS), row),
                   pl.BlockSpec((tm, n_fox), kv_row), pl.BlockSpec((tm, n_fox), kv_row),
                   pl.BlockSpec((tm, 3 * n_fox), row)],
        out_shape=[jax.ShapeDtypeStruct((m, n_main), F32), jax.ShapeDtypeStruct((m, GATE_COLS), F32),
                   jax.ShapeDtypeStruct((depth * m, n_fox), F32),
                   jax.ShapeDtypeStruct((depth * m, n_fox), F32),
                   jax.ShapeDtypeStruct((m, 3 * n_fox), BF16)],
        scratch_shapes=[pltpu.VMEM((tm, d), BF16)],
        input_output_aliases=aliases,
        compiler_params=_params(1), name="inproj")(*operands)


def _gate_kernel(sm_ref, b_ref, logf_ref, *rest, heads, cumulative):
    ls = jax.nn.log_sigmoid(sm_ref[0] + b_ref[...])
    logf_ref[0] = ls[:, GATE_F:GATE_F + heads]
    if cumulative:
        ft_ref, = rest
        ft_ref[0] = _cumsum_lanes(ls.T[GATE_F:GATE_F + heads, :])


def _gate(small3d, bias_row, *, heads, cumulative):
    b, rows, _ = small3d.shape
    out_specs = [pl.BlockSpec((1, rows, heads), lambda i: (i, 0, 0))]
    out_shape = [jax.ShapeDtypeStruct((b, rows, heads), F32)]
    if cumulative:
        out_specs.append(pl.BlockSpec((1, heads, rows), lambda i: (i, 0, 0)))
        out_shape.append(jax.ShapeDtypeStruct((b, heads, rows), F32))
    return pl.pallas_call(
        functools.partial(_gate_kernel, heads=heads, cumulative=cumulative),
        grid=(b,),
        in_specs=[pl.BlockSpec((1, rows, GATE_COLS), lambda i: (i, 0, 0)),
                  pl.BlockSpec((1, GATE_COLS), lambda i: (0, 0))],
        out_specs=out_specs, out_shape=out_shape,
        compiler_params=_params(1), name="fox_gate")(small3d, bias_row)


def _split_bf16(x):
    hi = x.astype(BF16)
    return hi, (x - hi.astype(F32)).astype(BF16)


def _dot_split(a, b):
    a_hi, a_lo = _split_bf16(a)
    b_hi, b_lo = _split_bf16(b)
    mm = lambda x, y: jnp.dot(x, y, preferred_element_type=F32)
    return mm(a_hi, b_hi) + (mm(a_lo, b_hi) + mm(a_hi, b_lo))


def _unit_lower_solve(a, rhs, c):
    r = a[0].shape[0]
    row = lax.broadcasted_iota(jnp.int32, (r, r), 0)
    col = lax.broadcasted_iota(jnp.int32, (r, r), 1)
    eye = jnp.where(row == col, 1.0, 0.0)
    p = [-x for x in a]
    t = [eye + x for x in p]
    for _ in range(int(math.log2(c)) - 1):
        p = [_dot(x, x) for x in p]
        t = [ti + _dot(ti, pi) for ti, pi in zip(t, p)]
    x = [_dot(ti, bi) for ti, bi in zip(t, rhs)]
    resid = [bi - xi - _dot_split(ai, xi) for ai, bi, xi in zip(a, rhs, x)]
    return [xi + _dot(ti, ri) for ti, xi, ri in zip(t, x, resid)]


def _gdn_kernel(qkv_ref, z_ref, sm_ref, conv0_ref, s0_ref, cw_ref, alog_ref, dt_ref, ng_ref,
                o_ref, s_ref, conv_ref, buf, *, rows, c, heads, conv_w, beta_col, a_col):
    ti = pl.program_id(1)
    nc = rows // c
    width = heads * HEAD_DIM
    hist = conv_w - 1
    lo = SUBLANES - hist

    @pl.when(ti == 0)
    def _():
        buf[lo:SUBLANES, :] = conv0_ref[0]
        s_ref[0] = s0_ref[0]

    y = _causal_conv(buf, qkv_ref[0], cw_ref, conv_w, rows)
    new_hist = buf[lo + rows:SUBLANES + rows, :]
    conv_ref[0] = new_hist
    buf[lo:SUBLANES, :] = new_hist
    y = jax.nn.silu(y)

    sm = sm_ref[0]
    beta = jax.nn.sigmoid(sm)
    g = -jnp.exp(alog_ref[...]) * jax.nn.softplus(sm + dt_ref[...])
    gcum = _cumsum_rows(g, c)
    g_end = jnp.concatenate(
        [jnp.broadcast_to(gcum[(i + 1) * c - 1:(i + 1) * c, :], (c, GATE_COLS)) for i in range(nc)],
        axis=0)
    pad = max(LANES - rows, 0)
    gcum_t = jnp.concatenate([gcum, jnp.zeros((pad, GATE_COLS), F32)], axis=0).T if pad else gcum.T

    row = lax.broadcasted_iota(jnp.int32, (rows, rows), 0)
    col = lax.broadcasted_iota(jnp.int32, (rows, rows), 1)
    lower = row >= col
    if nc > 1:
        lower = lower & (row // c == col // c)
    strict = lower & (row != col)
    z = z_ref[0]
    hs = range(heads)
    head = lambda base, h: y[:, base + h * HEAD_DIM:base + (h + 1) * HEAD_DIM]
    l2 = lambda t: t * lax.rsqrt(jnp.sum(t * t, axis=-1, keepdims=True) + EPS)
    q = [l2(head(0, h)) * HEAD_DIM ** -0.5 for h in hs]
    k = [l2(head(width, h)) for h in hs]
    v = [head(2 * width, h) for h in hs]
    bh = [beta[:, beta_col + h:beta_col + h + 1] for h in hs]
    gc = [gcum[:, a_col + h:a_col + h + 1] for h in hs]
    ge = [g_end[:, a_col + h:a_col + h + 1] for h in hs]
    gr = [gcum_t[a_col + h:a_col + h + 1, :rows] for h in hs]
    decay = [jnp.exp(jnp.where(lower, gc[h] - gr[h], NEG_INF)) for h in hs]
    kb = [k[h] * bh[h] for h in hs]
    eg = [jnp.exp(gc[h]) for h in hs]
    a = [_dot_nt(kb[h], k[h]) * jnp.where(strict, decay[h], 0.0) for h in hs]
    attn = [_dot_nt(q[h], k[h]) * decay[h] for h in hs]
    sol = _unit_lower_solve(a, [jnp.concatenate([v[h] * bh[h], kb[h] * eg[h]], axis=-1) for h in hs], c)
    qe = [q[h] * eg[h] for h in hs]
    kx = [k[h] * jnp.exp(ge[h] - gc[h]) for h in hs]
    s = [s_ref[0, h] for h in hs]
    deltas = [[] for _ in hs]
    inter = [[] for _ in hs]
    for i in range(nc):
        rs = slice(i * c, (i + 1) * c)
        d_i = [sol[h][rs, :HEAD_DIM] - _dot(sol[h][rs, HEAD_DIM:], s[h]) for h in hs]
        for h in hs:
            inter[h].append(_dot(qe[h][rs], s[h]))
            deltas[h].append(d_i[h])
        s = [jnp.exp(ge[h][(i + 1) * c - 1:(i + 1) * c, :]) * s[h] + _dot_tn(kx[h][rs], d_i[h])
             for h in hs]
    cat = lambda parts: parts[0] if nc == 1 else jnp.concatenate(parts, axis=0)
    for h in hs:
        s_ref[0, h] = s[h]
        o = cat(inter[h]) + _dot(attn[h], cat(deltas[h]))
        zh = z[:, h * HEAD_DIM:(h + 1) * HEAD_DIM]
        o_ref[0, :, h * HEAD_DIM:(h + 1) * HEAD_DIM] = (
            _rms(o, ng_ref[...]) * jax.nn.silu(zh)).astype(o_ref.dtype)


def _gdn(main3d, small3d, conv0, s0, conv_w, alog_row, dt_row, norm_g, *, rows, c, heads, beta_col,
         a_col):
    b, l, _ = main3d.shape
    width = heads * HEAD_DIM
    kw = conv_w.shape[0]
    fixed2 = lambda i, j: (0, 0)
    return pl.pallas_call(
        functools.partial(_gdn_kernel, rows=rows, c=c, heads=heads, conv_w=kw, beta_col=beta_col,
                          a_col=a_col),
        grid=(b, l // rows),
        in_specs=[pl.BlockSpec((1, rows, 3 * width), lambda i, j: (i, j, 0)),
                  pl.BlockSpec((1, rows, width), lambda i, j: (i, j, 3)),
                  pl.BlockSpec((1, rows, GATE_COLS), lambda i, j: (i, j, 0)),
                  pl.BlockSpec((1, kw - 1, 3 * width), lambda i, j: (i, 0, 0)),
                  pl.BlockSpec((1, heads, HEAD_DIM, HEAD_DIM), lambda i, j: (i, 0, 0, 0)),
                  pl.BlockSpec((kw, 3 * width), fixed2),
                  pl.BlockSpec((1, GATE_COLS), fixed2), pl.BlockSpec((1, GATE_COLS), fixed2),
                  pl.BlockSpec((1, HEAD_DIM), fixed2)],
        out_specs=[pl.BlockSpec((1, rows, width), lambda i, j: (i, j, 0)),
                   pl.BlockSpec((1, heads, HEAD_DIM, HEAD_DIM), lambda i, j: (i, 0, 0, 0)),
                   pl.BlockSpec((1, kw - 1, 3 * width), lambda i, j: (i, 0, 0))],
        out_shape=[jax.ShapeDtypeStruct((b, l, width), BF16),
                   jax.ShapeDtypeStruct((b, heads, HEAD_DIM, HEAD_DIM), F32),
                   jax.ShapeDtypeStruct((b, kw - 1, 3 * width), F32)],
        scratch_shapes=[pltpu.VMEM((SUBLANES + rows, 3 * width), F32)],
        compiler_params=_params(2), name="gdn")(
            main3d, main3d, small3d, conv0, s0, conv_w, alog_row, dt_row, norm_g)


def _fox_prompt_kernel(q_ref, k_ref, v_ref, ft_ref, g_ref, o_ref, vaug, m_scr, acc_scr, s_a, s_b,
                       *, tq):
    qi = pl.program_id(2)

    @pl.when(qi == 0)
    def _():
        vaug[:, :HEAD_DIM] = v_ref[0]
        vaug[:, HEAD_DIM:] = jnp.ones((vaug.shape[0], HEAD_DIM), vaug.dtype)

    m_scr[...] = jnp.full(m_scr.shape, NEG_INF, F32)
    acc_scr[...] = jnp.zeros(acc_scr.shape, F32)
    scale = HEAD_DIM ** -0.5
    q = q_ref[0]

    def scores(buf, j):
        start = pl.multiple_of(j * tq, tq)
        buf[...] = _dot_nt(q, k_ref[0, pl.ds(start, tq), :])

    def accumulate(buf, j, masked):
        start = pl.multiple_of(j * tq, tq)
        s = buf[...] * (scale * LOG2E) - ft_ref[0, 0, pl.ds(j, 1), :] * LOG2E
        if masked:
            row = lax.broadcasted_iota(jnp.int32, (tq, tq), 0)
            col = lax.broadcasted_iota(jnp.int32, (tq, tq), 1)
            s = jnp.where(col <= row, s, NEG_INF)
        m_prev = m_scr[...]
        m_new = jnp.maximum(m_prev, jnp.max(s, axis=-1, keepdims=True))
        alpha = jnp.exp2(m_prev - m_new)
        p = jnp.exp2(s - jnp.concatenate([m_new] * (tq // LANES), axis=1))
        pv = _dot(p, vaug[pl.ds(start, tq), :])
        acc_scr[...] = jnp.concatenate([alpha] * (2 * HEAD_DIM // LANES), axis=1) * acc_scr[...] + pv
        m_scr[...] = m_new

    scores(s_a, 0)

    def pair(jj, carry):
        j = 2 * jj
        scores(s_b, j + 1)
        accumulate(s_a, j, False)
        scores(s_a, j + 2)
        accumulate(s_b, j + 1, False)
        return carry

    lax.fori_loop(0, qi // 2, pair, 0)

    @pl.when(qi % 2 == 0)
    def _():
        accumulate(s_a, qi, True)

    @pl.when(qi % 2 == 1)
    def _():
        scores(s_b, qi)
        accumulate(s_a, qi - 1, False)
        accumulate(s_b, qi, True)

    o = acc_scr[:, :HEAD_DIM] / acc_scr[:, HEAD_DIM:]
    o_ref[0] = _rms(o, g_ref[0]).astype(o_ref.dtype)


def _fox_prompt(qkv3d, ft, norm_g, *, heads, tq):
    b, l, _ = qkv3d.shape
    ft4 = ft.reshape(b, heads, l // tq, tq)
    g3 = norm_g.reshape(heads, 1, HEAD_DIM)
    return pl.pallas_call(
        functools.partial(_fox_prompt_kernel, tq=tq),
        grid=(b, heads, l // tq),
        in_specs=[pl.BlockSpec((1, tq, HEAD_DIM), lambda i, h, j: (i, j, h)),
                  pl.BlockSpec((1, l, HEAD_DIM), lambda i, h, j: (i, 0, heads + h)),
                  pl.BlockSpec((1, l, HEAD_DIM), lambda i, h, j: (i, 0, 2 * heads + h)),
                  pl.BlockSpec((1, 1, l // tq, tq), lambda i, h, j: (i, h, 0, 0)),
                  pl.BlockSpec((1, 1, HEAD_DIM), lambda i, h, j: (h, 0, 0))],
        out_specs=pl.BlockSpec((1, tq, HEAD_DIM), lambda i, h, j: (i, j, h)),
        out_shape=jax.ShapeDtypeStruct((b, l, heads * HEAD_DIM), BF16),
        scratch_shapes=[pltpu.VMEM((l, 2 * HEAD_DIM), BF16), pltpu.VMEM((tq, LANES), F32),
                        pltpu.VMEM((tq, 2 * HEAD_DIM), F32), pltpu.VMEM((tq, tq), F32),
                        pltpu.VMEM((tq, tq), F32)],
        compiler_params=_params(3), name="fox_prompt")(qkv3d, qkv3d, qkv3d, ft4, g3)


def _fox_sample_kernel(q_ref, kn_ref, vn_ref, ck_ref, cv_ref, clt_ref, lnt_ref, g_ref, o_ref,
                       fc_scr, m_scr, l_scr, acc_scr, *, ls, heads, tp):
    j = pl.program_id(1)
    last = pl.num_programs(1) - 1
    scale = HEAD_DIM ** -0.5

    @pl.when(j == 0)
    def _():
        fc = _cumsum_lanes(clt_ref[0])
        for c in range(fc_scr.shape[0]):
            fc_scr[c] = fc[:, c * tp:(c + 1) * tp]
        m_scr[...] = jnp.full(m_scr.shape, NEG_INF, F32)
        l_scr[...] = jnp.zeros(l_scr.shape, F32)
        acc_scr[...] = jnp.zeros(acc_scr.shape, F32)

    def update(h, s, v):
        m_prev = m_scr[h]
        m_new = jnp.maximum(m_prev, jnp.max(s, axis=-1, keepdims=True))
        alpha = jnp.exp(m_prev - m_new)
        p = jnp.exp(s - m_new)
        l_scr[h] = alpha * l_scr[h] + jnp.sum(p, axis=-1, keepdims=True)
        acc_scr[h] = alpha * acc_scr[h] + _dot(p, v)
        m_scr[h] = m_new

    fcj = fc_scr[j]
    for h in range(heads):
        cs = slice(h * HEAD_DIM, (h + 1) * HEAD_DIM)
        k_h = ck_ref[0, pl.ds(h, tp, stride=heads), :]
        v_h = cv_ref[0, pl.ds(h, tp, stride=heads), :]
        update(h, _dot_nt(q_ref[0, :, cs], k_h) * scale - fcj[h:h + 1, :], v_h)

    @pl.when(j == last)
    def _():
        fn = fcj[:, tp - 1:tp] + _cumsum_lanes(lnt_ref[0])[:, :ls]
        row = lax.broadcasted_iota(jnp.int32, (ls, ls), 0)
        col = lax.broadcasted_iota(jnp.int32, (ls, ls), 1)
        for h in range(heads):
            cs = slice(h * HEAD_DIM, (h + 1) * HEAD_DIM)
            s_n = _dot_nt(q_ref[0, :, cs], kn_ref[0, :, cs]) * scale - fn[h:h + 1, :]
            update(h, jnp.where(col <= row, s_n, NEG_INF), vn_ref[0, :, cs])
            o = acc_scr[h] / l_scr[h]
            o_ref[0, :, cs] = _rms(o, g_ref[h:h + 1, :]).astype(o_ref.dtype)


def _fox_sample(qkv3d, cache_k, cache_v, cache_logf_t, logf_new_t, norm_g, *, heads, layer, tp):
    b, ls, _ = qkv3d.shape
    depth, _, p_len = cache_k.shape[:3]
    width = heads * HEAD_DIM
    ck = cache_k.reshape(depth * b, p_len * heads, HEAD_DIM)
    cv = cache_v.reshape(depth * b, p_len * heads, HEAD_DIM)
    cache_idx = lambda i, j: (layer * b + i, j, 0)
    return pl.pallas_call(
        functools.partial(_fox_sample_kernel, ls=ls, heads=heads, tp=tp),
        grid=(b, p_len // tp),
        in_specs=[pl.BlockSpec((1, ls, width), lambda i, j: (i, 0, 0)),
                  pl.BlockSpec((1, ls, width), lambda i, j: (i, 0, 1)),
                  pl.BlockSpec((1, ls, width), lambda i, j: (i, 0, 2)),
                  pl.BlockSpec((1, tp * heads, HEAD_DIM), cache_idx),
                  pl.BlockSpec((1, tp * heads, HEAD_DIM), cache_idx),
                  pl.BlockSpec((1, heads, p_len), lambda i, j: (i, 0, 0)),
                  pl.BlockSpec((1, heads, LANES), lambda i, j: (i, 0, 0)),
                  pl.BlockSpec((heads, HEAD_DIM), lambda i, j: (0, 0))],
        out_specs=pl.BlockSpec((1, ls, width), lambda i, j: (i, 0, 0)),
        out_shape=jax.ShapeDtypeStruct((b, ls, width), BF16),
        scratch_shapes=[pltpu.VMEM((p_len // tp, heads, tp), F32), pltpu.VMEM((heads, ls, 1), F32),
                        pltpu.VMEM((heads, ls, 1), F32), pltpu.VMEM((heads, ls, HEAD_DIM), F32)],
        compiler_params=_params(2), name="fox_sample")(
            qkv3d, qkv3d, qkv3d, ck, cv, cache_logf_t, logf_new_t, norm_g)


def _lru_kernel(x_ref, gate_ref, conv0_ref, h0_ref, cw_ref, cb_ref, wa_ref, wx_ref, ba_ref, bx_ref,
                lam_ref, ng_ref, o_ref, h_ref, conv_ref, buf, a_scr, hs_scr, *, tl, blocks, conv_w):
    ti = pl.program_id(1)
    hist = conv_w - 1
    lo = SUBLANES - hist
    bw = x_ref.shape[2] // blocks

    @pl.when(ti == 0)
    def _():
        buf[lo:SUBLANES, :] = conv0_ref[0]
        h_ref[0] = h0_ref[0]

    xc = _causal_conv(buf, x_ref[0], cw_ref, conv_w, tl) + cb_ref[...]
    new_hist = buf[lo + tl:SUBLANES + tl, :]
    conv_ref[0] = new_hist
    buf[lo:SUBLANES, :] = new_hist

    for n in range(blocks):
        sl = slice(n * bw, (n + 1) * bw)
        xb = xc[:, sl]
        r = jax.nn.sigmoid(_dot(xb, wa_ref[n]) + ba_ref[:, sl])
        i = jax.nn.sigmoid(_dot(xb, wx_ref[n]) + bx_ref[:, sl])
        log_a = -LRU_C * r * jax.nn.softplus(-lam_ref[:, sl])
        a_scr[:, sl] = jnp.exp(log_a)
        hs_scr[:, sl] = jnp.sqrt(-_expm1(2.0 * log_a)) * (i * xb)

    row = lax.broadcasted_iota(jnp.int32, (SUBLANES, a_scr.shape[1]), 0)

    def group(gi, h):
        start = pl.multiple_of(gi * SUBLANES, SUBLANES)
        a = a_scr[pl.ds(start, SUBLANES), :]
        u = hs_scr[pl.ds(start, SUBLANES), :]
        s = 1
        while s < SUBLANES:
            keep = row >= s
            u = jnp.where(keep, u + a * pltpu.roll(u, s, 0), u)
            a = jnp.where(keep, a * pltpu.roll(a, s, 0), a)
            s *= 2
        hg = u + a * h
        hs_scr[pl.ds(start, SUBLANES), :] = hg
        return hg[SUBLANES - 1:SUBLANES, :]

    h_ref[0] = lax.fori_loop(0, tl // SUBLANES, group, h_ref[0], unroll=4 if tl >= 32 else True)

    gate = gate_ref[0]
    for n in range(blocks):
        sl = slice(n * bw, (n + 1) * bw)
        o_ref[0, :, sl] = (_rms(hs_scr[:, sl], ng_ref[:, sl])
                           * jax.nn.gelu(gate[:, sl])).astype(o_ref.dtype)


def _lru(main3d, conv0, h0, cw, cb, wa, wx, ba, bx, lam, ng, *, tl, x_block, gate_block):
    b, l, _ = main3d.shape
    blocks, bw, _ = wa.shape
    width = blocks * bw
    kw = cw.shape[0]
    fixed2 = lambda i, j: (0, 0)
    fixed3 = lambda i, j: (0, 0, 0)
    vec = pl.BlockSpec((1, width), fixed2)
    return pl.pallas_call(
        functools.partial(_lru_kernel, tl=tl, blocks=blocks, conv_w=kw),
        grid=(b, l // tl),
        in_specs=[pl.BlockSpec((1, tl, width), lambda i, j: (i, j, x_block)),
                  pl.BlockSpec((1, tl, width), lambda i, j: (i, j, gate_block)),
                  pl.BlockSpec((1, kw - 1, width), lambda i, j: (i, 0, 0)),
                  pl.BlockSpec((1, 1, width), lambda i, j: (i, 0, 0)),
                  pl.BlockSpec((kw, width), fixed2), vec,
                  pl.BlockSpec((blocks, bw, bw), fixed3), pl.BlockSpec((blocks, bw, bw), fixed3),
                  vec, vec, vec, vec],
        out_specs=[pl.BlockSpec((1, tl, width), lambda i, j: (i, j, 0)),
                   pl.BlockSpec((1, 1, width), lambda i, j: (i, 0, 0)),
                   pl.BlockSpec((1, kw - 1, width), lambda i, j: (i, 0, 0))],
        out_shape=[jax.ShapeDtypeStruct((b, l, width), BF16),
                   jax.ShapeDtypeStruct((b, 1, width), F32),
                   jax.ShapeDtypeStruct((b, kw - 1, width), F32)],
        scratch_shapes=[pltpu.VMEM((SUBLANES + tl, width), F32), pltpu.VMEM((tl, width), F32),
                        pltpu.VMEM((tl, width), F32)],
        compiler_params=_params(2), name="rglru")(
            main3d, main3d, conv0, h0, cw, cb, wa, wx, ba, bx, lam, ng)


def _outproj_kernel(x_ref, og_ref, of_ref, ol_ref, w_ref, y_ref):
    wg, wf = og_ref.shape[1], of_ref.shape[1]
    y = x_ref[...] + jnp.dot(og_ref[...], w_ref[0:wg, :], preferred_element_type=F32)
    y = y + jnp.dot(of_ref[...], w_ref[wg:wg + wf, :], preferred_element_type=F32)
    y_ref[...] = y + jnp.dot(ol_ref[...], w_ref[wg + wf:, :], preferred_element_type=F32)


def _outproj(x2d, o_gdn, o_fox, o_lru, w, *, tm):
    m, d = x2d.shape
    row = lambda i: (i, 0)
    return pl.pallas_call(
        _outproj_kernel, grid=(m // tm,),
        in_specs=[pl.BlockSpec((tm, d), row), pl.BlockSpec((tm, o_gdn.shape[1]), row),
                  pl.BlockSpec((tm, o_fox.shape[1]), row), pl.BlockSpec((tm, o_lru.shape[1]), row),
                  _resident(w.shape, lambda i: (0, 0))],
        out_specs=pl.BlockSpec((tm, d), row),
        out_shape=jax.ShapeDtypeStruct((m, d), F32),
        compiler_params=_params(1), name="outproj")(x2d, o_gdn, o_fox, o_lru, w)


def _ffn_kernel(x_ref, g_ref, wg_ref, wv_ref, wd_ref, cw_ref, c0_ref, fg_ref, y_ref, cnew_ref,
                h_scr, gp_scr, *, nb, tiles_per_batch, tm, conv_w, final_norm):
    gi = pl.program_id(0)
    f = pl.program_id(1)
    hist = conv_w - 1
    lo = SUBLANES - hist

    rows = lambda j: slice(j * tm, (j + 1) * tm)

    @pl.when(f == 0)
    def _():
        for j in range(nb):
            x = x_ref[0, j]
            h_scr[rows(j), :] = _rms(x, g_ref[...]).astype(BF16)
            y_ref[0, j] = x

    if tiles_per_batch > 1:
        first = gi % tiles_per_batch == 0

        @pl.when(first)
        def _():
            gp_scr[0, lo:SUBLANES, :] = c0_ref[0]

        @pl.when(jnp.logical_not(first))
        def _():
            gp_scr[0, lo:SUBLANES, :] = cnew_ref[0, f]
    else:
        for j in range(nb):
            gp_scr[j, lo:SUBLANES, :] = c0_ref[j]

    tf = wg_ref.shape[1]
    half = tf // FFN_SPLIT
    h = h_scr[...]
    ups = []
    for c in range(FFN_SPLIT):
        cs = slice(c * half, (c + 1) * half)
        ups.append((jnp.dot(h, wg_ref[:, cs], preferred_element_type=F32),
                    jnp.dot(h, wv_ref[:, cs], preferred_element_type=F32)))
    contrib = None
    for c in range(FFN_SPLIT):
        cs = slice(c * half, (c + 1) * half)
        gp, val = ups[c]
        acts = [jax.nn.silu(_causal_conv(gp_scr.at[j], gp[rows(j)], cw_ref, conv_w, tm, cs))
                * val[rows(j)] for j in range(nb)]
        act = (acts[0] if nb == 1 else jnp.concatenate(acts, axis=0)).astype(BF16)
        part = jnp.dot(act, wd_ref[cs, :], preferred_element_type=F32)
        contrib = part if contrib is None else contrib + part
    for j in range(nb):
        cnew_ref[j, f] = gp_scr[j, lo + tm:SUBLANES + tm, :]
        y_ref[0, j] += contrib[rows(j)]

    if final_norm:
        @pl.when(f == pl.num_programs(1) - 1)
        def _():
            for j in range(nb):
                y_ref[0, j] = _rms(y_ref[0, j], fg_ref[...])


def _ffn(x3d, g, wg, wv, wd, cw, c0, fg, *, tm, tf, final_norm):
    b, l, d = x3d.shape
    dff = wg.shape[1]
    kw = cw.shape[0]
    tiles_per_batch = l // tm
    nb = b if tiles_per_batch == 1 else 1
    groups = b * l // (nb * tm)
    x4 = x3d.reshape(groups, nb, tm, d)
    nf = dff // tf
    if tiles_per_batch == 1:
        c_idx = lambda i, f: (0, 0, f)
        cnew_idx = lambda i, f: (0, 0, 0, 0)
    else:
        c_idx = lambda i, f: (i // tiles_per_batch, 0, f)
        cnew_idx = lambda i, f: (i // tiles_per_batch, 0, 0, 0)
    y, cnew = pl.pallas_call(
        functools.partial(_ffn_kernel, nb=nb, tiles_per_batch=tiles_per_batch, tm=tm, conv_w=kw,
                          final_norm=final_norm),
        grid=(groups, dff // tf),
        in_specs=[pl.BlockSpec((1, nb, tm, d), lambda i, f: (i, 0, 0, 0)),
                  pl.BlockSpec((1, d), lambda i, f: (0, 0)),
                  pl.BlockSpec((d, tf), lambda i, f: (0, f)),
                  pl.BlockSpec((d, tf), lambda i, f: (0, f)),
                  pl.BlockSpec((tf, d), lambda i, f: (f, 0)),
                  pl.BlockSpec((kw, tf), lambda i, f: (0, f)),
                  pl.BlockSpec((nb, kw - 1, tf), c_idx),
                  pl.BlockSpec((1, d), lambda i, f: (0, 0))],
        out_specs=[pl.BlockSpec((1, nb, tm, d), lambda i, f: (i, 0, 0, 0)),
                   pl.BlockSpec((nb, nf, kw - 1, tf), cnew_idx)],
        out_shape=[jax.ShapeDtypeStruct((groups, nb, tm, d), F32),
                   jax.ShapeDtypeStruct((b, nf, kw - 1, tf), F32)],
        scratch_shapes=[pltpu.VMEM((nb * tm, d), BF16), pltpu.VMEM((nb, SUBLANES + tm, tf), F32)],
        compiler_params=_params(2), name="ffn")(x4, g, wg, wv, wd, cw, c0, fg)
    return y.reshape(b, l, d), jnp.swapaxes(cnew, 1, 2).reshape(b, kw - 1, dff)


def _pad_cols(a, width):
    return jnp.pad(a, ((0, 0), (0, width - a.shape[1])))


def _layer_weights(l, w_in, w_out, ffn_w_up, ffn_w_down, sizes):
    gdn_w, gdn_h, fox_w, fox_h, lru_w = sizes
    w = w_in[l]
    offs = [0]
    for s in (gdn_w, gdn_w, gdn_w, gdn_w, gdn_h, gdn_h, fox_w, fox_w, fox_w, fox_h, lru_w, lru_w):
        offs.append(offs[-1] + s)
    gq, gk, gv, gz, gb, ga, fq, fk, fv, ff, lx, lg = [w[:, offs[i]:offs[i + 1]] for i in range(12)]
    small = _pad_cols(jnp.concatenate([ff, gb, ga], axis=1), GATE_COLS)
    packed = jnp.concatenate([gq, gk, gv, gz, lx, lg, fq, fk, fv, small], axis=1).astype(BF16)
    dff = ffn_w_down.shape[1]
    return dict(w_in=packed, w_out=w_out[l].astype(BF16), w_gate=ffn_w_up[l, :, :dff].astype(BF16),
                w_val=ffn_w_up[l, :, dff:].astype(BF16), w_down=ffn_w_down[l].astype(BF16))


def _run_group(x, cache, states, p, weights, dims, tiles):
    gdn_w, gdn_h, fox_w, fox_h, lru_w = dims
    b, l, d = x.shape
    depth = len(weights)
    n_main = 4 * gdn_w + 2 * lru_w
    gdn_s, gdn_conv, lru_h, lru_conv, ffn_conv = states
    beta_col, a_col = GATE_F + fox_h, GATE_F + fox_h + gdn_h
    outs = [[] for _ in range(6)]
    kv = None
    for li in range(depth):
        wl = weights[li]
        main, small, fk, fv, qkv = _inproj(
            x.reshape(b * l, d), p['norm_mix_g'][li][None], wl['w_in'], kv,
            n_main=n_main, n_fox=fox_w, tm=tiles['proj'], layer=li, depth=depth)
        kv = (fk, fv)
        main3 = main.reshape(b, l, n_main)
        small3 = small.reshape(b, l, GATE_COLS)
        qkv3 = qkv.reshape(b, l, 3 * fox_w)
        bias_row = _pad_cols(p['fox_f_bias'][li][None], GATE_COLS)
        if cache is None:
            logf, ft = _gate(small3, bias_row, heads=fox_h, cumulative=True)
            o_fox = _fox_prompt(qkv3, ft, p['fox_norm_g'][li], heads=fox_h, tq=tiles['attn'])
        else:
            logf, = _gate(small3, bias_row, heads=fox_h, cumulative=False)
            clt = jnp.swapaxes(cache[2][li], 1, 2)
            lnt = jnp.pad(jnp.swapaxes(logf, 1, 2), ((0, 0), (0, 0), (0, LANES - l)))
            o_fox = _fox_sample(qkv3, cache[0], cache[1], clt, lnt, p['fox_norm_g'][li],
                                heads=fox_h, layer=li, tp=min(1024, cache[0].shape[2]))
        row = lambda v, c0: jnp.pad(v[None], ((0, 0), (c0, GATE_COLS - c0 - v.shape[0])))
        o_gdn, s_new, gconv_new = _gdn(
            main3, small3, gdn_conv[li], gdn_s[li], p['gdn_conv_w'][li],
            row(p['gdn_a_log'][li], a_col), row(p['gdn_dt_bias'][li], a_col),
            p['gdn_norm_g'][li][None], rows=tiles['gdn'], c=min(l, CHUNK), heads=gdn_h,
            beta_col=beta_col, a_col=a_col)
        o_lru, h_new, lconv_new = _lru(
            main3, lru_conv[li], lru_h[li][:, None, :], p['lru_conv_w'][li], p['lru_conv_b'][li][None],
            p['lru_w_a'][li].astype(BF16), p['lru_w_x'][li].astype(BF16), p['lru_b_a'][li][None],
            p['lru_b_x'][li][None], p['lru_lambda'][li][None], p['lru_norm_g'][li][None],
            tl=tiles['lru'], x_block=4 * gdn_w // lru_w, gate_block=4 * gdn_w // lru_w + 1)
        x1 = _outproj(x.reshape(b * l, d), o_gdn.reshape(b * l, gdn_w), o_fox.reshape(b * l, fox_w),
                      o_lru.reshape(b * l, lru_w), wl['w_out'], tm=tiles['proj'])
        x, fconv_new = _ffn(
            x1.reshape(b, l, d), p['norm_ffn_g'][li][None], wl['w_gate'], wl['w_val'], wl['w_down'],
            p['ffn_conv_w'][li], ffn_conv[li], p['final_norm_g'][None],
            tm=tiles['ffn'], tf=512, final_norm=li == depth - 1)
        new = [logf, s_new, gconv_new, h_new[:, 0, :], lconv_new, fconv_new]
        for lst, arr in zip(outs, new):
            lst.append(arr)
    kv_shape = (depth, b, l, fox_h, HEAD_DIM)
    return x, [kv[0].reshape(kv_shape), kv[1].reshape(kv_shape)] + [jnp.stack(lst) for lst in outs]


def kernel(x_prompt, x_sample, cache_fox_k, cache_fox_v, cache_fox_logf, state_gdn, state_gdn_conv,
           state_lru, state_lru_conv, state_ffn_conv, norm_mix_g, w_in, gdn_conv_w, gdn_a_log,
           gdn_dt_bias, gdn_norm_g, fox_f_bias, fox_norm_g, lru_conv_w, lru_conv_b, lru_w_a, lru_b_a,
           lru_w_x, lru_b_x, lru_lambda, lru_norm_g, w_out, norm_ffn_g, ffn_w_up, ffn_conv_w,
           ffn_w_down, final_norm_g):
    p = dict(norm_mix_g=norm_mix_g, gdn_conv_w=gdn_conv_w, gdn_a_log=gdn_a_log,
             gdn_dt_bias=gdn_dt_bias, gdn_norm_g=gdn_norm_g, fox_f_bias=fox_f_bias,
             fox_norm_g=fox_norm_g, lru_conv_w=lru_conv_w, lru_conv_b=lru_conv_b, lru_w_a=lru_w_a,
             lru_b_a=lru_b_a, lru_w_x=lru_w_x, lru_b_x=lru_b_x, lru_lambda=lru_lambda,
             lru_norm_g=lru_norm_g, norm_ffn_g=norm_ffn_g, ffn_conv_w=ffn_conv_w,
             final_norm_g=final_norm_g)
    depth = w_in.shape[0]
    bp, lp, _ = x_prompt.shape
    gdn_h = state_gdn.shape[2]
    fox_h = cache_fox_k.shape[3]
    lru_w = state_lru.shape[2]
    dims = (gdn_h * HEAD_DIM, gdn_h, fox_h * HEAD_DIM, fox_h, lru_w)
    dff = ffn_w_down.shape[1]
    weights = [_layer_weights(l, w_in, w_out, ffn_w_up, ffn_w_down, dims) for l in range(depth)]

    zeros = lambda *s: jnp.zeros((depth, bp) + s, F32)
    prompt_states = (zeros(gdn_h, HEAD_DIM, HEAD_DIM), zeros(gdn_conv_w.shape[1] - 1, 3 * dims[0]),
                     zeros(lru_w), zeros(lru_conv_w.shape[1] - 1, lru_w),
                     zeros(ffn_conv_w.shape[1] - 1, dff))
    prompt_tiles = dict(proj=min(256, bp * lp), attn=min(512, lp), lru=min(256, lp), ffn=min(512, lp),
                        gdn=min(256, lp))
    y_prompt, sp = _run_group(x_prompt, None, prompt_states, p, weights, dims, prompt_tiles)

    bs, ls, _ = x_sample.shape
    sample_tiles = dict(proj=bs * ls, lru=ls, ffn=ls, gdn=ls)
    y_sample, ss = _run_group(
        x_sample, (cache_fox_k, cache_fox_v, cache_fox_logf),
        (state_gdn, state_gdn_conv, state_lru, state_lru_conv, state_ffn_conv), p, weights, dims,
        sample_tiles)
    return (y_prompt, y_sample, *sp, *ss)
```

```python
import functools
import math

import jax
import jax.numpy as jnp
from jax import lax
from jax.experimental import pallas as pl
from jax.experimental.pallas import tpu as pltpu

F32 = jnp.float32
BF16 = jnp.bfloat16

HEAD_DIM = 128
CHUNK = 64
LRU_C = 8.0
EPS = 1e-6
LANES = 128
SUBLANES = 8
VMEM_LIMIT = 56 * 1024 * 1024

GATE_COLS = LANES
GATE_F = 0
FFN_SPLIT = 2
NEG_INF = float("-inf")
LOG2E = math.log2(math.e)


def _params(n_axes):
    return pltpu.CompilerParams(dimension_semantics=("arbitrary",) * n_axes,
                                vmem_limit_bytes=VMEM_LIMIT)


def _resident(block_shape, index_map):
    return pl.BlockSpec(block_shape, index_map, pipeline_mode=pl.Buffered(1))


def _dot(a, b):
    return jnp.dot(a.astype(BF16), b.astype(BF16), preferred_element_type=F32)


def _dot_nt(a, b):
    return lax.dot_general(a.astype(BF16), b.astype(BF16), (((1,), (1,)), ((), ())),
                           preferred_element_type=F32)


def _dot_tn(a, b):
    return lax.dot_general(a.astype(BF16), b.astype(BF16), (((0,), (0,)), ((), ())),
                           preferred_element_type=F32)


def _dot_f32(a, b):
    return jnp.dot(a, b, preferred_element_type=F32, precision=lax.Precision.HIGHEST)


def _rms(x, g):
    return x * lax.rsqrt(jnp.mean(x * x, axis=-1, keepdims=True) + EPS) * g


def _expm1(x):
    u = jnp.exp(x)
    um1 = u - 1.0
    near = jnp.where(u == 1.0, x, um1 * x / jnp.log(u))
    return jnp.where(jnp.abs(x) > 0.5, um1, near)


def _cumsum_lanes(x, seg=None):
    n = x.shape[-1]
    lane = lax.broadcasted_iota(jnp.int32, x.shape, x.ndim - 1)
    pos = lane if seg is None else lane % seg
    s = 1
    while s < (n if seg is None else seg):
        x = x + jnp.where(pos >= s, pltpu.roll(x, s, x.ndim - 1), 0.0)
        s *= 2
    return x


def _cumsum_rows(x, seg):
    pos = lax.broadcasted_iota(jnp.int32, x.shape, 0) % seg
    s = 1
    while s < seg:
        x = x + jnp.where(pos >= s, pltpu.roll(x, s, 0), 0.0)
        s *= 2
    return x


def _causal_conv(buf, x, w_ref, width, rows, cols=slice(None)):
    lo = SUBLANES - (width - 1)
    buf[SUBLANES:SUBLANES + rows, cols] = x
    y = w_ref[width - 1:width, cols] * x
    for j in range(width - 1):
        y = y + w_ref[j:j + 1, cols] * buf[lo + j:lo + j + rows, cols]
    return y


def _inproj_kernel(*refs, n_main, n_fox, seg, n_in):
    x_ref, g_ref, w_ref = refs[:3]
    main_ref, small_ref, fk_ref, fv_ref, qkv_ref, h_scr = refs[n_in:]
    h_scr[...] = _rms(x_ref[...], g_ref[...]).astype(BF16)

    def cols(c0, width):
        return jnp.dot(h_scr[...], w_ref[:, c0:c0 + width], preferred_element_type=F32)

    for c0 in range(0, n_main, seg):
        main_ref[:, c0:c0 + seg] = cols(c0, seg)
    for j in range(3):
        for c in range(0, n_fox, seg):
            p = cols(n_main + j * n_fox + c, seg)
            qkv_ref[:, j * n_fox + c:j * n_fox + c + seg] = p.astype(BF16)
            if j == 1:
                fk_ref[:, c:c + seg] = p
            if j == 2:
                fv_ref[:, c:c + seg] = p
    small_ref[...] = cols(n_main + 3 * n_fox, GATE_COLS)


def _inproj(x2d, g, w, kv_prev, *, n_main, n_fox, tm, layer, depth):
    m, d = x2d.shape
    n_all = w.shape[1]
    row = lambda i: (i, 0)
    kv_row = lambda i: (layer * (m // tm) + i, 0)
    fixed = lambda i: (0, 0)
    in_specs = [pl.BlockSpec((tm, d), row), _resident((1, d), fixed), _resident((d, n_all), fixed)]
    operands = [x2d, g, w]
    aliases = {}
    if kv_prev is not None:
        in_specs += [pl.BlockSpec(memory_space=pl.ANY)] * 2
        operands += list(kv_prev)
        aliases = {3: 2, 4: 3}
    return pl.pallas_call(
        functools.partial(_inproj_kernel, n_main=n_main, n_fox=n_fox, seg=512,
                          n_in=len(operands)),
        grid=(m // tm,),
        in_specs=in_specs,
        out_specs=[pl.BlockSpec((tm, n_main), row), pl.BlockSpec((tm, GATE_COLS), row),
                   pl.BlockSpec((tm, n_fox), kv_row), pl.BlockSpec((tm, n_fox), kv_row),
                   pl.BlockSpec((tm, 3 * n_fox), row)],
        out_shape=[jax.ShapeDtypeStruct((m, n_main), F32), jax.ShapeDtypeStruct((m, GATE_COLS), F32),
                   jax.ShapeDtypeStruct((depth * m, n_fox), F32),
                   jax.ShapeDtypeStruct((depth * m, n_fox), F32),
                   jax.ShapeDtypeStruct((m, 3 * n_fox), BF16)],
        scratch_shapes=[pltpu.VMEM((tm, d), BF16)],
        input_output_aliases=aliases,
        compiler_params=_params(1), name="inproj")(*operands)


def _gate_kernel(sm_ref, b_ref, logf_ref, *rest, heads, cumulative):
    ls = jax.nn.log_sigmoid(sm_ref[0] + b_ref[...])
    logf_ref[0] = ls[:, GATE_F:GATE_F + heads]
    if cumulative:
        ft_ref, = rest
        ft_ref[0] = _cumsum_lanes(ls.T[GATE_F:GATE_F + heads, :])


def _gate(small3d, bias_row, *, heads, cumulative):
    b, rows, _ = small3d.shape
    out_specs = [pl.BlockSpec((1, rows, heads), lambda i: (i, 0, 0))]
    out_shape = [jax.ShapeDtypeStruct((b, rows, heads), F32)]
    if cumulative:
        out_specs.append(pl.BlockSpec((1, heads, rows), lambda i: (i, 0, 0)))
        out_shape.append(jax.ShapeDtypeStruct((b, heads, rows), F32))
    return pl.pallas_call(
        functools.partial(_gate_kernel, heads=heads, cumulative=cumulative),
        grid=(b,),
        in_specs=[pl.BlockSpec((1, rows, GATE_COLS), lambda i: (i, 0, 0)),
                  pl.BlockSpec((1, GATE_COLS), lambda i: (0, 0))],
        out_specs=out_specs, out_shape=out_shape,
        compiler_params=_params(1), name="fox_gate")(small3d, bias_row)


def _split_bf16(x):
    hi = x.astype(BF16)
    return hi, (x - hi.astype(F32)).astype(BF16)


def _dot_split(a, b):
    a_hi, a_lo = _split_bf16(a)
    b_hi, b_lo = _split_bf16(b)
    mm = lambda x, y: jnp.dot(x, y, preferred_element_type=F32)
    return mm(a_hi, b_hi) + (mm(a_lo, b_hi) + mm(a_hi, b_lo))


def _unit_lower_solve(a, rhs, c):
    r = a[0].shape[0]
    row = lax.broadcasted_iota(jnp.int32, (r, r), 0)
    col = lax.broadcasted_iota(jnp.int32, (r, r), 1)
    eye = jnp.where(row == col, 1.0, 0.0)
    p = [-x for x in a]
    t = [eye + x for x in p]
    for _ in range(int(math.log2(c)) - 1):
        p = [_dot(x, x) for x in p]
        t = [ti + _dot(ti, pi) for ti, pi in zip(t, p)]
    x = [_dot(ti, bi) for ti, bi in zip(t, rhs)]
    resid = [bi - xi - _dot_split(ai, xi) for ai, bi, xi in zip(a, rhs, x)]
    return [xi + _dot(ti, ri) for ti, xi, ri in zip(t, x, resid)]


def _gdn_kernel(qkv_ref, z_ref, sm_ref, conv0_ref, s0_ref, cw_ref, alog_ref, dt_ref, ng_ref,
                o_ref, s_ref, conv_ref, buf, *, rows, c, heads, conv_w, beta_col, a_col):
    ti = pl.program_id(1)
    nc = rows // c
    width = heads * HEAD_DIM
    hist = conv_w - 1
    lo = SUBLANES - hist

    @pl.when(ti == 0)
    def _():
        buf[lo:SUBLANES, :] = conv0_ref[0]
        s_ref[0] = s0_ref[0]

    y = _causal_conv(buf, qkv_ref[0], cw_ref, conv_w, rows)
    new_hist = buf[lo + rows:SUBLANES + rows, :]
    conv_ref[0] = new_hist
    buf[lo:SUBLANES, :] = new_hist
    y = jax.nn.silu(y)

    sm = sm_ref[0]
    beta = jax.nn.sigmoid(sm)
    g = -jnp.exp(alog_ref[...]) * jax.nn.softplus(sm + dt_ref[...])
    gcum = _cumsum_rows(g, c)
    g_end = jnp.concatenate(
        [jnp.broadcast_to(gcum[(i + 1) * c - 1:(i + 1) * c, :], (c, GATE_COLS)) for i in range(nc)],
        axis=0)
    pad = max(LANES - rows, 0)
    gcum_t = jnp.concatenate([gcum, jnp.zeros((pad, GATE_COLS), F32)], axis=0).T if pad else gcum.T

    row = lax.broadcasted_iota(jnp.int32, (rows, rows), 0)
    col = lax.broadcasted_iota(jnp.int32, (rows, rows), 1)
    lower = row >= col
    if nc > 1:
        lower = lower & (row // c == col // c)
    strict = lower & (row != col)
    z = z_ref[0]
    hs = range(heads)
    head = lambda base, h: y[:, base + h * HEAD_DIM:base + (h + 1) * HEAD_DIM]
    l2 = lambda t: t * lax.rsqrt(jnp.sum(t * t, axis=-1, keepdims=True) + EPS)
    q = [l2(head(0, h)) * HEAD_DIM ** -0.5 for h in hs]
    k = [l2(head(width, h)) for h in hs]
    v = [head(2 * width, h) for h in hs]
    bh = [beta[:, beta_col + h:beta_col + h + 1] for h in hs]
    gc = [gcum[:, a_col + h:a_col + h + 1] for h in hs]
    ge = [g_end[:, a_col + h:a_col + h + 1] for h in hs]
    gr = [gcum_t[a_col + h:a_col + h + 1, :rows] for h in hs]
    decay = [jnp.exp(jnp.where(lower, gc[h] - gr[h], NEG_INF)) for h in hs]
    kb = [k[h] * bh[h] for h in hs]
    eg = [jnp.exp(gc[h]) for h in hs]
    a = [_dot_nt(kb[h], k[h]) * jnp.where(strict, decay[h], 0.0) for h in hs]
    attn = [_dot_nt(q[h], k[h]) * decay[h] for h in hs]
    sol = _unit_lower_solve(a, [jnp.concatenate([v[h] * bh[h], kb[h] * eg[h]], axis=-1) for h in hs], c)
    qe = [q[h] * eg[h] for h in hs]
    kx = [k[h] * jnp.exp(ge[h] - gc[h]) for h in hs]
    s = [s_ref[0, h] for h in hs]
    deltas = [[] for _ in hs]
    inter = [[] for _ in hs]
    for i in range(nc):
        rs = slice(i * c, (i + 1) * c)
        d_i = [sol[h][rs, :HEAD_DIM] - _dot(sol[h][rs, HEAD_DIM:], s[h]) for h in hs]
        for h in hs:
            inter[h].append(_dot(qe[h][rs], s[h]))
            deltas[h].append(d_i[h])
        s = [jnp.exp(ge[h][(i + 1) * c - 1:(i + 1) * c, :]) * s[h] + _dot_tn(kx[h][rs], d_i[h])
             for h in hs]
    cat = lambda parts: parts[0] if nc == 1 else jnp.concatenate(parts, axis=0)
    for h in hs:
        s_ref[0, h] = s[h]
        o = cat(inter[h]) + _dot(attn[h], cat(deltas[h]))
        zh = z[:, h * HEAD_DIM:(h + 1) * HEAD_DIM]
        o_ref[0, :, h * HEAD_DIM:(h + 1) * HEAD_DIM] = (
            _rms(o, ng_ref[...]) * jax.nn.silu(zh)).astype(o_ref.dtype)


def _gdn(main3d, small3d, conv0, s0, conv_w, alog_row, dt_row, norm_g, *, rows, c, heads, beta_col,
         a_col):
    b, l, _ = main3d.shape
    width = heads * HEAD_DIM
    kw = conv_w.shape[0]
    fixed2 = lambda i, j: (0, 0)
    return pl.pallas_call(
        functools.partial(_gdn_kernel, rows=rows, c=c, heads=heads, conv_w=kw, beta_col=beta_col,
                          a_col=a_col),
        grid=(b, l // rows),
        in_specs=[pl.BlockSpec((1, rows, 3 * width), lambda i, j: (i, j, 0)),
                  pl.BlockSpec((1, rows, width), lambda i, j: (i, j, 3)),
                  pl.BlockSpec((1, rows, GATE_COLS), lambda i, j: (i, j, 0)),
                  pl.BlockSpec((1, kw - 1, 3 * width), lambda i, j: (i, 0, 0)),
                  pl.BlockSpec((1, heads, HEAD_DIM, HEAD_DIM), lambda i, j: (i, 0, 0, 0)),
                  pl.BlockSpec((kw, 3 * width), fixed2),
                  pl.BlockSpec((1, GATE_COLS), fixed2), pl.BlockSpec((1, GATE_COLS), fixed2),
                  pl.BlockSpec((1, HEAD_DIM), fixed2)],
        out_specs=[pl.BlockSpec((1, rows, width), lambda i, j: (i, j, 0)),
                   pl.BlockSpec((1, heads, HEAD_DIM, HEAD_DIM), lambda i, j: (i, 0, 0, 0)),
                   pl.BlockSpec((1, kw - 1, 3 * width), lambda i, j: (i, 0, 0))],
        out_shape=[jax.ShapeDtypeStruct((b, l, width), BF16),
                   jax.ShapeDtypeStruct((b, heads, HEAD_DIM, HEAD_DIM), F32),
                   jax.ShapeDtypeStruct((b, kw - 1, 3 * width), F32)],
        scratch_shapes=[pltpu.VMEM((SUBLANES + rows, 3 * width), F32)],
        compiler_params=_params(2), name="gdn")(
            main3d, main3d, small3d, conv0, s0, conv_w, alog_row, dt_row, norm_g)


def _fox_prompt_kernel(q_ref, k_ref, v_ref, ft_ref, g_ref, o_ref, vaug, m_scr, acc_scr, s_a, s_b,
                       *, tq):
    nq = q_ref.shape[1] // tq
    vaug[:, :HEAD_DIM] = v_ref[0]
    vaug[:, HEAD_DIM:] = jnp.ones((vaug.shape[0], HEAD_DIM), vaug.dtype)
    scale = HEAD_DIM ** -0.5

    def block_start(j):
        return j * tq if isinstance(j, int) else pl.multiple_of(j * tq, tq)

    def scores(buf, qi, j):
        buf[...] = _dot_nt(q_ref[0, qi * tq:(qi + 1) * tq, :], k_ref[0, pl.ds(block_start(j), tq), :])

    def accumulate(buf, j, masked):
        start = block_start(j)
        s = buf[...] * (scale * LOG2E) - ft_ref[0, 0, pl.ds(j, 1), :] * LOG2E
        if masked:
            row = lax.broadcasted_iota(jnp.int32, (tq, tq), 0)
            col = lax.broadcasted_iota(jnp.int32, (tq, tq), 1)
            s = jnp.where(col <= row, s, NEG_INF)
        m_prev = m_scr[...]
        m_new = jnp.maximum(m_prev, jnp.max(s, axis=-1, keepdims=True))
        alpha = jnp.exp2(m_prev - m_new)
        p = jnp.exp2(s - jnp.concatenate([m_new] * (tq // LANES), axis=1))
        pv = _dot(p, vaug[pl.ds(start, tq), :])
        acc_scr[...] = jnp.concatenate([alpha] * (2 * HEAD_DIM // LANES), axis=1) * acc_scr[...] + pv
        m_scr[...] = m_new

    cur, other = s_a, s_b
    scores(cur, 0, 0)
    for qi in range(nq):
        m_scr[...] = jnp.full(m_scr.shape, NEG_INF, F32)
        acc_scr[...] = jnp.zeros(acc_scr.shape, F32)

        def pair(jj, carry, qi=qi, cur=cur, other=other):
            j = 2 * jj
            scores(other, qi, j + 1)
            accumulate(cur, j, False)
            scores(cur, qi, j + 2)
            accumulate(other, j + 1, False)
            return carry

        if qi >= 2:
            lax.fori_loop(0, qi // 2, pair, 0)
        if qi % 2 == 0:
            if qi + 1 < nq:
                scores(other, qi + 1, 0)
            accumulate(cur, qi, True)
            cur, other = other, cur
        else:
            scores(other, qi, qi)
            accumulate(cur, qi - 1, False)
            if qi + 1 < nq:
                scores(cur, qi + 1, 0)
            accumulate(other, qi, True)
        o = acc_scr[:, :HEAD_DIM] / acc_scr[:, HEAD_DIM:]
        o_ref[0, qi * tq:(qi + 1) * tq, :] = _rms(o, g_ref[0]).astype(o_ref.dtype)


def _fox_prompt(qkv3d, ft, norm_g, *, heads, tq):
    b, l, _ = qkv3d.shape
    ft4 = ft.reshape(b, heads, l // tq, tq)
    g3 = norm_g.reshape(heads, 1, HEAD_DIM)
    head_block = lambda part: pl.BlockSpec((1, l, HEAD_DIM), lambda i, h: (i, 0, part * heads + h))
    return pl.pallas_call(
        functools.partial(_fox_prompt_kernel, tq=tq),
        grid=(b, heads),
        in_specs=[head_block(0), head_block(1), head_block(2),
                  pl.BlockSpec((1, 1, l // tq, tq), lambda i, h: (i, h, 0, 0)),
                  pl.BlockSpec((1, 1, HEAD_DIM), lambda i, h: (h, 0, 0))],
        out_specs=pl.BlockSpec((1, l, HEAD_DIM), lambda i, h: (i, 0, h)),
        out_shape=jax.ShapeDtypeStruct((b, l, heads * HEAD_DIM), BF16),
        scratch_shapes=[pltpu.VMEM((l, 2 * HEAD_DIM), BF16), pltpu.VMEM((tq, LANES), F32),
                        pltpu.VMEM((tq, 2 * HEAD_DIM), F32), pltpu.VMEM((tq, tq), F32),
                        pltpu.VMEM((tq, tq), F32)],
        compiler_params=_params(2), name="fox_prompt")(qkv3d, qkv3d, qkv3d, ft4, g3)


def _fox_sample_kernel(q_ref, kn_ref, vn_ref, ck_ref, cv_ref, clt_ref, lnt_ref, g_ref, o_ref,
                       fc_scr, m_scr, l_scr, acc_scr, *, ls, heads, tp):
    j = pl.program_id(1)
    last = pl.num_programs(1) - 1
    scale = HEAD_DIM ** -0.5

    @pl.when(j == 0)
    def _():
        fc = _cumsum_lanes(clt_ref[0])
        for c in range(fc_scr.shape[0]):
            fc_scr[c] = fc[:, c * tp:(c + 1) * tp]
        m_scr[...] = jnp.full(m_scr.shape, NEG_INF, F32)
        l_scr[...] = jnp.zeros(l_scr.shape, F32)
        acc_scr[...] = jnp.zeros(acc_scr.shape, F32)

    def update(h, s, v):
        m_prev = m_scr[h]
        m_new = jnp.maximum(m_prev, jnp.max(s, axis=-1, keepdims=True))
        alpha = jnp.exp(m_prev - m_new)
        p = jnp.exp(s - m_new)
        l_scr[h] = alpha * l_scr[h] + jnp.sum(p, axis=-1, keepdims=True)
        acc_scr[h] = alpha * acc_scr[h] + _dot(p, v)
        m_scr[h] = m_new

    fcj = fc_scr[j]
    for h in range(heads):
        cs = slice(h * HEAD_DIM, (h + 1) * HEAD_DIM)
        k_h = ck_ref[0, pl.ds(h, tp, stride=heads), :]
        v_h = cv_ref[0, pl.ds(h, tp, stride=heads), :]
        update(h, _dot_nt(q_ref[0, :, cs], k_h) * scale - fcj[h:h + 1, :], v_h)

    @pl.when(j == last)
    def _():
        fn = fcj[:, tp - 1:tp] + _cumsum_lanes(lnt_ref[0])[:, :ls]
        row = lax.broadcasted_iota(jnp.int32, (ls, ls), 0)
        col = lax.broadcasted_iota(jnp.int32, (ls, ls), 1)
        for h in range(heads):
            cs = slice(h * HEAD_DIM, (h + 1) * HEAD_DIM)
            s_n = _dot_nt(q_ref[0, :, cs], kn_ref[0, :, cs]) * scale - fn[h:h + 1, :]
            update(h, jnp.where(col <= row, s_n, NEG_INF), vn_ref[0, :, cs])
            o = acc_scr[h] / l_scr[h]
            o_ref[0, :, cs] = _rms(o, g_ref[h:h + 1, :]).astype(o_ref.dtype)


def _fox_sample(qkv3d, cache_k, cache_v, cache_logf_t, logf_new_t, norm_g, *, heads, layer, tp):
    b, ls, _ = qkv3d.shape
    depth, _, p_len = cache_k.shape[:3]
    width = heads * HEAD_DIM
    ck = cache_k.reshape(depth * b, p_len * heads, HEAD_DIM)
    cv = cache_v.reshape(depth * b, p_len * heads, HEAD_DIM)
    cache_idx = lambda i, j: (layer * b + i, j, 0)
    return pl.pallas_call(
        functools.partial(_fox_sample_kernel, ls=ls, heads=heads, tp=tp),
        grid=(b, p_len // tp),
        in_specs=[pl.BlockSpec((1, ls, width), lambda i, j: (i, 0, 0)),
                  pl.BlockSpec((1, ls, width), lambda i, j: (i, 0, 1)),
                  pl.BlockSpec((1, ls, width), lambda i, j: (i, 0, 2)),
                  pl.BlockSpec((1, tp * heads, HEAD_DIM), cache_idx),
                  pl.BlockSpec((1, tp * heads, HEAD_DIM), cache_idx),
                  pl.BlockSpec((1, heads, p_len), lambda i, j: (i, 0, 0)),
                  pl.BlockSpec((1, heads, LANES), lambda i, j: (i, 0, 0)),
                  pl.BlockSpec((heads, HEAD_DIM), lambda i, j: (0, 0))],
        out_specs=pl.BlockSpec((1, ls, width), lambda i, j: (i, 0, 0)),
        out_shape=jax.ShapeDtypeStruct((b, ls, width), BF16),
        scratch_shapes=[pltpu.VMEM((p_len // tp, heads, tp), F32), pltpu.VMEM((heads, ls, 1), F32),
                        pltpu.VMEM((heads, ls, 1), F32), pltpu.VMEM((heads, ls, HEAD_DIM), F32)],
        compiler_params=_params(2), name="fox_sample")(
            qkv3d, qkv3d, qkv3d, ck, cv, cache_logf_t, logf_new_t, norm_g)


def _lru_kernel(x_ref, gate_ref, conv0_ref, h0_ref, cw_ref, cb_ref, wa_ref, wx_ref, ba_ref, bx_ref,
                lam_ref, ng_ref, o_ref, h_ref, conv_ref, buf, a_scr, hs_scr, *, tl, blocks, conv_w):
    ti = pl.program_id(1)
    hist = conv_w - 1
    lo = SUBLANES - hist
    bw = x_ref.shape[2] // blocks

    @pl.when(ti == 0)
    def _():
        buf[lo:SUBLANES, :] = conv0_ref[0]
        h_ref[0] = h0_ref[0]

    xc = _causal_conv(buf, x_ref[0], cw_ref, conv_w, tl) + cb_ref[...]
    new_hist = buf[lo + tl:SUBLANES + tl, :]
    conv_ref[0] = new_hist
    buf[lo:SUBLANES, :] = new_hist

    for n in range(blocks):
        sl = slice(n * bw, (n + 1) * bw)
        xb = xc[:, sl]
        r = jax.nn.sigmoid(_dot(xb, wa_ref[n]) + ba_ref[:, sl])
        i = jax.nn.sigmoid(_dot(xb, wx_ref[n]) + bx_ref[:, sl])
        log_a = -LRU_C * r * jax.nn.softplus(-lam_ref[:, sl])
        a_scr[:, sl] = jnp.exp(log_a)
        hs_scr[:, sl] = jnp.sqrt(-_expm1(2.0 * log_a)) * (i * xb)

    row = lax.broadcasted_iota(jnp.int32, (SUBLANES, a_scr.shape[1]), 0)

    def group(gi, h):
        start = pl.multiple_of(gi * SUBLANES, SUBLANES)
        a = a_scr[pl.ds(start, SUBLANES), :]
        u = hs_scr[pl.ds(start, SUBLANES), :]
        s = 1
        while s < SUBLANES:
            keep = row >= s
            u = jnp.where(keep, u + a * pltpu.roll(u, s, 0), u)
            a = jnp.where(keep, a * pltpu.roll(a, s, 0), a)
            s *= 2
        hg = u + a * h
        hs_scr[pl.ds(start, SUBLANES), :] = hg
        return hg[SUBLANES - 1:SUBLANES, :]

    h_ref[0] = lax.fori_loop(0, tl // SUBLANES, group, h_ref[0], unroll=4 if tl >= 32 else True)

    gate = gate_ref[0]
    for n in range(blocks):
        sl = slice(n * bw, (n + 1) * bw)
        o_ref[0, :, sl] = (_rms(hs_scr[:, sl], ng_ref[:, sl])
                           * jax.nn.gelu(gate[:, sl])).astype(o_ref.dtype)


def _lru(main3d, conv0, h0, cw, cb, wa, wx, ba, bx, lam, ng, *, tl, x_block, gate_block):
    b, l, _ = main3d.shape
    blocks, bw, _ = wa.shape
    width = blocks * bw
    kw = cw.shape[0]
    fixed2 = lambda i, j: (0, 0)
    fixed3 = lambda i, j: (0, 0, 0)
    vec = pl.BlockSpec((1, width), fixed2)
    return pl.pallas_call(
        functools.partial(_lru_kernel, tl=tl, blocks=blocks, conv_w=kw),
        grid=(b, l // tl),
        in_specs=[pl.BlockSpec((1, tl, width), lambda i, j: (i, j, x_block)),
                  pl.BlockSpec((1, tl, width), lambda i, j: (i, j, gate_block)),
                  pl.BlockSpec((1, kw - 1, width), lambda i, j: (i, 0, 0)),
                  pl.BlockSpec((1, 1, width), lambda i, j: (i, 0, 0)),
                  pl.BlockSpec((kw, width), fixed2), vec,
                  pl.BlockSpec((blocks, bw, bw), fixed3), pl.BlockSpec((blocks, bw, bw), fixed3),
                  vec, vec, vec, vec],
        out_specs=[pl.BlockSpec((1, tl, width), lambda i, j: (i, j, 0)),
                   pl.BlockSpec((1, 1, width), lambda i, j: (i, 0, 0)),
                   pl.BlockSpec((1, kw - 1, width), lambda i, j: (i, 0, 0))],
        out_shape=[jax.ShapeDtypeStruct((b, l, width), BF16),
                   jax.ShapeDtypeStruct((b, 1, width), F32),
                   jax.ShapeDtypeStruct((b, kw - 1, width), F32)],
        scratch_shapes=[pltpu.VMEM((SUBLANES + tl, width), F32), pltpu.VMEM((tl, width), F32),
                        pltpu.VMEM((tl, width), F32)],
        compiler_params=_params(2), name="rglru")(
            main3d, main3d, conv0, h0, cw, cb, wa, wx, ba, bx, lam, ng)


def _outproj_kernel(x_ref, og_ref, of_ref, ol_ref, w_ref, y_ref):
    wg, wf = og_ref.shape[1], of_ref.shape[1]
    y = x_ref[...] + jnp.dot(og_ref[...], w_ref[0:wg, :], preferred_element_type=F32)
    y = y + jnp.dot(of_ref[...], w_ref[wg:wg + wf, :], preferred_element_type=F32)
    y_ref[...] = y + jnp.dot(ol_ref[...], w_ref[wg + wf:, :], preferred_element_type=F32)


def _outproj(x2d, o_gdn, o_fox, o_lru, w, *, tm):
    m, d = x2d.shape
    row = lambda i: (i, 0)
    return pl.pallas_call(
        _outproj_kernel, grid=(m // tm,),
        in_specs=[pl.BlockSpec((tm, d), row), pl.BlockSpec((tm, o_gdn.shape[1]), row),
                  pl.BlockSpec((tm, o_fox.shape[1]), row), pl.BlockSpec((tm, o_lru.shape[1]), row),
                  _resident(w.shape, lambda i: (0, 0))],
        out_specs=pl.BlockSpec((tm, d), row),
        out_shape=jax.ShapeDtypeStruct((m, d), F32),
        compiler_params=_params(1), name="outproj")(x2d, o_gdn, o_fox, o_lru, w)


def _ffn_kernel(x_ref, g_ref, wg_ref, wv_ref, wd_ref, cw_ref, c0_ref, fg_ref, y_ref, cnew_ref,
                h_scr, gp_scr, *, nb, tiles_per_batch, tm, conv_w, final_norm):
    gi = pl.program_id(0)
    f = pl.program_id(1)
    hist = conv_w - 1
    lo = SUBLANES - hist

    rows = lambda j: slice(j * tm, (j + 1) * tm)

    @pl.when(f == 0)
    def _():
        for j in range(nb):
            x = x_ref[0, j]
            h_scr[rows(j), :] = _rms(x, g_ref[...]).astype(BF16)
            y_ref[0, j] = x

    if tiles_per_batch > 1:
        first = gi % tiles_per_batch == 0

        @pl.when(first)
        def _():
            gp_scr[0, lo:SUBLANES, :] = c0_ref[0]

        @pl.when(jnp.logical_not(first))
        def _():
            gp_scr[0, lo:SUBLANES, :] = cnew_ref[0, f]
    else:
        for j in range(nb):
            gp_scr[j, lo:SUBLANES, :] = c0_ref[j]

    tf = wg_ref.shape[1]
    half = tf // FFN_SPLIT
    h = h_scr[...]
    ups = []
    for c in range(FFN_SPLIT):
        cs = slice(c * half, (c + 1) * half)
        ups.append((jnp.dot(h, wg_ref[:, cs], preferred_element_type=F32),
                    jnp.dot(h, wv_ref[:, cs], preferred_element_type=F32)))
    for c in range(FFN_SPLIT):
        cs = slice(c * half, (c + 1) * half)
        gp, val = ups[c]
        acts = [jax.nn.silu(_causal_conv(gp_scr.at[j], gp[rows(j)], cw_ref, conv_w, tm, cs))
                * val[rows(j)] for j in range(nb)]
        act = (acts[0] if nb == 1 else jnp.concatenate(acts, axis=0)).astype(BF16)
        part = jnp.dot(act, wd_ref[cs, :], preferred_element_type=F32)
        for j in range(nb):
            y_ref[0, j] += part[rows(j)]
    for j in range(nb):
        cnew_ref[j, f] = gp_scr[j, lo + tm:SUBLANES + tm, :]

    if final_norm:
        @pl.when(f == pl.num_programs(1) - 1)
        def _():
            for j in range(nb):
                y_ref[0, j] = _rms(y_ref[0, j], fg_ref[...])


def _ffn(x3d, g, wg, wv, wd, cw, c0, fg, *, tm, tf, final_norm):
    b, l, d = x3d.shape
    dff = wg.shape[1]
    kw = cw.shape[0]
    tiles_per_batch = l // tm
    nb = b if tiles_per_batch == 1 else 1
    groups = b * l // (nb * tm)
    x4 = x3d.reshape(groups, nb, tm, d)
    nf = dff // tf
    if tiles_per_batch == 1:
        c_idx = lambda i, f: (0, 0, f)
        cnew_idx = lambda i, f: (0, 0, 0, 0)
    else:
        c_idx = lambda i, f: (i // tiles_per_batch, 0, f)
        cnew_idx = lambda i, f: (i // tiles_per_batch, 0, 0, 0)
    y, cnew = pl.pallas_call(
        functools.partial(_ffn_kernel, nb=nb, tiles_per_batch=tiles_per_batch, tm=tm, conv_w=kw,
                          final_norm=final_norm),
        grid=(groups, dff // tf),
        in_specs=[pl.BlockSpec((1, nb, tm, d), lambda i, f: (i, 0, 0, 0),
                               pipeline_mode=pl.Buffered(1)),
                  pl.BlockSpec((1, d), lambda i, f: (0, 0)),
                  pl.BlockSpec((d, tf), lambda i, f: (0, f)),
                  pl.BlockSpec((d, tf), lambda i, f: (0, f)),
                  pl.BlockSpec((tf, d), lambda i, f: (f, 0)),
                  pl.BlockSpec((kw, tf), lambda i, f: (0, f)),
                  pl.BlockSpec((nb, kw - 1, tf), c_idx),
                  pl.BlockSpec((1, d), lambda i, f: (0, 0))],
        out_specs=[pl.BlockSpec((1, nb, tm, d), lambda i, f: (i, 0, 0, 0)),
                   pl.BlockSpec((nb, nf, kw - 1, tf), cnew_idx)],
        out_shape=[jax.ShapeDtypeStruct((groups, nb, tm, d), F32),
                   jax.ShapeDtypeStruct((b, nf, kw - 1, tf), F32)],
        scratch_shapes=[pltpu.VMEM((nb * tm, d), BF16), pltpu.VMEM((nb, SUBLANES + tm, tf), F32)],
        compiler_params=_params(2), name="ffn")(x4, g, wg, wv, wd, cw, c0, fg)
    return y.reshape(b, l, d), jnp.swapaxes(cnew, 1, 2).reshape(b, kw - 1, dff)


def _pad_cols(a, width):
    return jnp.pad(a, ((0, 0), (0, width - a.shape[1])))


def _layer_weights(l, w_in, w_out, ffn_w_up, ffn_w_down, sizes):
    gdn_w, gdn_h, fox_w, fox_h, lru_w = sizes
    w = w_in[l]
    offs = [0]
    for s in (gdn_w, gdn_w, gdn_w, gdn_w, gdn_h, gdn_h, fox_w, fox_w, fox_w, fox_h, lru_w, lru_w):
        offs.append(offs[-1] + s)
    gq, gk, gv, gz, gb, ga, fq, fk, fv, ff, lx, lg = [w[:, offs[i]:offs[i + 1]] for i in range(12)]
    small = _pad_cols(jnp.concatenate([ff, gb, ga], axis=1), GATE_COLS)
    packed = jnp.concatenate([gq, gk, gv, gz, lx, lg, fq, fk, fv, small], axis=1).astype(BF16)
    dff = ffn_w_down.shape[1]
    return dict(w_in=packed, w_out=w_out[l].astype(BF16), w_gate=ffn_w_up[l, :, :dff].astype(BF16),
                w_val=ffn_w_up[l, :, dff:].astype(BF16), w_down=ffn_w_down[l].astype(BF16))


def _run_group(x, cache, states, p, weights, dims, tiles):
    gdn_w, gdn_h, fox_w, fox_h, lru_w = dims
    b, l, d = x.shape
    depth = len(weights)
    n_main = 4 * gdn_w + 2 * lru_w
    gdn_s, gdn_conv, lru_h, lru_conv, ffn_conv = states
    beta_col, a_col = GATE_F + fox_h, GATE_F + fox_h + gdn_h
    outs = [[] for _ in range(6)]
    kv = None
    for li in range(depth):
        wl = weights[li]
        main, small, fk, fv, qkv = _inproj(
            x.reshape(b * l, d), p['norm_mix_g'][li][None], wl['w_in'], kv,
            n_main=n_main, n_fox=fox_w, tm=tiles['proj'], layer=li, depth=depth)
        kv = (fk, fv)
        main3 = main.reshape(b, l, n_main)
        small3 = small.reshape(b, l, GATE_COLS)
        qkv3 = qkv.reshape(b, l, 3 * fox_w)
        bias_row = _pad_cols(p['fox_f_bias'][li][None], GATE_COLS)
        if cache is None:
            logf, ft = _gate(small3, bias_row, heads=fox_h, cumulative=True)
            o_fox = _fox_prompt(qkv3, ft, p['fox_norm_g'][li], heads=fox_h, tq=tiles['attn'])
        else:
            logf, = _gate(small3, bias_row, heads=fox_h, cumulative=False)
            clt = jnp.swapaxes(cache[2][li], 1, 2)
            lnt = jnp.pad(jnp.swapaxes(logf, 1, 2), ((0, 0), (0, 0), (0, LANES - l)))
            o_fox = _fox_sample(qkv3, cache[0], cache[1], clt, lnt, p['fox_norm_g'][li],
                                heads=fox_h, layer=li, tp=min(1024, cache[0].shape[2]))
        row = lambda v, c0: jnp.pad(v[None], ((0, 0), (c0, GATE_COLS - c0 - v.shape[0])))
        o_gdn, s_new, gconv_new = _gdn(
            main3, small3, gdn_conv[li], gdn_s[li], p['gdn_conv_w'][li],
            row(p['gdn_a_log'][li], a_col), row(p['gdn_dt_bias'][li], a_col),
            p['gdn_norm_g'][li][None], rows=tiles['gdn'], c=min(l, CHUNK), heads=gdn_h,
            beta_col=beta_col, a_col=a_col)
        o_lru, h_new, lconv_new = _lru(
            main3, lru_conv[li], lru_h[li][:, None, :], p['lru_conv_w'][li], p['lru_conv_b'][li][None],
            p['lru_w_a'][li].astype(BF16), p['lru_w_x'][li].astype(BF16), p['lru_b_a'][li][None],
            p['lru_b_x'][li][None], p['lru_lambda'][li][None], p['lru_norm_g'][li][None],
            tl=tiles['lru'], x_block=4 * gdn_w // lru_w, gate_block=4 * gdn_w // lru_w + 1)
        x1 = _outproj(x.reshape(b * l, d), o_gdn.reshape(b * l, gdn_w), o_fox.reshape(b * l, fox_w),
                      o_lru.reshape(b * l, lru_w), wl['w_out'], tm=tiles['proj'])
        x, fconv_new = _ffn(
            x1.reshape(b, l, d), p['norm_ffn_g'][li][None], wl['w_gate'], wl['w_val'], wl['w_down'],
            p['ffn_conv_w'][li], ffn_conv[li], p['final_norm_g'][None],
            tm=tiles['ffn'], tf=512, final_norm=li == depth - 1)
        new = [logf, s_new, gconv_new, h_new[:, 0, :], lconv_new, fconv_new]
        for lst, arr in zip(outs, new):
            lst.append(arr)
    kv_shape = (depth, b, l, fox_h, HEAD_DIM)
    return x, [kv[0].reshape(kv_shape), kv[1].reshape(kv_shape)] + [jnp.stack(lst) for lst in outs]


def kernel(x_prompt, x_sample, cache_fox_k, cache_fox_v, cache_fox_logf, state_gdn, state_gdn_conv,
           state_lru, state_lru_conv, state_ffn_conv, norm_mix_g, w_in, gdn_conv_w, gdn_a_log,
           gdn_dt_bias, gdn_norm_g, fox_f_bias, fox_norm_g, lru_conv_w, lru_conv_b, lru_w_a, lru_b_a,
           lru_w_x, lru_b_x, lru_lambda, lru_norm_g, w_out, norm_ffn_g, ffn_w_up, ffn_conv_w,
           ffn_w_down, final_norm_g):
    p = dict(norm_mix_g=norm_mix_g, gdn_conv_w=gdn_conv_w, gdn_a_log=gdn_a_log,
             gdn_dt_bias=gdn_dt_bias, gdn_norm_g=gdn_norm_g, fox_f_bias=fox_f_bias,
             fox_norm_g=fox_norm_g, lru_conv_w=lru_conv_w, lru_conv_b=lru_conv_b, lru_w_a=lru_w_a,
             lru_b_a=lru_b_a, lru_w_x=lru_w_x, lru_b_x=lru_b_x, lru_lambda=lru_lambda,
             lru_norm_g=lru_norm_g, norm_ffn_g=norm_ffn_g, ffn_conv_w=ffn_conv_w,
             final_norm_g=final_norm_g)
    depth = w_in.shape[0]
    bp, lp, _ = x_prompt.shape
    gdn_h = state_gdn.shape[2]
    fox_h = cache_fox_k.shape[3]
    lru_w = state_lru.shape[2]
    dims = (gdn_h * HEAD_DIM, gdn_h, fox_h * HEAD_DIM, fox_h, lru_w)
    dff = ffn_w_down.shape[1]
    weights = [_layer_weights(l, w_in, w_out, ffn_w_up, ffn_w_down, dims) for l in range(depth)]

    zeros = lambda *s: jnp.zeros((depth, bp) + s, F32)
    prompt_states = (zeros(gdn_h, HEAD_DIM, HEAD_DIM), zeros(gdn_conv_w.shape[1] - 1, 3 * dims[0]),
                     zeros(lru_w), zeros(lru_conv_w.shape[1] - 1, lru_w),
                     zeros(ffn_conv_w.shape[1] - 1, dff))
    prompt_tiles = dict(proj=min(256, bp * lp), attn=min(512, lp), lru=min(256, lp), ffn=min(1024, lp),
                        gdn=min(256, lp))
    y_prompt, sp = _run_group(x_prompt, None, prompt_states, p, weights, dims, prompt_tiles)

    bs, ls, _ = x_sample.shape
    sample_tiles = dict(proj=bs * ls, lru=ls, ffn=ls, gdn=ls)
    y_sample, ss = _run_group(
        x_sample, (cache_fox_k, cache_fox_v, cache_fox_logf),
        (state_gdn, state_gdn_conv, state_lru, state_lru_conv, state_ffn_conv), p, weights, dims,
        sample_tiles)
    return (y_prompt, y_sample, *sp, *ss)
```

```python
import functools
import math

import jax
import jax.numpy as jnp
from jax import lax
from jax.experimental import pallas as pl
from jax.experimental.pallas import tpu as pltpu

F32 = jnp.float32
BF16 = jnp.bfloat16

HEAD_DIM = 128
CHUNK = 64
LRU_C = 8.0
EPS = 1e-6
LANES = 128
SUBLANES = 8
VMEM_LIMIT = 56 * 1024 * 1024

GATE_COLS = LANES
GATE_F = 0
FFN_SPLIT = 2
NEG_INF = float("-inf")
LOG2E = math.log2(math.e)


def _params(n_axes):
    return pltpu.CompilerParams(dimension_semantics=("arbitrary",) * n_axes,
                                vmem_limit_bytes=VMEM_LIMIT)


def _resident(block_shape, index_map):
    return pl.BlockSpec(block_shape, index_map, pipeline_mode=pl.Buffered(1))


def _dot(a, b):
    return jnp.dot(a.astype(BF16), b.astype(BF16), preferred_element_type=F32)


def _dot_nt(a, b):
    return lax.dot_general(a.astype(BF16), b.astype(BF16), (((1,), (1,)), ((), ())),
                           preferred_element_type=F32)


def _dot_tn(a, b):
    return lax.dot_general(a.astype(BF16), b.astype(BF16), (((0,), (0,)), ((), ())),
                           preferred_element_type=F32)


def _dot_f32(a, b):
    return jnp.dot(a, b, preferred_element_type=F32, precision=lax.Precision.HIGHEST)


def _rms(x, g):
    return x * lax.rsqrt(jnp.mean(x * x, axis=-1, keepdims=True) + EPS) * g


def _expm1(x):
    u = jnp.exp(x)
    um1 = u - 1.0
    near = jnp.where(u == 1.0, x, um1 * x / jnp.log(u))
    return jnp.where(jnp.abs(x) > 0.5, um1, near)


def _cumsum_lanes(x, seg=None):
    n = x.shape[-1]
    lane = lax.broadcasted_iota(jnp.int32, x.shape, x.ndim - 1)
    pos = lane if seg is None else lane % seg
    s = 1
    while s < (n if seg is None else seg):
        x = x + jnp.where(pos >= s, pltpu.roll(x, s, x.ndim - 1), 0.0)
        s *= 2
    return x


def _cumsum_rows(x, seg):
    pos = lax.broadcasted_iota(jnp.int32, x.shape, 0) % seg
    s = 1
    while s < seg:
        x = x + jnp.where(pos >= s, pltpu.roll(x, s, 0), 0.0)
        s *= 2
    return x


def _causal_conv(buf, x, w_ref, width, rows, cols=slice(None)):
    lo = SUBLANES - (width - 1)
    buf[SUBLANES:SUBLANES + rows, cols] = x
    y = w_ref[width - 1:width, cols] * x
    for j in range(width - 1):
        y = y + w_ref[j:j + 1, cols] * buf[lo + j:lo + j + rows, cols]
    return y


def _inproj_kernel(*refs, n_main, n_fox, seg, layer):
    x_ref, g_ref, w_ref = refs[:3]
    n_in = 5 if layer else 3
    main_ref, small_ref, fk_ref, fv_ref, qkv_ref, h_scr = refs[n_in:]
    if layer:
        fk_ref[:layer] = refs[3][...]
        fv_ref[:layer] = refs[4][...]
    h_scr[...] = _rms(x_ref[...], g_ref[...]).astype(BF16)

    def cols(c0, width):
        return jnp.dot(h_scr[...], w_ref[:, c0:c0 + width], preferred_element_type=F32)

    for c0 in range(0, n_main, seg):
        main_ref[:, c0:c0 + seg] = cols(c0, seg)
    for j in range(3):
        for c in range(0, n_fox, seg):
            p = cols(n_main + j * n_fox + c, seg)
            qkv_ref[:, j * n_fox + c:j * n_fox + c + seg] = p.astype(BF16)
            if j == 1:
                fk_ref[layer, :, c:c + seg] = p
            if j == 2:
                fv_ref[layer, :, c:c + seg] = p
    small_ref[...] = cols(n_main + 3 * n_fox, GATE_COLS)


def _inproj(x2d, g, w, kv_prev, *, n_main, n_fox, tm, layer):
    m, d = x2d.shape
    n_all = w.shape[2]
    row = lambda i: (i, 0)
    kv_row = lambda i: (0, i, 0)
    in_specs = [pl.BlockSpec((tm, d), row), _resident((None, 1, d), lambda i: (layer, 0, 0)),
                _resident((None, d, n_all), lambda i: (layer, 0, 0))]
    operands = [x2d, g, w]
    if layer:
        in_specs += [pl.BlockSpec((layer, tm, n_fox), kv_row)] * 2
        operands += list(kv_prev)
    kv_out = jax.ShapeDtypeStruct((layer + 1, m, n_fox), F32)
    return pl.pallas_call(
        functools.partial(_inproj_kernel, n_main=n_main, n_fox=n_fox, seg=512, layer=layer),
        grid=(m // tm,),
        in_specs=in_specs,
        out_specs=[pl.BlockSpec((tm, n_main), row), pl.BlockSpec((tm, GATE_COLS), row),
                   pl.BlockSpec((layer + 1, tm, n_fox), kv_row),
                   pl.BlockSpec((layer + 1, tm, n_fox), kv_row),
                   pl.BlockSpec((tm, 3 * n_fox), row)],
        out_shape=[jax.ShapeDtypeStruct((m, n_main), F32), jax.ShapeDtypeStruct((m, GATE_COLS), F32),
                   kv_out, kv_out, jax.ShapeDtypeStruct((m, 3 * n_fox), BF16)],
        scratch_shapes=[pltpu.VMEM((tm, d), BF16)],
        compiler_params=_params(1), name="inproj")(*operands)


def _gate_kernel(sm_ref, b_ref, logf_ref, *rest, heads, cumulative):
    ls = jax.nn.log_sigmoid(sm_ref[0] + b_ref[...])
    logf_ref[0] = ls[:, GATE_F:GATE_F + heads]
    if cumulative:
        ft_ref, = rest
        ft_ref[0] = _cumsum_lanes(ls.T[GATE_F:GATE_F + heads, :])


def _gate(small3d, bias_row, *, heads, cumulative):
    b, rows, _ = small3d.shape
    out_specs = [pl.BlockSpec((1, rows, heads), lambda i: (i, 0, 0))]
    out_shape = [jax.ShapeDtypeStruct((b, rows, heads), F32)]
    if cumulative:
        out_specs.append(pl.BlockSpec((1, heads, rows), lambda i: (i, 0, 0)))
        out_shape.append(jax.ShapeDtypeStruct((b, heads, rows), F32))
    return pl.pallas_call(
        functools.partial(_gate_kernel, heads=heads, cumulative=cumulative),
        grid=(b,),
        in_specs=[pl.BlockSpec((1, rows, GATE_COLS), lambda i: (i, 0, 0)),
                  pl.BlockSpec((1, GATE_COLS), lambda i: (0, 0))],
        out_specs=out_specs, out_shape=out_shape,
        compiler_params=_params(1), name="fox_gate")(small3d, bias_row)


def _split_bf16(x):
    hi = x.astype(BF16)
    return hi, (x - hi.astype(F32)).astype(BF16)


def _dot_split(a, b):
    a_hi, a_lo = _split_bf16(a)
    b_hi, b_lo = _split_bf16(b)
    mm = lambda x, y: jnp.dot(x, y, preferred_element_type=F32)
    return mm(a_hi, b_hi) + (mm(a_lo, b_hi) + mm(a_hi, b_lo))


def _unit_lower_solve(a, rhs, c):
    r = a[0].shape[0]
    row = lax.broadcasted_iota(jnp.int32, (r, r), 0)
    col = lax.broadcasted_iota(jnp.int32, (r, r), 1)
    eye = jnp.where(row == col, 1.0, 0.0)
    p = [-x for x in a]
    t = [eye + x for x in p]
    for _ in range(int(math.log2(c)) - 1):
        p = [_dot(x, x) for x in p]
        t = [ti + _dot(ti, pi) for ti, pi in zip(t, p)]
    x = [_dot(ti, bi) for ti, bi in zip(t, rhs)]
    resid = [bi - xi - _dot_split(ai, xi) for ai, bi, xi in zip(a, rhs, x)]
    return [xi + _dot(ti, ri) for ti, xi, ri in zip(t, x, resid)]


def _gdn_kernel(qkv_ref, z_ref, sm_ref, conv0_ref, s0_ref, cw_ref, alog_ref, dt_ref, ng_ref,
                o_ref, s_ref, conv_ref, buf, *, rows, c, heads, conv_w, beta_col, a_col):
    ti = pl.program_id(1)
    nc = rows // c
    width = heads * HEAD_DIM
    hist = conv_w - 1
    lo = SUBLANES - hist

    @pl.when(ti == 0)
    def _():
        buf[lo:SUBLANES, :] = conv0_ref[0]
        s_ref[0] = s0_ref[0]

    y = _causal_conv(buf, qkv_ref[0], cw_ref, conv_w, rows)
    new_hist = buf[lo + rows:SUBLANES + rows, :]
    conv_ref[0] = new_hist
    buf[lo:SUBLANES, :] = new_hist
    y = jax.nn.silu(y)

    sm = sm_ref[0]
    beta = jax.nn.sigmoid(sm)
    g = -jnp.exp(alog_ref[...]) * jax.nn.softplus(sm + dt_ref[...])
    gcum = _cumsum_rows(g, c)
    g_end = jnp.concatenate(
        [jnp.broadcast_to(gcum[(i + 1) * c - 1:(i + 1) * c, :], (c, GATE_COLS)) for i in range(nc)],
        axis=0)
    pad = max(LANES - rows, 0)
    gcum_t = jnp.concatenate([gcum, jnp.zeros((pad, GATE_COLS), F32)], axis=0).T if pad else gcum.T

    row = lax.broadcasted_iota(jnp.int32, (rows, rows), 0)
    col = lax.broadcasted_iota(jnp.int32, (rows, rows), 1)
    lower = row >= col
    if nc > 1:
        lower = lower & (row // c == col // c)
    strict = lower & (row != col)
    z = z_ref[0]
    hs = range(heads)
    head = lambda base, h: y[:, base + h * HEAD_DIM:base + (h + 1) * HEAD_DIM]
    l2 = lambda t: t * lax.rsqrt(jnp.sum(t * t, axis=-1, keepdims=True) + EPS)
    q = [l2(head(0, h)) * HEAD_DIM ** -0.5 for h in hs]
    k = [l2(head(width, h)) for h in hs]
    v = [head(2 * width, h) for h in hs]
    bh = [beta[:, beta_col + h:beta_col + h + 1] for h in hs]
    gc = [gcum[:, a_col + h:a_col + h + 1] for h in hs]
    ge = [g_end[:, a_col + h:a_col + h + 1] for h in hs]
    gr = [gcum_t[a_col + h:a_col + h + 1, :rows] for h in hs]
    decay = [jnp.exp(jnp.where(lower, gc[h] - gr[h], NEG_INF)) for h in hs]
    kb = [k[h] * bh[h] for h in hs]
    eg = [jnp.exp(gc[h]) for h in hs]
    a = [_dot_nt(kb[h], k[h]) * jnp.where(strict, decay[h], 0.0) for h in hs]
    attn = [_dot_nt(q[h], k[h]) * decay[h] for h in hs]
    sol = _unit_lower_solve(a, [jnp.concatenate([v[h] * bh[h], kb[h] * eg[h]], axis=-1) for h in hs], c)
    qe = [q[h] * eg[h] for h in hs]
    kx = [k[h] * jnp.exp(ge[h] - gc[h]) for h in hs]
    s = [s_ref[0, h] for h in hs]
    deltas = [[] for _ in hs]
    inter = [[] for _ in hs]
    for i in range(nc):
        rs = slice(i * c, (i + 1) * c)
        d_i = [sol[h][rs, :HEAD_DIM] - _dot(sol[h][rs, HEAD_DIM:], s[h]) for h in hs]
        for h in hs:
            inter[h].append(_dot(qe[h][rs], s[h]))
            deltas[h].append(d_i[h])
        s = [jnp.exp(ge[h][(i + 1) * c - 1:(i + 1) * c, :]) * s[h] + _dot_tn(kx[h][rs], d_i[h])
             for h in hs]
    cat = lambda parts: parts[0] if nc == 1 else jnp.concatenate(parts, axis=0)
    for h in hs:
        s_ref[0, h] = s[h]
        o = cat(inter[h]) + _dot(attn[h], cat(deltas[h]))
        zh = z[:, h * HEAD_DIM:(h + 1) * HEAD_DIM]
        o_ref[0, :, h * HEAD_DIM:(h + 1) * HEAD_DIM] = (
            _rms(o, ng_ref[...]) * jax.nn.silu(zh)).astype(o_ref.dtype)


def _gdn(main3d, small3d, conv0, s0, conv_w, alog_row, dt_row, norm_g, *, rows, c, heads, beta_col,
         a_col):
    b, l, _ = main3d.shape
    width = heads * HEAD_DIM
    kw = conv_w.shape[0]
    fixed2 = lambda i, j: (0, 0)
    return pl.pallas_call(
        functools.partial(_gdn_kernel, rows=rows, c=c, heads=heads, conv_w=kw, beta_col=beta_col,
                          a_col=a_col),
        grid=(b, l // rows),
        in_specs=[pl.BlockSpec((1, rows, 3 * width), lambda i, j: (i, j, 0)),
                  pl.BlockSpec((1, rows, width), lambda i, j: (i, j, 3)),
                  pl.BlockSpec((1, rows, GATE_COLS), lambda i, j: (i, j, 0)),
                  pl.BlockSpec((1, kw - 1, 3 * width), lambda i, j: (i, 0, 0)),
                  pl.BlockSpec((1, heads, HEAD_DIM, HEAD_DIM), lambda i, j: (i, 0, 0, 0)),
                  pl.BlockSpec((kw, 3 * width), fixed2),
                  pl.BlockSpec((1, GATE_COLS), fixed2), pl.BlockSpec((1, GATE_COLS), fixed2),
                  pl.BlockSpec((1, HEAD_DIM), fixed2)],
        out_specs=[pl.BlockSpec((1, rows, width), lambda i, j: (i, j, 0)),
                   pl.BlockSpec((1, heads, HEAD_DIM, HEAD_DIM), lambda i, j: (i, 0, 0, 0)),
                   pl.BlockSpec((1, kw - 1, 3 * width), lambda i, j: (i, 0, 0))],
        out_shape=[jax.ShapeDtypeStruct((b, l, width), BF16),
                   jax.ShapeDtypeStruct((b, heads, HEAD_DIM, HEAD_DIM), F32),
                   jax.ShapeDtypeStruct((b, kw - 1, 3 * width), F32)],
        scratch_shapes=[pltpu.VMEM((SUBLANES + rows, 3 * width), F32)],
        compiler_params=_params(2), name="gdn")(
            main3d, main3d, small3d, conv0, s0, conv_w, alog_row, dt_row, norm_g)


def _fox_prompt_kernel(q_ref, k_ref, v_ref, ft_ref, g_ref, o_ref, vaug, m_scr, acc_scr, s_a, s_b,
                       *, tq):
    nq = q_ref.shape[1] // tq
    vaug[:, :HEAD_DIM] = v_ref[0]
    vaug[:, HEAD_DIM:] = jnp.ones((vaug.shape[0], HEAD_DIM), vaug.dtype)
    scale = HEAD_DIM ** -0.5

    def block_start(j):
        return j * tq if isinstance(j, int) else pl.multiple_of(j * tq, tq)

    def scores(buf, qi, j):
        buf[...] = _dot_nt(q_ref[0, qi * tq:(qi + 1) * tq, :], k_ref[0, pl.ds(block_start(j), tq), :])

    def accumulate(buf, j, masked):
        start = block_start(j)
        s = buf[...] * (scale * LOG2E) - ft_ref[0, 0, pl.ds(j, 1), :] * LOG2E
        if masked:
            row = lax.broadcasted_iota(jnp.int32, (tq, tq), 0)
            col = lax.broadcasted_iota(jnp.int32, (tq, tq), 1)
            s = jnp.where(col <= row, s, NEG_INF)
        m_prev = m_scr[...]
        m_new = jnp.maximum(m_prev, jnp.max(s, axis=-1, keepdims=True))
        alpha = jnp.exp2(m_prev - m_new)
        p = jnp.exp2(s - jnp.concatenate([m_new] * (tq // LANES), axis=1))
        pv = _dot(p, vaug[pl.ds(start, tq), :])
        acc_scr[...] = jnp.concatenate([alpha] * (2 * HEAD_DIM // LANES), axis=1) * acc_scr[...] + pv
        m_scr[...] = m_new

    cur, other = s_a, s_b
    scores(cur, 0, 0)
    for qi in range(nq):
        m_scr[...] = jnp.full(m_scr.shape, NEG_INF, F32)
        acc_scr[...] = jnp.zeros(acc_scr.shape, F32)

        def pair(jj, carry, qi=qi, cur=cur, other=other):
            j = 2 * jj
            scores(other, qi, j + 1)
            accumulate(cur, j, False)
            scores(cur, qi, j + 2)
            accumulate(other, j + 1, False)
            return carry

        if qi >= 2:
            lax.fori_loop(0, qi // 2, pair, 0)
        if qi % 2 == 0:
            if qi + 1 < nq:
                scores(other, qi + 1, 0)
            accumulate(cur, qi, True)
            cur, other = other, cur
        else:
            scores(other, qi, qi)
            accumulate(cur, qi - 1, False)
            if qi + 1 < nq:
                scores(cur, qi + 1, 0)
            accumulate(other, qi, True)
        o = acc_scr[:, :HEAD_DIM] / acc_scr[:, HEAD_DIM:]
        o_ref[0, qi * tq:(qi + 1) * tq, :] = _rms(o, g_ref[0]).astype(o_ref.dtype)


def _fox_prompt(qkv3d, ft, norm_g, *, heads, tq):
    b, l, _ = qkv3d.shape
    ft4 = ft.reshape(b, heads, l // tq, tq)
    g3 = norm_g.reshape(heads, 1, HEAD_DIM)
    head_block = lambda part: pl.BlockSpec((1, l, HEAD_DIM), lambda i, h: (i, 0, part * heads + h))
    return pl.pallas_call(
        functools.partial(_fox_prompt_kernel, tq=tq),
        grid=(b, heads),
        in_specs=[head_block(0), head_block(1), head_block(2),
                  pl.BlockSpec((1, 1, l // tq, tq), lambda i, h: (i, h, 0, 0)),
                  pl.BlockSpec((1, 1, HEAD_DIM), lambda i, h: (h, 0, 0))],
        out_specs=pl.BlockSpec((1, l, HEAD_DIM), lambda i, h: (i, 0, h)),
        out_shape=jax.ShapeDtypeStruct((b, l, heads * HEAD_DIM), BF16),
        scratch_shapes=[pltpu.VMEM((l, 2 * HEAD_DIM), BF16), pltpu.VMEM((tq, LANES), F32),
                        pltpu.VMEM((tq, 2 * HEAD_DIM), F32), pltpu.VMEM((tq, tq), F32),
                        pltpu.VMEM((tq, tq), F32)],
        compiler_params=_params(2), name="fox_prompt")(qkv3d, qkv3d, qkv3d, ft4, g3)


def _fox_sample_kernel(q_ref, kn_ref, vn_ref, ck_ref, cv_ref, clt_ref, lnt_ref, g_ref, o_ref,
                       fc_scr, m_scr, l_scr, acc_scr, *, ls, heads, tp):
    j = pl.program_id(1)
    last = pl.num_programs(1) - 1
    scale = HEAD_DIM ** -0.5

    @pl.when(j == 0)
    def _():
        fc = _cumsum_lanes(clt_ref[0])
        for c in range(fc_scr.shape[0]):
            fc_scr[c] = fc[:, c * tp:(c + 1) * tp]
        m_scr[...] = jnp.full(m_scr.shape, NEG_INF, F32)
        l_scr[...] = jnp.zeros(l_scr.shape, F32)
        acc_scr[...] = jnp.zeros(acc_scr.shape, F32)

    def update(h, s, v):
        m_prev = m_scr[h]
        m_new = jnp.maximum(m_prev, jnp.max(s, axis=-1, keepdims=True))
        alpha = jnp.exp(m_prev - m_new)
        p = jnp.exp(s - m_new)
        l_scr[h] = alpha * l_scr[h] + jnp.sum(p, axis=-1, keepdims=True)
        acc_scr[h] = alpha * acc_scr[h] + _dot(p, v)
        m_scr[h] = m_new

    fcj = fc_scr[j]
    for h in range(heads):
        cs = slice(h * HEAD_DIM, (h + 1) * HEAD_DIM)
        k_h = ck_ref[0, pl.ds(h, tp, stride=heads), :]
        v_h = cv_ref[0, pl.ds(h, tp, stride=heads), :]
        update(h, _dot_nt(q_ref[0, :, cs], k_h) * scale - fcj[h:h + 1, :], v_h)

    @pl.when(j == last)
    def _():
        fn = fcj[:, tp - 1:tp] + _cumsum_lanes(lnt_ref[0])[:, :ls]
        row = lax.broadcasted_iota(jnp.int32, (ls, ls), 0)
        col = lax.broadcasted_iota(jnp.int32, (ls, ls), 1)
        for h in range(heads):
            cs = slice(h * HEAD_DIM, (h + 1) * HEAD_DIM)
            s_n = _dot_nt(q_ref[0, :, cs], kn_ref[0, :, cs]) * scale - fn[h:h + 1, :]
            update(h, jnp.where(col <= row, s_n, NEG_INF), vn_ref[0, :, cs])
            o = acc_scr[h] / l_scr[h]
            o_ref[0, :, cs] = _rms(o, g_ref[h:h + 1, :]).astype(o_ref.dtype)


def _fox_sample(qkv3d, cache_k, cache_v, cache_logf_t, logf_new_t, norm_g, *, heads, layer, tp):
    b, ls, _ = qkv3d.shape
    depth, _, p_len = cache_k.shape[:3]
    width = heads * HEAD_DIM
    ck = cache_k.reshape(depth * b, p_len * heads, HEAD_DIM)
    cv = cache_v.reshape(depth * b, p_len * heads, HEAD_DIM)
    cache_idx = lambda i, j: (layer * b + i, j, 0)
    return pl.pallas_call(
        functools.partial(_fox_sample_kernel, ls=ls, heads=heads, tp=tp),
        grid=(b, p_len // tp),
        in_specs=[pl.BlockSpec((1, ls, width), lambda i, j: (i, 0, 0)),
                  pl.BlockSpec((1, ls, width), lambda i, j: (i, 0, 1)),
                  pl.BlockSpec((1, ls, width), lambda i, j: (i, 0, 2)),
                  pl.BlockSpec((1, tp * heads, HEAD_DIM), cache_idx),
                  pl.BlockSpec((1, tp * heads, HEAD_DIM), cache_idx),
                  pl.BlockSpec((1, heads, p_len), lambda i, j: (i, 0, 0)),
                  pl.BlockSpec((1, heads, LANES), lambda i, j: (i, 0, 0)),
                  pl.BlockSpec((heads, HEAD_DIM), lambda i, j: (0, 0))],
        out_specs=pl.BlockSpec((1, ls, width), lambda i, j: (i, 0, 0)),
        out_shape=jax.ShapeDtypeStruct((b, ls, width), BF16),
        scratch_shapes=[pltpu.VMEM((p_len // tp, heads, tp), F32), pltpu.VMEM((heads, ls, 1), F32),
                        pltpu.VMEM((heads, ls, 1), F32), pltpu.VMEM((heads, ls, HEAD_DIM), F32)],
        compiler_params=_params(2), name="fox_sample")(
            qkv3d, qkv3d, qkv3d, ck, cv, cache_logf_t, logf_new_t, norm_g)


def _lru_kernel(x_ref, gate_ref, conv0_ref, h0_ref, cw_ref, cb_ref, wa_ref, wx_ref, ba_ref, bx_ref,
                lam_ref, ng_ref, o_ref, h_ref, conv_ref, buf, a_scr, hs_scr, *, tl, blocks, conv_w):
    ti = pl.program_id(1)
    hist = conv_w - 1
    lo = SUBLANES - hist
    bw = x_ref.shape[2] // blocks

    @pl.when(ti == 0)
    def _():
        buf[lo:SUBLANES, :] = conv0_ref[0]
        h_ref[0] = h0_ref[0]

    xc = _causal_conv(buf, x_ref[0], cw_ref, conv_w, tl) + cb_ref[...]
    new_hist = buf[lo + tl:SUBLANES + tl, :]
    conv_ref[0] = new_hist
    buf[lo:SUBLANES, :] = new_hist

    for n in range(blocks):
        sl = slice(n * bw, (n + 1) * bw)
        xb = xc[:, sl]
        r = jax.nn.sigmoid(_dot(xb, wa_ref[n]) + ba_ref[:, sl])
        i = jax.nn.sigmoid(_dot(xb, wx_ref[n]) + bx_ref[:, sl])
        log_a = -LRU_C * r * jax.nn.softplus(-lam_ref[:, sl])
        a_scr[:, sl] = jnp.exp(log_a)
        hs_scr[:, sl] = jnp.sqrt(-_expm1(2.0 * log_a)) * (i * xb)

    row = lax.broadcasted_iota(jnp.int32, (SUBLANES, a_scr.shape[1]), 0)

    def group(gi, h):
        start = pl.multiple_of(gi * SUBLANES, SUBLANES)
        a = a_scr[pl.ds(start, SUBLANES), :]
        u = hs_scr[pl.ds(start, SUBLANES), :]
        s = 1
        while s < SUBLANES:
            keep = row >= s
            u = jnp.where(keep, u + a * pltpu.roll(u, s, 0), u)
            a = jnp.where(keep, a * pltpu.roll(a, s, 0), a)
            s *= 2
        hg = u + a * h
        hs_scr[pl.ds(start, SUBLANES), :] = hg
        return hg[SUBLANES - 1:SUBLANES, :]

    h_ref[0] = lax.fori_loop(0, tl // SUBLANES, group, h_ref[0], unroll=4 if tl >= 32 else True)

    gate = gate_ref[0]
    for n in range(blocks):
        sl = slice(n * bw, (n + 1) * bw)
        o_ref[0, :, sl] = (_rms(hs_scr[:, sl], ng_ref[:, sl])
                           * jax.nn.gelu(gate[:, sl])).astype(o_ref.dtype)


def _lru(main3d, conv0, h0, cw, cb, wa, wx, ba, bx, lam, ng, *, tl, x_block, gate_block):
    b, l, _ = main3d.shape
    blocks, bw, _ = wa.shape
    width = blocks * bw
    kw = cw.shape[0]
    fixed2 = lambda i, j: (0, 0)
    fixed3 = lambda i, j: (0, 0, 0)
    vec = pl.BlockSpec((1, width), fixed2)
    return pl.pallas_call(
        functools.partial(_lru_kernel, tl=tl, blocks=blocks, conv_w=kw),
        grid=(b, l // tl),
        in_specs=[pl.BlockSpec((1, tl, width), lambda i, j: (i, j, x_block)),
                  pl.BlockSpec((1, tl, width), lambda i, j: (i, j, gate_block)),
                  pl.BlockSpec((1, kw - 1, width), lambda i, j: (i, 0, 0)),
                  pl.BlockSpec((1, 1, width), lambda i, j: (i, 0, 0)),
                  pl.BlockSpec((kw, width), fixed2), vec,
                  pl.BlockSpec((blocks, bw, bw), fixed3), pl.BlockSpec((blocks, bw, bw), fixed3),
                  vec, vec, vec, vec],
        out_specs=[pl.BlockSpec((1, tl, width), lambda i, j: (i, j, 0)),
                   pl.BlockSpec((1, 1, width), lambda i, j: (i, 0, 0)),
                   pl.BlockSpec((1, kw - 1, width), lambda i, j: (i, 0, 0))],
        out_shape=[jax.ShapeDtypeStruct((b, l, width), BF16),
                   jax.ShapeDtypeStruct((b, 1, width), F32),
                   jax.ShapeDtypeStruct((b, kw - 1, width), F32)],
        scratch_shapes=[pltpu.VMEM((SUBLANES + tl, width), F32), pltpu.VMEM((tl, width), F32),
                        pltpu.VMEM((tl, width), F32)],
        compiler_params=_params(2), name="rglru")(
            main3d, main3d, conv0, h0, cw, cb, wa, wx, ba, bx, lam, ng)


def _outproj_kernel(x_ref, og_ref, of_ref, ol_ref, w_ref, y_ref):
    wg, wf = og_ref.shape[1], of_ref.shape[1]
    y = x_ref[...] + jnp.dot(og_ref[...], w_ref[0:wg, :], preferred_element_type=F32)
    y = y + jnp.dot(of_ref[...], w_ref[wg:wg + wf, :], preferred_element_type=F32)
    y_ref[...] = y + jnp.dot(ol_ref[...], w_ref[wg + wf:, :], preferred_element_type=F32)


def _outproj(x2d, o_gdn, o_fox, o_lru, w, *, tm, layer):
    m, d = x2d.shape
    row = lambda i: (i, 0)
    return pl.pallas_call(
        _outproj_kernel, grid=(m // tm,),
        in_specs=[pl.BlockSpec((tm, d), row), pl.BlockSpec((tm, o_gdn.shape[1]), row),
                  pl.BlockSpec((tm, o_fox.shape[1]), row), pl.BlockSpec((tm, o_lru.shape[1]), row),
                  _resident((None,) + w.shape[1:], lambda i: (layer, 0, 0))],
        out_specs=pl.BlockSpec((tm, d), row),
        out_shape=jax.ShapeDtypeStruct((m, d), F32),
        compiler_params=_params(1), name="outproj")(x2d, o_gdn, o_fox, o_lru, w)


def _ffn_kernel(x_ref, g_ref, wg_ref, wv_ref, wd_ref, cw_ref, c0_ref, fg_ref, y_ref, cnew_ref,
                h_scr, gp_scr, *, nb, tiles_per_batch, tm, conv_w, final_norm):
    gi = pl.program_id(0)
    f = pl.program_id(1)
    hist = conv_w - 1
    lo = SUBLANES - hist

    rows = lambda j: slice(j * tm, (j + 1) * tm)

    @pl.when(f == 0)
    def _():
        for j in range(nb):
            x = x_ref[0, j]
            h_scr[rows(j), :] = _rms(x, g_ref[...]).astype(BF16)
            y_ref[0, j] = x

    if tiles_per_batch > 1:
        first = gi % tiles_per_batch == 0

        @pl.when(first)
        def _():
            gp_scr[0, lo:SUBLANES, :] = c0_ref[0]

        @pl.when(jnp.logical_not(first))
        def _():
            gp_scr[0, lo:SUBLANES, :] = cnew_ref[0, f]
    else:
        for j in range(nb):
            gp_scr[j, lo:SUBLANES, :] = c0_ref[j]

    tf = wg_ref.shape[1]
    half = tf // FFN_SPLIT
    h = h_scr[...]
    ups = []
    for c in range(FFN_SPLIT):
        cs = slice(c * half, (c + 1) * half)
        ups.append((jnp.dot(h, wg_ref[:, cs], preferred_element_type=F32),
                    jnp.dot(h, wv_ref[:, cs], preferred_element_type=F32)))
    contrib = None
    for c in range(FFN_SPLIT):
        cs = slice(c * half, (c + 1) * half)
        gp, val = ups[c]
        acts = [jax.nn.silu(_causal_conv(gp_scr.at[j], gp[rows(j)], cw_ref, conv_w, tm, cs))
                * val[rows(j)] for j in range(nb)]
        act = (acts[0] if nb == 1 else jnp.concatenate(acts, axis=0)).astype(BF16)
        part = jnp.dot(act, wd_ref[cs, :], preferred_element_type=F32)
        contrib = part if contrib is None else contrib + part
    for j in range(nb):
        cnew_ref[j, f] = gp_scr[j, lo + tm:SUBLANES + tm, :]
        y_ref[0, j] += contrib[rows(j)]

    if final_norm:
        @pl.when(f == pl.num_programs(1) - 1)
        def _():
            for j in range(nb):
                y_ref[0, j] = _rms(y_ref[0, j], fg_ref[...])


def _ffn(x3d, g, w_up, w_down, cw, c0, fg, *, tm, tf, final_norm, layer):
    b, l, d = x3d.shape
    dff = w_down.shape[1]
    kw = cw.shape[1]
    tiles_per_batch = l // tm
    nb = b if tiles_per_batch == 1 else 1
    groups = b * l // (nb * tm)
    x4 = x3d.reshape(groups, nb, tm, d)
    nf = dff // tf
    if tiles_per_batch == 1:
        c_idx = lambda i, f: (0, 0, f)
        cnew_idx = lambda i, f: (0, 0, 0, 0)
    else:
        c_idx = lambda i, f: (i // tiles_per_batch, 0, f)
        cnew_idx = lambda i, f: (i // tiles_per_batch, 0, 0, 0)
    y, cnew = pl.pallas_call(
        functools.partial(_ffn_kernel, nb=nb, tiles_per_batch=tiles_per_batch, tm=tm, conv_w=kw,
                          final_norm=final_norm),
        grid=(groups, dff // tf),
        in_specs=[pl.BlockSpec((1, nb, tm, d), lambda i, f: (i, 0, 0, 0)),
                  pl.BlockSpec((None, 1, d), lambda i, f: (layer, 0, 0)),
                  pl.BlockSpec((None, d, tf), lambda i, f: (layer, 0, f)),
                  pl.BlockSpec((None, d, tf), lambda i, f: (layer, 0, nf + f)),
                  pl.BlockSpec((None, tf, d), lambda i, f: (layer, f, 0)),
                  pl.BlockSpec((None, kw, tf), lambda i, f: (layer, 0, f)),
                  pl.BlockSpec((nb, kw - 1, tf), c_idx),
                  pl.BlockSpec((1, d), lambda i, f: (0, 0))],
        out_specs=[pl.BlockSpec((1, nb, tm, d), lambda i, f: (i, 0, 0, 0)),
                   pl.BlockSpec((nb, nf, kw - 1, tf), cnew_idx)],
        out_shape=[jax.ShapeDtypeStruct((groups, nb, tm, d), F32),
                   jax.ShapeDtypeStruct((b, nf, kw - 1, tf), F32)],
        scratch_shapes=[pltpu.VMEM((nb * tm, d), BF16), pltpu.VMEM((nb, SUBLANES + tm, tf), F32)],
        compiler_params=_params(2), name="ffn")(x4, g, w_up, w_up, w_down, cw, c0, fg)
    return y.reshape(b, l, d), jnp.swapaxes(cnew, 1, 2).reshape(b, kw - 1, dff)


def _pad_cols(a, width):
    return jnp.pad(a, ((0, 0), (0, width - a.shape[1])))


def _matmul_weights(w_in, w_out, ffn_w_up, ffn_w_down, sizes):
    gdn_w, gdn_h, fox_w, fox_h, lru_w = sizes
    offs = [0]
    for s in (gdn_w, gdn_w, gdn_w, gdn_w, gdn_h, gdn_h, fox_w, fox_w, fox_w, fox_h, lru_w, lru_w):
        offs.append(offs[-1] + s)
    cols = lambda i, j: w_in[:, :, offs[i]:offs[j]]
    pad = jnp.zeros(w_in.shape[:2] + (GATE_COLS - fox_h - 2 * gdn_h,), w_in.dtype)
    packed = jnp.concatenate([cols(0, 4), cols(10, 12), cols(6, 9), cols(9, 10), cols(4, 6), pad],
                             axis=2).astype(BF16)
    return dict(w_in=packed, w_out=w_out.astype(BF16), w_up=ffn_w_up.astype(BF16),
                w_down=ffn_w_down.astype(BF16))


def _run_group(x, cache, states, p, weights, dims, tiles):
    gdn_w, gdn_h, fox_w, fox_h, lru_w = dims
    b, l, d = x.shape
    depth = weights['w_in'].shape[0]
    n_main = 4 * gdn_w + 2 * lru_w
    gdn_s, gdn_conv, lru_h, lru_conv, ffn_conv = states
    beta_col, a_col = GATE_F + fox_h, GATE_F + fox_h + gdn_h
    outs = [[] for _ in range(6)]
    kv = None
    for li in range(depth):
        main, small, fk, fv, qkv = _inproj(
            x.reshape(b * l, d), p['norm_mix_g'][:, None, :], weights['w_in'], kv,
            n_main=n_main, n_fox=fox_w, tm=tiles['proj'], layer=li)
        kv = (fk, fv)
        main3 = main.reshape(b, l, n_main)
        small3 = small.reshape(b, l, GATE_COLS)
        qkv3 = qkv.reshape(b, l, 3 * fox_w)
        bias_row = _pad_cols(p['fox_f_bias'][li][None], GATE_COLS)
        if cache is None:
            logf, ft = _gate(small3, bias_row, heads=fox_h, cumulative=True)
            o_fox = _fox_prompt(qkv3, ft, p['fox_norm_g'][li], heads=fox_h, tq=tiles['attn'])
        else:
            logf, = _gate(small3, bias_row, heads=fox_h, cumulative=False)
            clt = jnp.swapaxes(cache[2][li], 1, 2)
            lnt = jnp.pad(jnp.swapaxes(logf, 1, 2), ((0, 0), (0, 0), (0, LANES - l)))
            o_fox = _fox_sample(qkv3, cache[0], cache[1], clt, lnt, p['fox_norm_g'][li],
                                heads=fox_h, layer=li, tp=min(1024, cache[0].shape[2]))
        row = lambda v, c0: jnp.pad(v[None], ((0, 0), (c0, GATE_COLS - c0 - v.shape[0])))
        o_gdn, s_new, gconv_new = _gdn(
            main3, small3, gdn_conv[li], gdn_s[li], p['gdn_conv_w'][li],
            row(p['gdn_a_log'][li], a_col), row(p['gdn_dt_bias'][li], a_col),
            p['gdn_norm_g'][li][None], rows=tiles['gdn'], c=min(l, CHUNK), heads=gdn_h,
            beta_col=beta_col, a_col=a_col)
        o_lru, h_new, lconv_new = _lru(
            main3, lru_conv[li], lru_h[li][:, None, :], p['lru_conv_w'][li], p['lru_conv_b'][li][None],
            p['lru_w_a'][li].astype(BF16), p['lru_w_x'][li].astype(BF16), p['lru_b_a'][li][None],
            p['lru_b_x'][li][None], p['lru_lambda'][li][None], p['lru_norm_g'][li][None],
            tl=tiles['lru'], x_block=4 * gdn_w // lru_w, gate_block=4 * gdn_w // lru_w + 1)
        x1 = _outproj(x.reshape(b * l, d), o_gdn.reshape(b * l, gdn_w), o_fox.reshape(b * l, fox_w),
                      o_lru.reshape(b * l, lru_w), weights['w_out'], tm=tiles['proj'], layer=li)
        x, fconv_new = _ffn(
            x1.reshape(b, l, d), p['norm_ffn_g'][:, None, :], weights['w_up'], weights['w_down'],
            p['ffn_conv_w'], ffn_conv[li], p['final_norm_g'][None],
            tm=tiles['ffn'], tf=512, final_norm=li == depth - 1, layer=li)
        new = [logf, s_new, gconv_new, h_new[:, 0, :], lconv_new, fconv_new]
        for lst, arr in zip(outs, new):
            lst.append(arr)
    kv_shape = (depth, b, l, fox_h, HEAD_DIM)
    return x, [kv[0].reshape(kv_shape), kv[1].reshape(kv_shape)] + [jnp.stack(lst) for lst in outs]


def kernel(x_prompt, x_sample, cache_fox_k, cache_fox_v, cache_fox_logf, state_gdn, state_gdn_conv,
           state_lru, state_lru_conv, state_ffn_conv, norm_mix_g, w_in, gdn_conv_w, gdn_a_log,
           gdn_dt_bias, gdn_norm_g, fox_f_bias, fox_norm_g, lru_conv_w, lru_conv_b, lru_w_a, lru_b_a,
           lru_w_x, lru_b_x, lru_lambda, lru_norm_g, w_out, norm_ffn_g, ffn_w_up, ffn_conv_w,
           ffn_w_down, final_norm_g):
    p = dict(norm_mix_g=norm_mix_g, gdn_conv_w=gdn_conv_w, gdn_a_log=gdn_a_log,
             gdn_dt_bias=gdn_dt_bias, gdn_norm_g=gdn_norm_g, fox_f_bias=fox_f_bias,
             fox_norm_g=fox_norm_g, lru_conv_w=lru_conv_w, lru_conv_b=lru_conv_b, lru_w_a=lru_w_a,
             lru_b_a=lru_b_a, lru_w_x=lru_w_x, lru_b_x=lru_b_x, lru_lambda=lru_lambda,
             lru_norm_g=lru_norm_g, norm_ffn_g=norm_ffn_g, ffn_conv_w=ffn_conv_w,
             final_norm_g=final_norm_g)
    depth = w_in.shape[0]
    bp, lp, _ = x_prompt.shape
    gdn_h = state_gdn.shape[2]
    fox_h = cache_fox_k.shape[3]
    lru_w = state_lru.shape[2]
    dims = (gdn_h * HEAD_DIM, gdn_h, fox_h * HEAD_DIM, fox_h, lru_w)
    dff = ffn_w_down.shape[1]
    weights = _matmul_weights(w_in, w_out, ffn_w_up, ffn_w_down, dims)

    zeros = lambda *s: jnp.zeros((depth, bp) + s, F32)
    prompt_states = (zeros(gdn_h, HEAD_DIM, HEAD_DIM), zeros(gdn_conv_w.shape[1] - 1, 3 * dims[0]),
                     zeros(lru_w), zeros(lru_conv_w.shape[1] - 1, lru_w),
                     zeros(ffn_conv_w.shape[1] - 1, dff))
    prompt_tiles = dict(proj=min(256, bp * lp), attn=min(512, lp), lru=min(256, lp), ffn=min(512, lp),
                        gdn=min(256, lp))
    y_prompt, sp = _run_group(x_prompt, None, prompt_states, p, weights, dims, prompt_tiles)

    bs, ls, _ = x_sample.shape
    sample_tiles = dict(proj=bs * ls, lru=ls, ffn=ls, gdn=ls)
    y_sample, ss = _run_group(
        x_sample, (cache_fox_k, cache_fox_v, cache_fox_logf),
        (state_gdn, state_gdn_conv, state_lru, state_lru_conv, state_ffn_conv), p, weights, dims,
        sample_tiles)
    return (y_prompt, y_sample, *sp, *ss)
```

```python
import functools
import math

import jax
import jax.numpy as jnp
from jax import lax
from jax.experimental import pallas as pl
from jax.experimental.pallas import tpu as pltpu

F32 = jnp.float32
BF16 = jnp.bfloat16

HEAD_DIM = 128
CHUNK = 64
LRU_C = 8.0
EPS = 1e-6
LANES = 128
SUBLANES = 8
VMEM_LIMIT = 56 * 1024 * 1024

GATE_COLS = LANES
GATE_F = 0
FFN_SPLIT = 2
GDN_SEQS = 2
NEG_INF = float("-inf")
LOG2E = math.log2(math.e)


def _params(n_axes):
    return pltpu.CompilerParams(dimension_semantics=("arbitrary",) * n_axes,
                                vmem_limit_bytes=VMEM_LIMIT)


def _resident(block_shape, index_map):
    return pl.BlockSpec(block_shape, index_map, pipeline_mode=pl.Buffered(1))


def _dot(a, b):
    return jnp.dot(a.astype(BF16), b.astype(BF16), preferred_element_type=F32)


def _dot_nt(a, b):
    return lax.dot_general(a.astype(BF16), b.astype(BF16), (((1,), (1,)), ((), ())),
                           preferred_element_type=F32)


def _dot_tn(a, b):
    return lax.dot_general(a.astype(BF16), b.astype(BF16), (((0,), (0,)), ((), ())),
                           preferred_element_type=F32)


def _dot_f32(a, b):
    return jnp.dot(a, b, preferred_element_type=F32, precision=lax.Precision.HIGHEST)


def _rms(x, g):
    return x * lax.rsqrt(jnp.mean(x * x, axis=-1, keepdims=True) + EPS) * g


def _expm1(x):
    u = jnp.exp(x)
    um1 = u - 1.0
    near = jnp.where(u == 1.0, x, um1 * x / jnp.log(u))
    return jnp.where(jnp.abs(x) > 0.5, um1, near)


def _cumsum_lanes(x, seg=None):
    n = x.shape[-1]
    lane = lax.broadcasted_iota(jnp.int32, x.shape, x.ndim - 1)
    pos = lane if seg is None else lane % seg
    s = 1
    while s < (n if seg is None else seg):
        x = x + jnp.where(pos >= s, pltpu.roll(x, s, x.ndim - 1), 0.0)
        s *= 2
    return x


def _cumsum_rows(x, seg):
    pos = lax.broadcasted_iota(jnp.int32, x.shape, 0) % seg
    s = 1
    while s < seg:
        x = x + jnp.where(pos >= s, pltpu.roll(x, s, 0), 0.0)
        s *= 2
    return x


def _causal_conv(buf, x, w_ref, width, rows, cols=slice(None)):
    lo = SUBLANES - (width - 1)
    buf[SUBLANES:SUBLANES + rows, cols] = x
    y = w_ref[width - 1:width, cols] * x
    for j in range(width - 1):
        y = y + w_ref[j:j + 1, cols] * buf[lo + j:lo + j + rows, cols]
    return y


def _inproj_kernel(*refs, n_fox, seg, layer):
    x_ref, g_ref, wa_ref, wb_ref, wf_ref, ws_ref = refs[:6]
    n_in = 8 if layer else 6
    main_ref, small_ref, fk_ref, fv_ref, qkv_ref, h_scr = refs[n_in:]
    if layer:
        fk_ref[:layer] = refs[6][...]
        fv_ref[:layer] = refs[7][...]
    h_scr[...] = _rms(x_ref[...], g_ref[...]).astype(BF16)

    def cols(w_ref, c0, width):
        return jnp.dot(h_scr[...], w_ref[:, c0:c0 + width], preferred_element_type=F32)

    n_a = wa_ref.shape[1]
    for c0 in range(0, n_a, seg):
        main_ref[:, c0:c0 + seg] = cols(wa_ref, c0, seg)
    for c0 in range(0, wb_ref.shape[1], seg):
        main_ref[:, n_a + c0:n_a + c0 + seg] = cols(wb_ref, c0, seg)
    for j in range(3):
        for c in range(0, n_fox, seg):
            p = cols(wf_ref, j * n_fox + c, seg)
            qkv_ref[:, j * n_fox + c:j * n_fox + c + seg] = p.astype(BF16)
            if j == 1:
                fk_ref[layer, :, c:c + seg] = p
            if j == 2:
                fv_ref[layer, :, c:c + seg] = p
    small_ref[...] = cols(ws_ref, 0, GATE_COLS)


def _inproj(x2d, g, w_slabs, kv_prev, *, n_main, n_fox, tm, layer):
    m, d = x2d.shape
    row = lambda i: (i, 0)
    kv_row = lambda i: (0, i, 0)
    pick = lambda i: (layer, 0, 0)
    in_specs = [pl.BlockSpec((tm, d), row), _resident((None, 1, d), pick)]
    in_specs += [_resident((None, d, w.shape[2]), pick) for w in w_slabs]
    operands = [x2d, g, *w_slabs]
    if layer:
        in_specs += [pl.BlockSpec((layer, tm, n_fox), kv_row)] * 2
        operands += list(kv_prev)
    kv_out = jax.ShapeDtypeStruct((layer + 1, m, n_fox), F32)
    return pl.pallas_call(
        functools.partial(_inproj_kernel, n_fox=n_fox, seg=512, layer=layer),
        grid=(m // tm,),
        in_specs=in_specs,
        out_specs=[pl.BlockSpec((tm, n_main), row), pl.BlockSpec((tm, GATE_COLS), row),
                   pl.BlockSpec((layer + 1, tm, n_fox), kv_row),
                   pl.BlockSpec((layer + 1, tm, n_fox), kv_row),
                   pl.BlockSpec((tm, 3 * n_fox), row)],
        out_shape=[jax.ShapeDtypeStruct((m, n_main), F32), jax.ShapeDtypeStruct((m, GATE_COLS), F32),
                   kv_out, kv_out, jax.ShapeDtypeStruct((m, 3 * n_fox), BF16)],
        scratch_shapes=[pltpu.VMEM((tm, d), BF16)],
        compiler_params=_params(1), name="inproj")(*operands)


def _gate_kernel(sm_ref, b_ref, logf_ref, *rest, heads, cumulative):
    ls = jax.nn.log_sigmoid(sm_ref[0] + b_ref[...])
    logf_ref[0] = ls[:, GATE_F:GATE_F + heads]
    if cumulative:
        ft_ref, = rest
        ft_ref[0] = _cumsum_lanes(ls.T[GATE_F:GATE_F + heads, :])


def _gate(small3d, bias_row, *, heads, cumulative):
    b, rows, _ = small3d.shape
    out_specs = [pl.BlockSpec((1, rows, heads), lambda i: (i, 0, 0))]
    out_shape = [jax.ShapeDtypeStruct((b, rows, heads), F32)]
    if cumulative:
        out_specs.append(pl.BlockSpec((1, heads, rows), lambda i: (i, 0, 0)))
        out_shape.append(jax.ShapeDtypeStruct((b, heads, rows), F32))
    return pl.pallas_call(
        functools.partial(_gate_kernel, heads=heads, cumulative=cumulative),
        grid=(b,),
        in_specs=[pl.BlockSpec((1, rows, GATE_COLS), lambda i: (i, 0, 0)),
                  pl.BlockSpec((1, GATE_COLS), lambda i: (0, 0))],
        out_specs=out_specs, out_shape=out_shape,
        compiler_params=_params(1), name="fox_gate")(small3d, bias_row)


def _split_bf16(x):
    hi = x.astype(BF16)
    return hi, (x - hi.astype(F32)).astype(BF16)


def _dot_split(a, b):
    a_hi, a_lo = _split_bf16(a)
    b_hi, b_lo = _split_bf16(b)
    mm = lambda x, y: jnp.dot(x, y, preferred_element_type=F32)
    return mm(a_hi, b_hi) + (mm(a_lo, b_hi) + mm(a_hi, b_lo))


def _unit_lower_solve(a, rhs, c):
    r = a[0].shape[0]
    row = lax.broadcasted_iota(jnp.int32, (r, r), 0)
    col = lax.broadcasted_iota(jnp.int32, (r, r), 1)
    eye = jnp.where(row == col, 1.0, 0.0)
    p = [-x for x in a]
    t = [eye + x for x in p]
    for _ in range(int(math.log2(c)) - 1):
        p = [_dot(x, x) for x in p]
        t = [ti + _dot(ti, pi) for ti, pi in zip(t, p)]
    x = [_dot(ti, bi) for ti, bi in zip(t, rhs)]
    resid = [bi - xi - _dot_split(ai, xi) for ai, bi, xi in zip(a, rhs, x)]
    return [xi + _dot(ti, ri) for ti, xi, ri in zip(t, x, resid)]


def _gdn_kernel(qkv_ref, z_ref, sm_ref, conv0_ref, s0_ref, cw_ref, alog_ref, dt_ref, ng_ref,
                o_ref, s_ref, conv_ref, buf, *, nbat, rows, c, heads, conv_w, beta_col, a_col):
    ti = pl.program_id(1)
    nc = rows // c
    width = heads * HEAD_DIM
    hist = conv_w - 1
    lo = SUBLANES - hist

    @pl.when(ti == 0)
    def _():
        buf[:, lo:SUBLANES, :] = conv0_ref[...]
        s_ref[...] = s0_ref[...]

    row = lax.broadcasted_iota(jnp.int32, (rows, rows), 0)
    col = lax.broadcasted_iota(jnp.int32, (rows, rows), 1)
    lower = row >= col
    if nc > 1:
        lower = lower & (row // c == col // c)
    strict = lower & (row != col)
    pad = max(LANES - rows, 0)

    ys, beta, gcum, g_end, gcum_t = [], [], [], [], []
    for bi in range(nbat):
        y = _causal_conv(buf.at[bi], qkv_ref[bi], cw_ref, conv_w, rows)
        new_hist = buf[bi, lo + rows:SUBLANES + rows, :]
        conv_ref[bi] = new_hist
        buf[bi, lo:SUBLANES, :] = new_hist
        ys.append(jax.nn.silu(y))
        sm = sm_ref[bi]
        beta.append(jax.nn.sigmoid(sm))
        g = -jnp.exp(alog_ref[...]) * jax.nn.softplus(sm + dt_ref[...])
        gc_b = _cumsum_rows(g, c)
        gcum.append(gc_b)
        g_end.append(jnp.concatenate(
            [jnp.broadcast_to(gc_b[(i + 1) * c - 1:(i + 1) * c, :], (c, GATE_COLS))
             for i in range(nc)], axis=0))
        gcum_t.append(jnp.concatenate([gc_b, jnp.zeros((pad, GATE_COLS), F32)], axis=0).T
                      if pad else gc_b.T)

    chains = [(bi, h) for bi in range(nbat) for h in range(heads)]
    ids = range(len(chains))
    head = lambda bi, base, h: ys[bi][:, base + h * HEAD_DIM:base + (h + 1) * HEAD_DIM]
    l2 = lambda t: t * lax.rsqrt(jnp.sum(t * t, axis=-1, keepdims=True) + EPS)
    q = [l2(head(bi, 0, h)) * HEAD_DIM ** -0.5 for bi, h in chains]
    k = [l2(head(bi, width, h)) for bi, h in chains]
    v = [head(bi, 2 * width, h) for bi, h in chains]
    bh = [beta[bi][:, beta_col + h:beta_col + h + 1] for bi, h in chains]
    gc = [gcum[bi][:, a_col + h:a_col + h + 1] for bi, h in chains]
    ge = [g_end[bi][:, a_col + h:a_col + h + 1] for bi, h in chains]
    gr = [gcum_t[bi][a_col + h:a_col + h + 1, :rows] for bi, h in chains]
    decay = [jnp.exp(jnp.where(lower, gc[n] - gr[n], NEG_INF)) for n in ids]
    kb = [k[n] * bh[n] for n in ids]
    eg = [jnp.exp(gc[n]) for n in ids]
    a = [_dot_nt(kb[n], k[n]) * jnp.where(strict, decay[n], 0.0) for n in ids]
    attn = [_dot_nt(q[n], k[n]) * decay[n] for n in ids]
    sol = _unit_lower_solve(a, [jnp.concatenate([v[n] * bh[n], kb[n] * eg[n]], axis=-1) for n in ids], c)
    qe = [q[n] * eg[n] for n in ids]
    kx = [k[n] * jnp.exp(ge[n] - gc[n]) for n in ids]
    s = [s_ref[bi, h] for bi, h in chains]
    deltas = [[] for _ in ids]
    inter = [[] for _ in ids]
    for i in range(nc):
        rs = slice(i * c, (i + 1) * c)
        d_i = [sol[n][rs, :HEAD_DIM] - _dot(sol[n][rs, HEAD_DIM:], s[n]) for n in ids]
        for n in ids:
            inter[n].append(_dot(qe[n][rs], s[n]))
            deltas[n].append(d_i[n])
        s = [jnp.exp(ge[n][(i + 1) * c - 1:(i + 1) * c, :]) * s[n] + _dot_tn(kx[n][rs], d_i[n])
             for n in ids]
    cat = lambda parts: parts[0] if nc == 1 else jnp.concatenate(parts, axis=0)
    for n, (bi, h) in enumerate(chains):
        s_ref[bi, h] = s[n]
        o = cat(inter[n]) + _dot(attn[n], cat(deltas[n]))
        zh = z_ref[bi, :, h * HEAD_DIM:(h + 1) * HEAD_DIM]
        o_ref[bi, :, h * HEAD_DIM:(h + 1) * HEAD_DIM] = (
            _rms(o, ng_ref[...]) * jax.nn.silu(zh)).astype(o_ref.dtype)


def _gdn(main3d, small3d, conv0, s0, conv_w, alog_row, dt_row, norm_g, *, nbat, rows, c, heads,
         beta_col, a_col):
    b, l, _ = main3d.shape
    width = heads * HEAD_DIM
    kw = conv_w.shape[0]
    fixed2 = lambda i, j: (0, 0)
    return pl.pallas_call(
        functools.partial(_gdn_kernel, nbat=nbat, rows=rows, c=c, heads=heads, conv_w=kw,
                          beta_col=beta_col, a_col=a_col),
        grid=(b // nbat, l // rows),
        in_specs=[pl.BlockSpec((nbat, rows, 3 * width), lambda i, j: (i, j, 0)),
                  pl.BlockSpec((nbat, rows, width), lambda i, j: (i, j, 3)),
                  pl.BlockSpec((nbat, rows, GATE_COLS), lambda i, j: (i, j, 0)),
                  pl.BlockSpec((nbat, kw - 1, 3 * width), lambda i, j: (i, 0, 0)),
                  pl.BlockSpec((nbat, heads, HEAD_DIM, HEAD_DIM), lambda i, j: (i, 0, 0, 0)),
                  pl.BlockSpec((kw, 3 * width), fixed2),
                  pl.BlockSpec((1, GATE_COLS), fixed2), pl.BlockSpec((1, GATE_COLS), fixed2),
                  pl.BlockSpec((1, HEAD_DIM), fixed2)],
        out_specs=[pl.BlockSpec((nbat, rows, width), lambda i, j: (i, j, 0)),
                   pl.BlockSpec((nbat, heads, HEAD_DIM, HEAD_DIM), lambda i, j: (i, 0, 0, 0)),
                   pl.BlockSpec((nbat, kw - 1, 3 * width), lambda i, j: (i, 0, 0))],
        out_shape=[jax.ShapeDtypeStruct((b, l, width), BF16),
                   jax.ShapeDtypeStruct((b, heads, HEAD_DIM, HEAD_DIM), F32),
                   jax.ShapeDtypeStruct((b, kw - 1, 3 * width), F32)],
        scratch_shapes=[pltpu.VMEM((nbat, SUBLANES + rows, 3 * width), F32)],
        compiler_params=_params(2), name="gdn")(
            main3d, main3d, small3d, conv0, s0, conv_w, alog_row, dt_row, norm_g)


def _fox_prompt_kernel(q_ref, k_ref, v_ref, ft_ref, g_ref, o_ref, vaug, m_scr, acc_scr, s_a, s_b,
                       *, tq):
    nq = q_ref.shape[1] // tq
    vaug[:, :HEAD_DIM] = v_ref[0]
    vaug[:, HEAD_DIM:] = jnp.ones((vaug.shape[0], HEAD_DIM), vaug.dtype)
    scale = HEAD_DIM ** -0.5

    def block_start(j):
        return j * tq if isinstance(j, int) else pl.multiple_of(j * tq, tq)

    def scores(buf, qi, j):
        buf[...] = _dot_nt(q_ref[0, qi * tq:(qi + 1) * tq, :], k_ref[0, pl.ds(block_start(j), tq), :])

    def accumulate(buf, j, masked):
        start = block_start(j)
        s = buf[...] * (scale * LOG2E) - ft_ref[0, 0, pl.ds(j, 1), :] * LOG2E
        if masked:
            row = lax.broadcasted_iota(jnp.int32, (tq, tq), 0)
            col = lax.broadcasted_iota(jnp.int32, (tq, tq), 1)
            s = jnp.where(col <= row, s, NEG_INF)
        m_prev = m_scr[...]
        m_new = jnp.maximum(m_prev, jnp.max(s, axis=-1, keepdims=True))
        alpha = jnp.exp2(m_prev - m_new)
        p = jnp.exp2(s - jnp.concatenate([m_new] * (tq // LANES), axis=1))
        pv = _dot(p, vaug[pl.ds(start, tq), :])
        acc_scr[...] = jnp.concatenate([alpha] * (2 * HEAD_DIM // LANES), axis=1) * acc_scr[...] + pv
        m_scr[...] = m_new

    cur, other = s_a, s_b
    scores(cur, 0, 0)
    for qi in range(nq):
        m_scr[...] = jnp.full(m_scr.shape, NEG_INF, F32)
        acc_scr[...] = jnp.zeros(acc_scr.shape, F32)

        def pair(jj, carry, qi=qi, cur=cur, other=other):
            j = 2 * jj
            scores(other, qi, j + 1)
            accumulate(cur, j, False)
            scores(cur, qi, j + 2)
            accumulate(other, j + 1, False)
            return carry

        if qi >= 2:
            lax.fori_loop(0, qi // 2, pair, 0)
        if qi % 2 == 0:
            if qi + 1 < nq:
                scores(other, qi + 1, 0)
            accumulate(cur, qi, True)
            cur, other = other, cur
        else:
            scores(other, qi, qi)
            accumulate(cur, qi - 1, False)
            if qi + 1 < nq:
                scores(cur, qi + 1, 0)
            accumulate(other, qi, True)
        o = acc_scr[:, :HEAD_DIM] / acc_scr[:, HEAD_DIM:]
        o_ref[0, qi * tq:(qi + 1) * tq, :] = _rms(o, g_ref[0]).astype(o_ref.dtype)


def _fox_prompt(qkv3d, ft, norm_g, *, heads, tq):
    b, l, _ = qkv3d.shape
    ft4 = ft.reshape(b, heads, l // tq, tq)
    g3 = norm_g.reshape(heads, 1, HEAD_DIM)
    head_block = lambda part: pl.BlockSpec((1, l, HEAD_DIM), lambda i, h: (i, 0, part * heads + h))
    return pl.pallas_call(
        functools.partial(_fox_prompt_kernel, tq=tq),
        grid=(b, heads),
        in_specs=[head_block(0), head_block(1), head_block(2),
                  pl.BlockSpec((1, 1, l // tq, tq), lambda i, h: (i, h, 0, 0)),
                  pl.BlockSpec((1, 1, HEAD_DIM), lambda i, h: (h, 0, 0))],
        out_specs=pl.BlockSpec((1, l, HEAD_DIM), lambda i, h: (i, 0, h)),
        out_shape=jax.ShapeDtypeStruct((b, l, heads * HEAD_DIM), BF16),
        scratch_shapes=[pltpu.VMEM((l, 2 * HEAD_DIM), BF16), pltpu.VMEM((tq, LANES), F32),
                        pltpu.VMEM((tq, 2 * HEAD_DIM), F32), pltpu.VMEM((tq, tq), F32),
                        pltpu.VMEM((tq, tq), F32)],
        compiler_params=_params(2), name="fox_prompt")(qkv3d, qkv3d, qkv3d, ft4, g3)


def _fox_sample_kernel(q_ref, kn_ref, vn_ref, ck_ref, cv_ref, clt_ref, lnt_ref, g_ref, o_ref,
                       fc_scr, m_scr, l_scr, acc_scr, *, ls, heads, tp):
    j = pl.program_id(1)
    last = pl.num_programs(1) - 1
    scale = HEAD_DIM ** -0.5

    @pl.when(j == 0)
    def _():
        fc = _cumsum_lanes(clt_ref[0])
        for c in range(fc_scr.shape[0]):
            fc_scr[c] = fc[:, c * tp:(c + 1) * tp]
        m_scr[...] = jnp.full(m_scr.shape, NEG_INF, F32)
        l_scr[...] = jnp.zeros(l_scr.shape, F32)
        acc_scr[...] = jnp.zeros(acc_scr.shape, F32)

    def update(h, s, v):
        m_prev = m_scr[h]
        m_new = jnp.maximum(m_prev, jnp.max(s, axis=-1, keepdims=True))
        alpha = jnp.exp(m_prev - m_new)
        p = jnp.exp(s - m_new)
        l_scr[h] = alpha * l_scr[h] + jnp.sum(p, axis=-1, keepdims=True)
        acc_scr[h] = alpha * acc_scr[h] + _dot(p, v)
        m_scr[h] = m_new

    fcj = fc_scr[j]
    for h in range(heads):
        cs = slice(h * HEAD_DIM, (h + 1) * HEAD_DIM)
        k_h = ck_ref[0, pl.ds(h, tp, stride=heads), :]
        v_h = cv_ref[0, pl.ds(h, tp, stride=heads), :]
        update(h, _dot_nt(q_ref[0, :, cs], k_h) * scale - fcj[h:h + 1, :], v_h)

    @pl.when(j == last)
    def _():
        fn = fcj[:, tp - 1:tp] + _cumsum_lanes(lnt_ref[0])[:, :ls]
        row = lax.broadcasted_iota(jnp.int32, (ls, ls), 0)
        col = lax.broadcasted_iota(jnp.int32, (ls, ls), 1)
        for h in range(heads):
            cs = slice(h * HEAD_DIM, (h + 1) * HEAD_DIM)
            s_n = _dot_nt(q_ref[0, :, cs], kn_ref[0, :, cs]) * scale - fn[h:h + 1, :]
            update(h, jnp.where(col <= row, s_n, NEG_INF), vn_ref[0, :, cs])
            o = acc_scr[h] / l_scr[h]
            o_ref[0, :, cs] = _rms(o, g_ref[h:h + 1, :]).astype(o_ref.dtype)


def _fox_sample(qkv3d, cache_k, cache_v, cache_logf_t, logf_new_t, norm_g, *, heads, layer, tp):
    b, ls, _ = qkv3d.shape
    depth, _, p_len = cache_k.shape[:3]
    width = heads * HEAD_DIM
    ck = cache_k.reshape(depth * b, p_len * heads, HEAD_DIM)
    cv = cache_v.reshape(depth * b, p_len * heads, HEAD_DIM)
    cache_idx = lambda i, j: (layer * b + i, j, 0)
    return pl.pallas_call(
        functools.partial(_fox_sample_kernel, ls=ls, heads=heads, tp=tp),
        grid=(b, p_len // tp),
        in_specs=[pl.BlockSpec((1, ls, width), lambda i, j: (i, 0, 0)),
                  pl.BlockSpec((1, ls, width), lambda i, j: (i, 0, 1)),
                  pl.BlockSpec((1, ls, width), lambda i, j: (i, 0, 2)),
                  pl.BlockSpec((1, tp * heads, HEAD_DIM), cache_idx),
                  pl.BlockSpec((1, tp * heads, HEAD_DIM), cache_idx),
                  pl.BlockSpec((1, heads, p_len), lambda i, j: (i, 0, 0)),
                  pl.BlockSpec((1, heads, LANES), lambda i, j: (i, 0, 0)),
                  pl.BlockSpec((heads, HEAD_DIM), lambda i, j: (0, 0))],
        out_specs=pl.BlockSpec((1, ls, width), lambda i, j: (i, 0, 0)),
        out_shape=jax.ShapeDtypeStruct((b, ls, width), BF16),
        scratch_shapes=[pltpu.VMEM((p_len // tp, heads, tp), F32), pltpu.VMEM((heads, ls, 1), F32),
                        pltpu.VMEM((heads, ls, 1), F32), pltpu.VMEM((heads, ls, HEAD_DIM), F32)],
        compiler_params=_params(2), name="fox_sample")(
            qkv3d, qkv3d, qkv3d, ck, cv, cache_logf_t, logf_new_t, norm_g)


def _lru_kernel(x_ref, gate_ref, conv0_ref, h0_ref, cw_ref, cb_ref, wa_ref, wx_ref, ba_ref, bx_ref,
                lam_ref, ng_ref, o_ref, h_ref, conv_ref, buf, a_scr, hs_scr, *, tl, blocks, conv_w):
    ti = pl.program_id(1)
    hist = conv_w - 1
    lo = SUBLANES - hist
    bw = x_ref.shape[2] // blocks

    @pl.when(ti == 0)
    def _():
        buf[lo:SUBLANES, :] = conv0_ref[0]
        h_ref[0] = h0_ref[0]

    xc = _causal_conv(buf, x_ref[0], cw_ref, conv_w, tl) + cb_ref[...]
    new_hist = buf[lo + tl:SUBLANES + tl, :]
    conv_ref[0] = new_hist
    buf[lo:SUBLANES, :] = new_hist

    for n in range(blocks):
        sl = slice(n * bw, (n + 1) * bw)
        xb = xc[:, sl]
        r = jax.nn.sigmoid(_dot(xb, wa_ref[n]) + ba_ref[:, sl])
        i = jax.nn.sigmoid(_dot(xb, wx_ref[n]) + bx_ref[:, sl])
        log_a = -LRU_C * r * jax.nn.softplus(-lam_ref[:, sl])
        a_scr[:, sl] = jnp.exp(log_a)
        hs_scr[:, sl] = jnp.sqrt(-_expm1(2.0 * log_a)) * (i * xb)

    row = lax.broadcasted_iota(jnp.int32, (SUBLANES, a_scr.shape[1]), 0)

    def group(gi, h):
        start = pl.multiple_of(gi * SUBLANES, SUBLANES)
        a = a_scr[pl.ds(start, SUBLANES), :]
        u = hs_scr[pl.ds(start, SUBLANES), :]
        s = 1
        while s < SUBLANES:
            keep = row >= s
            u = jnp.where(keep, u + a * pltpu.roll(u, s, 0), u)
            a = jnp.where(keep, a * pltpu.roll(a, s, 0), a)
            s *= 2
        hg = u + a * h
        hs_scr[pl.ds(start, SUBLANES), :] = hg
        return hg[SUBLANES - 1:SUBLANES, :]

    h_ref[0] = lax.fori_loop(0, tl // SUBLANES, group, h_ref[0], unroll=4 if tl >= 32 else True)

    gate = gate_ref[0]
    for n in range(blocks):
        sl = slice(n * bw, (n + 1) * bw)
        o_ref[0, :, sl] = (_rms(hs_scr[:, sl], ng_ref[:, sl])
                           * jax.nn.gelu(gate[:, sl])).astype(o_ref.dtype)


def _lru(main3d, conv0, h0, cw, cb, wa, wx, ba, bx, lam, ng, *, tl, x_block, gate_block):
    b, l, _ = main3d.shape
    blocks, bw, _ = wa.shape
    width = blocks * bw
    kw = cw.shape[0]
    fixed2 = lambda i, j: (0, 0)
    fixed3 = lambda i, j: (0, 0, 0)
    vec = pl.BlockSpec((1, width), fixed2)
    return pl.pallas_call(
        functools.partial(_lru_kernel, tl=tl, blocks=blocks, conv_w=kw),
        grid=(b, l // tl),
        in_specs=[pl.BlockSpec((1, tl, width), lambda i, j: (i, j, x_block)),
                  pl.BlockSpec((1, tl, width), lambda i, j: (i, j, gate_block)),
                  pl.BlockSpec((1, kw - 1, width), lambda i, j: (i, 0, 0)),
                  pl.BlockSpec((1, 1, width), lambda i, j: (i, 0, 0)),
                  pl.BlockSpec((kw, width), fixed2), vec,
                  pl.BlockSpec((blocks, bw, bw), fixed3), pl.BlockSpec((blocks, bw, bw), fixed3),
                  vec, vec, vec, vec],
        out_specs=[pl.BlockSpec((1, tl, width), lambda i, j: (i, j, 0)),
                   pl.BlockSpec((1, 1, width), lambda i, j: (i, 0, 0)),
                   pl.BlockSpec((1, kw - 1, width), lambda i, j: (i, 0, 0))],
        out_shape=[jax.ShapeDtypeStruct((b, l, width), BF16),
                   jax.ShapeDtypeStruct((b, 1, width), F32),
                   jax.ShapeDtypeStruct((b, kw - 1, width), F32)],
        scratch_shapes=[pltpu.VMEM((SUBLANES + tl, width), F32), pltpu.VMEM((tl, width), F32),
                        pltpu.VMEM((tl, width), F32)],
        compiler_params=_params(2), name="rglru")(
            main3d, main3d, conv0, h0, cw, cb, wa, wx, ba, bx, lam, ng)


def _outproj_kernel(x_ref, og_ref, of_ref, ol_ref, w_ref, y_ref):
    wg, wf = og_ref.shape[1], of_ref.shape[1]
    y = x_ref[...] + jnp.dot(og_ref[...], w_ref[0:wg, :], preferred_element_type=F32)
    y = y + jnp.dot(of_ref[...], w_ref[wg:wg + wf, :], preferred_element_type=F32)
    y_ref[...] = y + jnp.dot(ol_ref[...], w_ref[wg + wf:, :], preferred_element_type=F32)


def _outproj(x2d, o_gdn, o_fox, o_lru, w, *, tm, layer):
    m, d = x2d.shape
    row = lambda i: (i, 0)
    return pl.pallas_call(
        _outproj_kernel, grid=(m // tm,),
        in_specs=[pl.BlockSpec((tm, d), row), pl.BlockSpec((tm, o_gdn.shape[1]), row),
                  pl.BlockSpec((tm, o_fox.shape[1]), row), pl.BlockSpec((tm, o_lru.shape[1]), row),
                  _resident((None,) + w.shape[1:], lambda i: (layer, 0, 0))],
        out_specs=pl.BlockSpec((tm, d), row),
        out_shape=jax.ShapeDtypeStruct((m, d), F32),
        compiler_params=_params(1), name="outproj")(x2d, o_gdn, o_fox, o_lru, w)


def _ffn_kernel(x_ref, g_ref, wg_ref, wv_ref, wd_ref, cw_ref, c0_ref, fg_ref, y_ref, cnew_ref,
                h_scr, gp_scr, *, nb, tiles_per_batch, tm, conv_w, final_norm):
    gi = pl.program_id(0)
    f = pl.program_id(1)
    hist = conv_w - 1
    lo = SUBLANES - hist

    rows = lambda j: slice(j * tm, (j + 1) * tm)

    @pl.when(f == 0)
    def _():
        for j in range(nb):
            x = x_ref[0, j]
            h_scr[rows(j), :] = _rms(x, g_ref[...]).astype(BF16)
            y_ref[0, j] = x

    if tiles_per_batch > 1:
        first = gi % tiles_per_batch == 0

        @pl.when(first)
        def _():
            gp_scr[0, lo:SUBLANES, :] = c0_ref[0]

        @pl.when(jnp.logical_not(first))
        def _():
            gp_scr[0, lo:SUBLANES, :] = cnew_ref[0, f]
    else:
        for j in range(nb):
            gp_scr[j, lo:SUBLANES, :] = c0_ref[j]

    tf = wg_ref.shape[1]
    half = tf // FFN_SPLIT
    h = h_scr[...]
    ups = []
    for c in range(FFN_SPLIT):
        cs = slice(c * half, (c + 1) * half)
        ups.append((jnp.dot(h, wg_ref[:, cs], preferred_element_type=F32),
                    jnp.dot(h, wv_ref[:, cs], preferred_element_type=F32)))
    contrib = None
    for c in range(FFN_SPLIT):
        cs = slice(c * half, (c + 1) * half)
        gp, val = ups[c]
        acts = [jax.nn.silu(_causal_conv(gp_scr.at[j], gp[rows(j)], cw_ref, conv_w, tm, cs))
                * val[rows(j)] for j in range(nb)]
        act = (acts[0] if nb == 1 else jnp.concatenate(acts, axis=0)).astype(BF16)
        part = jnp.dot(act, wd_ref[cs, :], preferred_element_type=F32)
        contrib = part if contrib is None else contrib + part
    for j in range(nb):
        cnew_ref[j, f] = gp_scr[j, lo + tm:SUBLANES + tm, :]
        y_ref[0, j] += contrib[rows(j)]

    if final_norm:
        @pl.when(f == pl.num_programs(1) - 1)
        def _():
            for j in range(nb):
                y_ref[0, j] = _rms(y_ref[0, j], fg_ref[...])


def _ffn(x3d, g, w_up, w_down, cw, c0, fg, *, tm, tf, final_norm, layer):
    b, l, d = x3d.shape
    dff = w_down.shape[1]
    kw = cw.shape[1]
    tiles_per_batch = l // tm
    nb = b if tiles_per_batch == 1 else 1
    groups = b * l // (nb * tm)
    x4 = x3d.reshape(groups, nb, tm, d)
    nf = dff // tf
    if tiles_per_batch == 1:
        c_idx = lambda i, f: (0, 0, f)
        cnew_idx = lambda i, f: (0, 0, 0, 0)
    else:
        c_idx = lambda i, f: (i // tiles_per_batch, 0, f)
        cnew_idx = lambda i, f: (i // tiles_per_batch, 0, 0, 0)
    y, cnew = pl.pallas_call(
        functools.partial(_ffn_kernel, nb=nb, tiles_per_batch=tiles_per_batch, tm=tm, conv_w=kw,
                          final_norm=final_norm),
        grid=(groups, dff // tf),
        in_specs=[pl.BlockSpec((1, nb, tm, d), lambda i, f: (i, 0, 0, 0)),
                  pl.BlockSpec((None, 1, d), lambda i, f: (layer, 0, 0)),
                  pl.BlockSpec((None, d, tf), lambda i, f: (layer, 0, f)),
                  pl.BlockSpec((None, d, tf), lambda i, f: (layer, 0, nf + f)),
                  pl.BlockSpec((None, tf, d), lambda i, f: (layer, f, 0)),
                  pl.BlockSpec((None, kw, tf), lambda i, f: (layer, 0, f)),
                  pl.BlockSpec((nb, kw - 1, tf), c_idx),
                  pl.BlockSpec((1, d), lambda i, f: (0, 0))],
        out_specs=[pl.BlockSpec((1, nb, tm, d), lambda i, f: (i, 0, 0, 0)),
                   pl.BlockSpec((nb, nf, kw - 1, tf), cnew_idx)],
        out_shape=[jax.ShapeDtypeStruct((groups, nb, tm, d), F32),
                   jax.ShapeDtypeStruct((b, nf, kw - 1, tf), F32)],
        scratch_shapes=[pltpu.VMEM((nb * tm, d), BF16), pltpu.VMEM((nb, SUBLANES + tm, tf), F32)],
        compiler_params=_params(2), name="ffn")(x4, g, w_up, w_up, w_down, cw, c0, fg)
    return y.reshape(b, l, d), jnp.swapaxes(cnew, 1, 2).reshape(b, kw - 1, dff)


def _pad_cols(a, width):
    return jnp.pad(a, ((0, 0), (0, width - a.shape[1])))


def _matmul_weights(w_in, w_out, ffn_w_up, ffn_w_down, sizes):
    gdn_w, gdn_h, fox_w, fox_h, lru_w = sizes
    offs = [0]
    for s in (gdn_w, gdn_w, gdn_w, gdn_w, gdn_h, gdn_h, fox_w, fox_w, fox_w, fox_h, lru_w, lru_w):
        offs.append(offs[-1] + s)
    cols = lambda i, j: w_in[:, :, offs[i]:offs[j]]
    pad = jnp.zeros(w_in.shape[:2] + (GATE_COLS - fox_h - 2 * gdn_h,), w_in.dtype)
    slabs = (cols(0, 4), cols(10, 12), cols(6, 9), jnp.concatenate([cols(9, 10), cols(4, 6), pad], axis=2))
    return dict(w_in=tuple(s.astype(BF16) for s in slabs), w_out=w_out.astype(BF16),
                w_up=ffn_w_up.astype(BF16), w_down=ffn_w_down.astype(BF16))


def _run_group(x, cache, states, p, weights, dims, tiles):
    gdn_w, gdn_h, fox_w, fox_h, lru_w = dims
    b, l, d = x.shape
    depth = weights['w_out'].shape[0]
    n_main = 4 * gdn_w + 2 * lru_w
    gdn_s, gdn_conv, lru_h, lru_conv, ffn_conv = states
    beta_col, a_col = GATE_F + fox_h, GATE_F + fox_h + gdn_h
    outs = [[] for _ in range(6)]
    kv = None
    for li in range(depth):
        main, small, fk, fv, qkv = _inproj(
            x.reshape(b * l, d), p['norm_mix_g'][:, None, :], weights['w_in'], kv,
            n_main=n_main, n_fox=fox_w, tm=tiles['proj'], layer=li)
        kv = (fk, fv)
        main3 = main.reshape(b, l, n_main)
        small3 = small.reshape(b, l, GATE_COLS)
        qkv3 = qkv.reshape(b, l, 3 * fox_w)
        bias_row = _pad_cols(p['fox_f_bias'][li][None], GATE_COLS)
        if cache is None:
            logf, ft = _gate(small3, bias_row, heads=fox_h, cumulative=True)
            o_fox = _fox_prompt(qkv3, ft, p['fox_norm_g'][li], heads=fox_h, tq=tiles['attn'])
        else:
            logf, = _gate(small3, bias_row, heads=fox_h, cumulative=False)
            clt = jnp.swapaxes(cache[2][li], 1, 2)
            lnt = jnp.pad(jnp.swapaxes(logf, 1, 2), ((0, 0), (0, 0), (0, LANES - l)))
            o_fox = _fox_sample(qkv3, cache[0], cache[1], clt, lnt, p['fox_norm_g'][li],
                                heads=fox_h, layer=li, tp=min(1024, cache[0].shape[2]))
        row = lambda v, c0: jnp.pad(v[None], ((0, 0), (c0, GATE_COLS - c0 - v.shape[0])))
        o_gdn, s_new, gconv_new = _gdn(
            main3, small3, gdn_conv[li], gdn_s[li], p['gdn_conv_w'][li],
            row(p['gdn_a_log'][li], a_col), row(p['gdn_dt_bias'][li], a_col),
            p['gdn_norm_g'][li][None], nbat=GDN_SEQS, rows=tiles['gdn'], c=min(l, CHUNK), heads=gdn_h,
            beta_col=beta_col, a_col=a_col)
        o_lru, h_new, lconv_new = _lru(
            main3, lru_conv[li], lru_h[li][:, None, :], p['lru_conv_w'][li], p['lru_conv_b'][li][None],
            p['lru_w_a'][li].astype(BF16), p['lru_w_x'][li].astype(BF16), p['lru_b_a'][li][None],
            p['lru_b_x'][li][None], p['lru_lambda'][li][None], p['lru_norm_g'][li][None],
            tl=tiles['lru'], x_block=4 * gdn_w // lru_w, gate_block=4 * gdn_w // lru_w + 1)
        x1 = _outproj(x.reshape(b * l, d), o_gdn.reshape(b * l, gdn_w), o_fox.reshape(b * l, fox_w),
                      o_lru.reshape(b * l, lru_w), weights['w_out'], tm=tiles['proj'], layer=li)
        x, fconv_new = _ffn(
            x1.reshape(b, l, d), p['norm_ffn_g'][:, None, :], weights['w_up'], weights['w_down'],
            p['ffn_conv_w'], ffn_conv[li], p['final_norm_g'][None],
            tm=tiles['ffn'], tf=512, final_norm=li == depth - 1, layer=li)
        new = [logf, s_new, gconv_new, h_new[:, 0, :], lconv_new, fconv_new]
        for lst, arr in zip(outs, new):
            lst.append(arr)
    kv_shape = (depth, b, l, fox_h, HEAD_DIM)
    return x, [kv[0].reshape(kv_shape), kv[1].reshape(kv_shape)] + [jnp.stack(lst) for lst in outs]


def kernel(x_prompt, x_sample, cache_fox_k, cache_fox_v, cache_fox_logf, state_gdn, state_gdn_conv,
           state_lru, state_lru_conv, state_ffn_conv, norm_mix_g, w_in, gdn_conv_w, gdn_a_log,
           gdn_dt_bias, gdn_norm_g, fox_f_bias, fox_norm_g, lru_conv_w, lru_conv_b, lru_w_a, lru_b_a,
           lru_w_x, lru_b_x, lru_lambda, lru_norm_g, w_out, norm_ffn_g, ffn_w_up, ffn_conv_w,
           ffn_w_down, final_norm_g):
    p = dict(norm_mix_g=norm_mix_g, gdn_conv_w=gdn_conv_w, gdn_a_log=gdn_a_log,
             gdn_dt_bias=gdn_dt_bias, gdn_norm_g=gdn_norm_g, fox_f_bias=fox_f_bias,
             fox_norm_g=fox_norm_g, lru_conv_w=lru_conv_w, lru_conv_b=lru_conv_b, lru_w_a=lru_w_a,
             lru_b_a=lru_b_a, lru_w_x=lru_w_x, lru_b_x=lru_b_x, lru_lambda=lru_lambda,
             lru_norm_g=lru_norm_g, norm_ffn_g=norm_ffn_g, ffn_conv_w=ffn_conv_w,
             final_norm_g=final_norm_g)
    depth = w_in.shape[0]
    bp, lp, _ = x_prompt.shape
    gdn_h = state_gdn.shape[2]
    fox_h = cache_fox_k.shape[3]
    lru_w = state_lru.shape[2]
    dims = (gdn_h * HEAD_DIM, gdn_h, fox_h * HEAD_DIM, fox_h, lru_w)
    dff = ffn_w_down.shape[1]
    weights = _matmul_weights(w_in, w_out, ffn_w_up, ffn_w_down, dims)

    zeros = lambda *s: jnp.zeros((depth, bp) + s, F32)
    prompt_states = (zeros(gdn_h, HEAD_DIM, HEAD_DIM), zeros(gdn_conv_w.shape[1] - 1, 3 * dims[0]),
                     zeros(lru_w), zeros(lru_conv_w.shape[1] - 1, lru_w),
                     zeros(ffn_conv_w.shape[1] - 1, dff))
    prompt_tiles = dict(proj=min(256, bp * lp), attn=min(512, lp), lru=min(256, lp), ffn=min(512, lp),
                        gdn=min(256, lp))
    y_prompt, sp = _run_group(x_prompt, None, prompt_states, p, weights, dims, prompt_tiles)

    bs, ls, _ = x_sample.shape
    sample_tiles = dict(proj=bs * ls, lru=ls, ffn=ls, gdn=ls)
    y_sample, ss = _run_group(
        x_sample, (cache_fox_k, cache_fox_v, cache_fox_logf),
        (state_gdn, state_gdn_conv, state_lru, state_lru_conv, state_ffn_conv), p, weights, dims,
        sample_tiles)
    return (y_prompt, y_sample, *sp, *ss)
```

```python
import functools
import math

import jax
import jax.numpy as jnp
from jax import lax
from jax.experimental import pallas as pl
from jax.experimental.pallas import tpu as pltpu

F32 = jnp.float32
BF16 = jnp.bfloat16

HEAD_DIM = 128
CHUNK = 64
LRU_C = 8.0
EPS = 1e-6
LANES = 128
SUBLANES = 8
VMEM_LIMIT = 56 * 1024 * 1024

GATE_COLS = LANES
GATE_F = 0
FFN_SPLIT = 2
GDN_SEQS = 2
NEG_INF = float("-inf")
LOG2E = math.log2(math.e)


def _params(n_axes):
    return pltpu.CompilerParams(dimension_semantics=("arbitrary",) * n_axes,
                                vmem_limit_bytes=VMEM_LIMIT)


def _resident(block_shape, index_map):
    return pl.BlockSpec(block_shape, index_map, pipeline_mode=pl.Buffered(1))


def _dot(a, b):
    return jnp.dot(a.astype(BF16), b.astype(BF16), preferred_element_type=F32)


def _dot_nt(a, b):
    return lax.dot_general(a.astype(BF16), b.astype(BF16), (((1,), (1,)), ((), ())),
                           preferred_element_type=F32)


def _dot_tn(a, b):
    return lax.dot_general(a.astype(BF16), b.astype(BF16), (((0,), (0,)), ((), ())),
                           preferred_element_type=F32)


def _rms(x, g):
    return x * lax.rsqrt(jnp.mean(x * x, axis=-1, keepdims=True) + EPS) * g


def _expm1(x):
    u = jnp.exp(x)
    um1 = u - 1.0
    near = jnp.where(u == 1.0, x, um1 * x / jnp.log(u))
    return jnp.where(jnp.abs(x) > 0.5, um1, near)


def _cumsum_lanes(x):
    lane = lax.broadcasted_iota(jnp.int32, x.shape, x.ndim - 1)
    s = 1
    while s < x.shape[-1]:
        x = x + jnp.where(lane >= s, pltpu.roll(x, s, x.ndim - 1), 0.0)
        s *= 2
    return x


def _cumsum_rows(x, seg):
    pos = lax.broadcasted_iota(jnp.int32, x.shape, 0) % seg
    s = 1
    while s < seg:
        x = x + jnp.where(pos >= s, pltpu.roll(x, s, 0), 0.0)
        s *= 2
    return x


def _causal_conv(buf, x, w_ref, width, rows, cols=slice(None)):
    lo = SUBLANES - (width - 1)
    buf[SUBLANES:SUBLANES + rows, cols] = x
    y = w_ref[width - 1:width, cols] * x
    for j in range(width - 1):
        y = y + w_ref[j:j + 1, cols] * buf[lo + j:lo + j + rows, cols]
    return y


def _inproj_kernel(*refs, n_fox, seg, layer):
    x_ref, g_ref, wa_ref, wb_ref, wf_ref, ws_ref = refs[:6]
    n_in = 8 if layer else 6
    main_ref, small_ref, fk_ref, fv_ref, qkv_ref, h_scr = refs[n_in:]
    if layer:
        fk_ref[:layer] = refs[6][...]
        fv_ref[:layer] = refs[7][...]
    h_scr[...] = _rms(x_ref[...], g_ref[...]).astype(BF16)

    def cols(w_ref, c0, width):
        return jnp.dot(h_scr[...], w_ref[:, c0:c0 + width], preferred_element_type=F32)

    n_a = wa_ref.shape[1]
    for c0 in range(0, n_a, seg):
        main_ref[:, c0:c0 + seg] = cols(wa_ref, c0, seg)
    for c0 in range(0, wb_ref.shape[1], seg):
        main_ref[:, n_a + c0:n_a + c0 + seg] = cols(wb_ref, c0, seg)
    for j in range(3):
        for c in range(0, n_fox, seg):
            p = cols(wf_ref, j * n_fox + c, seg)
            qkv_ref[:, j * n_fox + c:j * n_fox + c + seg] = p.astype(BF16)
            if j == 1:
                fk_ref[layer, :, c:c + seg] = p
            if j == 2:
                fv_ref[layer, :, c:c + seg] = p
    small_ref[...] = cols(ws_ref, 0, GATE_COLS)


def _inproj(x2d, g, w_slabs, kv_prev, *, n_main, n_fox, tm, layer):
    m, d = x2d.shape
    row = lambda i: (i, 0)
    kv_row = lambda i: (0, i, 0)
    pick = lambda i: (layer, 0, 0)
    in_specs = [pl.BlockSpec((tm, d), row), _resident((None, 1, d), pick)]
    in_specs += [_resident((None, d, w.shape[2]), pick) for w in w_slabs]
    operands = [x2d, g, *w_slabs]
    if layer:
        in_specs += [pl.BlockSpec((layer, tm, n_fox), kv_row)] * 2
        operands += list(kv_prev)
    kv_out = jax.ShapeDtypeStruct((layer + 1, m, n_fox), F32)
    return pl.pallas_call(
        functools.partial(_inproj_kernel, n_fox=n_fox, seg=512, layer=layer),
        grid=(m // tm,),
        in_specs=in_specs,
        out_specs=[pl.BlockSpec((tm, n_main), row), pl.BlockSpec((tm, GATE_COLS), row),
                   pl.BlockSpec((layer + 1, tm, n_fox), kv_row),
                   pl.BlockSpec((layer + 1, tm, n_fox), kv_row),
                   pl.BlockSpec((tm, 3 * n_fox), row)],
        out_shape=[jax.ShapeDtypeStruct((m, n_main), F32), jax.ShapeDtypeStruct((m, GATE_COLS), F32),
                   kv_out, kv_out, jax.ShapeDtypeStruct((m, 3 * n_fox), BF16)],
        scratch_shapes=[pltpu.VMEM((tm, d), BF16)],
        compiler_params=_params(1), name="inproj")(*operands)


def _gate_kernel(sm_ref, b_ref, logf_ref, *rest, heads, cumulative):
    ls = jax.nn.log_sigmoid(sm_ref[0] + b_ref[...])
    logf_ref[0] = ls[:, GATE_F:GATE_F + heads]
    if cumulative:
        ft_ref, = rest
        ft_ref[0] = _cumsum_lanes(ls.T[GATE_F:GATE_F + heads, :])


def _gate(small3d, bias_row, *, heads, cumulative):
    b, rows, _ = small3d.shape
    out_specs = [pl.BlockSpec((1, rows, heads), lambda i: (i, 0, 0))]
    out_shape = [jax.ShapeDtypeStruct((b, rows, heads), F32)]
    if cumulative:
        out_specs.append(pl.BlockSpec((1, heads, rows), lambda i: (i, 0, 0)))
        out_shape.append(jax.ShapeDtypeStruct((b, heads, rows), F32))
    return pl.pallas_call(
        functools.partial(_gate_kernel, heads=heads, cumulative=cumulative),
        grid=(b,),
        in_specs=[pl.BlockSpec((1, rows, GATE_COLS), lambda i: (i, 0, 0)),
                  pl.BlockSpec((1, GATE_COLS), lambda i: (0, 0))],
        out_specs=out_specs, out_shape=out_shape,
        compiler_params=_params(1), name="fox_gate")(small3d, bias_row)


def _split_bf16(x):
    hi = x.astype(BF16)
    return hi, (x - hi.astype(F32)).astype(BF16)


def _dot_split(a, b):
    a_hi, a_lo = _split_bf16(a)
    b_hi, b_lo = _split_bf16(b)
    mm = lambda x, y: jnp.dot(x, y, preferred_element_type=F32)
    return mm(a_hi, b_hi) + (mm(a_lo, b_hi) + mm(a_hi, b_lo))


def _unit_lower_solve(a, rhs, c):
    r = a[0].shape[0]
    row = lax.broadcasted_iota(jnp.int32, (r, r), 0)
    col = lax.broadcasted_iota(jnp.int32, (r, r), 1)
    eye = jnp.where(row == col, 1.0, 0.0)
    p = [-x for x in a]
    t = [eye + x for x in p]
    for _ in range(int(math.log2(c)) - 1):
        p = [_dot(x, x) for x in p]
        t = [ti + _dot(ti, pi) for ti, pi in zip(t, p)]
    x = [_dot(ti, bi) for ti, bi in zip(t, rhs)]
    resid = [bi - xi - _dot_split(ai, xi) for ai, bi, xi in zip(a, rhs, x)]
    return [xi + _dot(ti, ri) for ti, xi, ri in zip(t, x, resid)]


def _gdn_kernel(qkv_ref, z_ref, sm_ref, conv0_ref, s0_ref, cw_ref, alog_ref, dt_ref, ng_ref,
                o_ref, s_ref, conv_ref, buf, *, nbat, rows, c, heads, conv_w, beta_col, a_col):
    ti = pl.program_id(1)
    nc = rows // c
    width = heads * HEAD_DIM
    hist = conv_w - 1
    lo = SUBLANES - hist

    @pl.when(ti == 0)
    def _():
        buf[:, lo:SUBLANES, :] = conv0_ref[...]
        s_ref[...] = s0_ref[...]

    row = lax.broadcasted_iota(jnp.int32, (rows, rows), 0)
    col = lax.broadcasted_iota(jnp.int32, (rows, rows), 1)
    lower = row >= col
    if nc > 1:
        lower = lower & (row // c == col // c)
    strict = lower & (row != col)
    pad = max(LANES - rows, 0)

    ys, beta, gcum, g_end, gcum_t = [], [], [], [], []
    for bi in range(nbat):
        y = _causal_conv(buf.at[bi], qkv_ref[bi], cw_ref, conv_w, rows)
        new_hist = buf[bi, lo + rows:SUBLANES + rows, :]
        conv_ref[bi] = new_hist
        buf[bi, lo:SUBLANES, :] = new_hist
        ys.append(jax.nn.silu(y))
        sm = sm_ref[bi]
        beta.append(jax.nn.sigmoid(sm))
        g = -jnp.exp(alog_ref[...]) * jax.nn.softplus(sm + dt_ref[...])
        gc_b = _cumsum_rows(g, c)
        gcum.append(gc_b)
        g_end.append(jnp.concatenate(
            [jnp.broadcast_to(gc_b[(i + 1) * c - 1:(i + 1) * c, :], (c, GATE_COLS))
             for i in range(nc)], axis=0))
        gcum_t.append(jnp.concatenate([gc_b, jnp.zeros((pad, GATE_COLS), F32)], axis=0).T
                      if pad else gc_b.T)

    chains = [(bi, h) for bi in range(nbat) for h in range(heads)]
    ids = range(len(chains))
    head = lambda bi, base, h: ys[bi][:, base + h * HEAD_DIM:base + (h + 1) * HEAD_DIM]
    l2 = lambda t: t * lax.rsqrt(jnp.sum(t * t, axis=-1, keepdims=True) + EPS)
    q = [l2(head(bi, 0, h)) * HEAD_DIM ** -0.5 for bi, h in chains]
    k = [l2(head(bi, width, h)) for bi, h in chains]
    v = [head(bi, 2 * width, h) for bi, h in chains]
    bh = [beta[bi][:, beta_col + h:beta_col + h + 1] for bi, h in chains]
    gc = [gcum[bi][:, a_col + h:a_col + h + 1] for bi, h in chains]
    ge = [g_end[bi][:, a_col + h:a_col + h + 1] for bi, h in chains]
    gr = [gcum_t[bi][a_col + h:a_col + h + 1, :rows] for bi, h in chains]
    decay = [jnp.exp(jnp.where(lower, gc[n] - gr[n], NEG_INF)) for n in ids]
    kb = [k[n] * bh[n] for n in ids]
    eg = [jnp.exp(gc[n]) for n in ids]
    a = [_dot_nt(kb[n], k[n]) * jnp.where(strict, decay[n], 0.0) for n in ids]
    attn = [_dot_nt(q[n], k[n]) * decay[n] for n in ids]
    sol = _unit_lower_solve(a, [jnp.concatenate([v[n] * bh[n], kb[n] * eg[n]], axis=-1) for n in ids], c)
    qe = [q[n] * eg[n] for n in ids]
    kx = [k[n] * jnp.exp(ge[n] - gc[n]) for n in ids]
    s = [s_ref[bi, h] for bi, h in chains]
    deltas = [[] for _ in ids]
    inter = [[] for _ in ids]
    for i in range(nc):
        rs = slice(i * c, (i + 1) * c)
        d_i = [sol[n][rs, :HEAD_DIM] - _dot(sol[n][rs, HEAD_DIM:], s[n]) for n in ids]
        for n in ids:
            inter[n].append(_dot(qe[n][rs], s[n]))
            deltas[n].append(d_i[n])
        s = [jnp.exp(ge[n][(i + 1) * c - 1:(i + 1) * c, :]) * s[n] + _dot_tn(kx[n][rs], d_i[n])
             for n in ids]
    cat = lambda parts: parts[0] if nc == 1 else jnp.concatenate(parts, axis=0)
    for n, (bi, h) in enumerate(chains):
        s_ref[bi, h] = s[n]
        o = cat(inter[n]) + _dot(attn[n], cat(deltas[n]))
        zh = z_ref[bi, :, h * HEAD_DIM:(h + 1) * HEAD_DIM]
        o_ref[bi, :, h * HEAD_DIM:(h + 1) * HEAD_DIM] = (
            _rms(o, ng_ref[...]) * jax.nn.silu(zh)).astype(o_ref.dtype)


def _gdn(main3d, small3d, conv0, s0, conv_w, alog_row, dt_row, norm_g, *, nbat, rows, c, heads,
         beta_col, a_col):
    b, l, _ = main3d.shape
    width = heads * HEAD_DIM
    kw = conv_w.shape[0]
    fixed2 = lambda i, j: (0, 0)
    return pl.pallas_call(
        functools.partial(_gdn_kernel, nbat=nbat, rows=rows, c=c, heads=heads, conv_w=kw,
                          beta_col=beta_col, a_col=a_col),
        grid=(b // nbat, l // rows),
        in_specs=[pl.BlockSpec((nbat, rows, 3 * width), lambda i, j: (i, j, 0)),
                  pl.BlockSpec((nbat, rows, width), lambda i, j: (i, j, 3)),
                  pl.BlockSpec((nbat, rows, GATE_COLS), lambda i, j: (i, j, 0)),
                  pl.BlockSpec((nbat, kw - 1, 3 * width), lambda i, j: (i, 0, 0)),
                  pl.BlockSpec((nbat, heads, HEAD_DIM, HEAD_DIM), lambda i, j: (i, 0, 0, 0)),
                  pl.BlockSpec((kw, 3 * width), fixed2),
                  pl.BlockSpec((1, GATE_COLS), fixed2), pl.BlockSpec((1, GATE_COLS), fixed2),
                  pl.BlockSpec((1, HEAD_DIM), fixed2)],
        out_specs=[pl.BlockSpec((nbat, rows, width), lambda i, j: (i, j, 0)),
                   pl.BlockSpec((nbat, heads, HEAD_DIM, HEAD_DIM), lambda i, j: (i, 0, 0, 0)),
                   pl.BlockSpec((nbat, kw - 1, 3 * width), lambda i, j: (i, 0, 0))],
        out_shape=[jax.ShapeDtypeStruct((b, l, width), BF16),
                   jax.ShapeDtypeStruct((b, heads, HEAD_DIM, HEAD_DIM), F32),
                   jax.ShapeDtypeStruct((b, kw - 1, 3 * width), F32)],
        scratch_shapes=[pltpu.VMEM((nbat, SUBLANES + rows, 3 * width), F32)],
        compiler_params=_params(2), name="gdn")(
            main3d, main3d, small3d, conv0, s0, conv_w, alog_row, dt_row, norm_g)


def _fox_prompt_kernel(q_ref, k_ref, v_ref, ft_ref, g_ref, o_ref, vaug, m_scr, acc_scr, s_a, s_b,
                       *, tq):
    nq = q_ref.shape[1] // tq
    vaug[:, :HEAD_DIM] = v_ref[0]
    vaug[:, HEAD_DIM:] = jnp.ones((vaug.shape[0], HEAD_DIM), vaug.dtype)
    scale = HEAD_DIM ** -0.5

    def block_start(j):
        return j * tq if isinstance(j, int) else pl.multiple_of(j * tq, tq)

    def scores(buf, qi, j):
        buf[...] = _dot_nt(q_ref[0, qi * tq:(qi + 1) * tq, :], k_ref[0, pl.ds(block_start(j), tq), :])

    def update(buf, j, r0, nrows, ncols, masked):
        start = block_start(j)
        rs = slice(r0, r0 + nrows)
        s = buf[rs, :ncols] * (scale * LOG2E) - ft_ref[0, 0, pl.ds(j, 1), :ncols] * LOG2E
        if masked:
            row = lax.broadcasted_iota(jnp.int32, (nrows, ncols), 0) + r0
            col = lax.broadcasted_iota(jnp.int32, (nrows, ncols), 1)
            s = jnp.where(col <= row, s, NEG_INF)
        m_prev = m_scr[rs, :]
        m_new = jnp.maximum(m_prev, jnp.max(s, axis=-1, keepdims=True))
        alpha = jnp.exp2(m_prev - m_new)
        p = jnp.exp2(s - jnp.concatenate([m_new] * (ncols // LANES), axis=1))
        pv = _dot(p, vaug[pl.ds(start, ncols), :])
        acc_scr[rs, :] = (jnp.concatenate([alpha] * (2 * HEAD_DIM // LANES), axis=1) * acc_scr[rs, :]
                          + pv)
        m_scr[rs, :] = m_new

    def accumulate(buf, j, masked):
        if masked and tq % (2 * LANES) == 0:
            update(buf, j, 0, tq // 2, tq // 2, True)
            update(buf, j, tq // 2, tq // 2, tq, True)
        else:
            update(buf, j, 0, tq, tq, masked)

    cur, other = s_a, s_b
    scores(cur, 0, 0)
    for qi in range(nq):
        m_scr[...] = jnp.full(m_scr.shape, NEG_INF, F32)
        acc_scr[...] = jnp.zeros(acc_scr.shape, F32)

        def pair(jj, carry, qi=qi, cur=cur, other=other):
            j = 2 * jj
            scores(other, qi, j + 1)
            accumulate(cur, j, False)
            scores(cur, qi, j + 2)
            accumulate(other, j + 1, False)
            return carry

        if qi >= 2:
            lax.fori_loop(0, qi // 2, pair, 0)
        if qi % 2 == 0:
            if qi + 1 < nq:
                scores(other, qi + 1, 0)
            accumulate(cur, qi, True)
            cur, other = other, cur
        else:
            scores(other, qi, qi)
            accumulate(cur, qi - 1, False)
            if qi + 1 < nq:
                scores(cur, qi + 1, 0)
            accumulate(other, qi, True)
        o = acc_scr[:, :HEAD_DIM] / acc_scr[:, HEAD_DIM:]
        o_ref[0, qi * tq:(qi + 1) * tq, :] = _rms(o, g_ref[0]).astype(o_ref.dtype)


def _fox_prompt(qkv3d, ft, norm_g, *, heads, tq):
    b, l, _ = qkv3d.shape
    ft4 = ft.reshape(b, heads, l // tq, tq)
    g3 = norm_g.reshape(heads, 1, HEAD_DIM)
    head_block = lambda part: pl.BlockSpec((1, l, HEAD_DIM), lambda i, h: (i, 0, part * heads + h))
    return pl.pallas_call(
        functools.partial(_fox_prompt_kernel, tq=tq),
        grid=(b, heads),
        in_specs=[head_block(0), head_block(1), head_block(2),
                  pl.BlockSpec((1, 1, l // tq, tq), lambda i, h: (i, h, 0, 0)),
                  pl.BlockSpec((1, 1, HEAD_DIM), lambda i, h: (h, 0, 0))],
        out_specs=pl.BlockSpec((1, l, HEAD_DIM), lambda i, h: (i, 0, h)),
        out_shape=jax.ShapeDtypeStruct((b, l, heads * HEAD_DIM), BF16),
        scratch_shapes=[pltpu.VMEM((l, 2 * HEAD_DIM), BF16), pltpu.VMEM((tq, LANES), F32),
                        pltpu.VMEM((tq, 2 * HEAD_DIM), F32), pltpu.VMEM((tq, tq), F32),
                        pltpu.VMEM((tq, tq), F32)],
        compiler_params=_params(2), name="fox_prompt")(qkv3d, qkv3d, qkv3d, ft4, g3)


def _fox_sample_kernel(q_ref, kn_ref, vn_ref, ck_ref, cv_ref, clt_ref, lnt_ref, g_ref, o_ref,
                       fc_scr, m_scr, l_scr, acc_scr, *, ls, heads, tp):
    j = pl.program_id(1)
    last = pl.num_programs(1) - 1
    scale = HEAD_DIM ** -0.5

    @pl.when(j == 0)
    def _():
        fc = _cumsum_lanes(clt_ref[0])
        for c in range(fc_scr.shape[0]):
            fc_scr[c] = fc[:, c * tp:(c + 1) * tp]
        m_scr[...] = jnp.full(m_scr.shape, NEG_INF, F32)
        l_scr[...] = jnp.zeros(l_scr.shape, F32)
        acc_scr[...] = jnp.zeros(acc_scr.shape, F32)

    def update(h, s, v):
        m_prev = m_scr[h]
        m_new = jnp.maximum(m_prev, jnp.max(s, axis=-1, keepdims=True))
        alpha = jnp.exp(m_prev - m_new)
        p = jnp.exp(s - m_new)
        l_scr[h] = alpha * l_scr[h] + jnp.sum(p, axis=-1, keepdims=True)
        acc_scr[h] = alpha * acc_scr[h] + _dot(p, v)
        m_scr[h] = m_new

    fcj = fc_scr[j]
    for h in range(heads):
        cs = slice(h * HEAD_DIM, (h + 1) * HEAD_DIM)
        k_h = ck_ref[0, pl.ds(h, tp, stride=heads), :]
        v_h = cv_ref[0, pl.ds(h, tp, stride=heads), :]
        update(h, _dot_nt(q_ref[0, :, cs], k_h) * scale - fcj[h:h + 1, :], v_h)

    @pl.when(j == last)
    def _():
        fn = fcj[:, tp - 1:tp] + _cumsum_lanes(lnt_ref[0])[:, :ls]
        row = lax.broadcasted_iota(jnp.int32, (ls, ls), 0)
        col = lax.broadcasted_iota(jnp.int32, (ls, ls), 1)
        for h in range(heads):
            cs = slice(h * HEAD_DIM, (h + 1) * HEAD_DIM)
            s_n = _dot_nt(q_ref[0, :, cs], kn_ref[0, :, cs]) * scale - fn[h:h + 1, :]
            update(h, jnp.where(col <= row, s_n, NEG_INF), vn_ref[0, :, cs])
            o = acc_scr[h] / l_scr[h]
            o_ref[0, :, cs] = _rms(o, g_ref[h:h + 1, :]).astype(o_ref.dtype)


def _fox_sample(qkv3d, cache_k, cache_v, cache_logf_t, logf_new_t, norm_g, *, heads, layer, tp):
    b, ls, _ = qkv3d.shape
    depth, _, p_len = cache_k.shape[:3]
    width = heads * HEAD_DIM
    ck = cache_k.reshape(depth * b, p_len * heads, HEAD_DIM)
    cv = cache_v.reshape(depth * b, p_len * heads, HEAD_DIM)
    cache_idx = lambda i, j: (layer * b + i, j, 0)
    return pl.pallas_call(
        functools.partial(_fox_sample_kernel, ls=ls, heads=heads, tp=tp),
        grid=(b, p_len // tp),
        in_specs=[pl.BlockSpec((1, ls, width), lambda i, j: (i, 0, 0)),
                  pl.BlockSpec((1, ls, width), lambda i, j: (i, 0, 1)),
                  pl.BlockSpec((1, ls, width), lambda i, j: (i, 0, 2)),
                  pl.BlockSpec((1, tp * heads, HEAD_DIM), cache_idx),
                  pl.BlockSpec((1, tp * heads, HEAD_DIM), cache_idx),
                  pl.BlockSpec((1, heads, p_len), lambda i, j: (i, 0, 0)),
                  pl.BlockSpec((1, heads, LANES), lambda i, j: (i, 0, 0)),
                  pl.BlockSpec((heads, HEAD_DIM), lambda i, j: (0, 0))],
        out_specs=pl.BlockSpec((1, ls, width), lambda i, j: (i, 0, 0)),
        out_shape=jax.ShapeDtypeStruct((b, ls, width), BF16),
        scratch_shapes=[pltpu.VMEM((p_len // tp, heads, tp), F32), pltpu.VMEM((heads, ls, 1), F32),
                        pltpu.VMEM((heads, ls, 1), F32), pltpu.VMEM((heads, ls, HEAD_DIM), F32)],
        compiler_params=_params(2), name="fox_sample")(
            qkv3d, qkv3d, qkv3d, ck, cv, cache_logf_t, logf_new_t, norm_g)


def _lru_kernel(x_ref, gate_ref, conv0_ref, h0_ref, cw_ref, cb_ref, wa_ref, wx_ref, ba_ref, bx_ref,
                lam_ref, ng_ref, o_ref, h_ref, conv_ref, buf, a_scr, hs_scr, *, tl, blocks, conv_w):
    ti = pl.program_id(1)
    hist = conv_w - 1
    lo = SUBLANES - hist
    bw = x_ref.shape[2] // blocks

    @pl.when(ti == 0)
    def _():
        buf[lo:SUBLANES, :] = conv0_ref[0]
        h_ref[0] = h0_ref[0]

    xc = _causal_conv(buf, x_ref[0], cw_ref, conv_w, tl) + cb_ref[...]
    new_hist = buf[lo + tl:SUBLANES + tl, :]
    conv_ref[0] = new_hist
    buf[lo:SUBLANES, :] = new_hist

    for n in range(blocks):
        sl = slice(n * bw, (n + 1) * bw)
        xb = xc[:, sl]
        r = jax.nn.sigmoid(_dot(xb, wa_ref[n]) + ba_ref[:, sl])
        i = jax.nn.sigmoid(_dot(xb, wx_ref[n]) + bx_ref[:, sl])
        log_a = -LRU_C * r * jax.nn.softplus(-lam_ref[:, sl])
        a_scr[:, sl] = jnp.exp(log_a)
        hs_scr[:, sl] = jnp.sqrt(-_expm1(2.0 * log_a)) * (i * xb)

    row = lax.broadcasted_iota(jnp.int32, (SUBLANES, a_scr.shape[1]), 0)

    def group(gi, h):
        start = pl.multiple_of(gi * SUBLANES, SUBLANES)
        a = a_scr[pl.ds(start, SUBLANES), :]
        u = hs_scr[pl.ds(start, SUBLANES), :]
        s = 1
        while s < SUBLANES:
            keep = row >= s
            u = jnp.where(keep, u + a * pltpu.roll(u, s, 0), u)
            a = jnp.where(keep, a * pltpu.roll(a, s, 0), a)
            s *= 2
        hg = u + a * h
        hs_scr[pl.ds(start, SUBLANES), :] = hg
        return hg[SUBLANES - 1:SUBLANES, :]

    h_ref[0] = lax.fori_loop(0, tl // SUBLANES, group, h_ref[0], unroll=4 if tl >= 32 else True)

    gate = gate_ref[0]
    for n in range(blocks):
        sl = slice(n * bw, (n + 1) * bw)
        o_ref[0, :, sl] = (_rms(hs_scr[:, sl], ng_ref[:, sl])
                           * jax.nn.gelu(gate[:, sl])).astype(o_ref.dtype)


def _lru(main3d, conv0, h0, cw, cb, wa, wx, ba, bx, lam, ng, *, tl, x_block, gate_block):
    b, l, _ = main3d.shape
    blocks, bw, _ = wa.shape
    width = blocks * bw
    kw = cw.shape[0]
    fixed2 = lambda i, j: (0, 0)
    fixed3 = lambda i, j: (0, 0, 0)
    vec = pl.BlockSpec((1, width), fixed2)
    return pl.pallas_call(
        functools.partial(_lru_kernel, tl=tl, blocks=blocks, conv_w=kw),
        grid=(b, l // tl),
        in_specs=[pl.BlockSpec((1, tl, width), lambda i, j: (i, j, x_block)),
                  pl.BlockSpec((1, tl, width), lambda i, j: (i, j, gate_block)),
                  pl.BlockSpec((1, kw - 1, width), lambda i, j: (i, 0, 0)),
                  pl.BlockSpec((1, 1, width), lambda i, j: (i, 0, 0)),
                  pl.BlockSpec((kw, width), fixed2), vec,
                  pl.BlockSpec((blocks, bw, bw), fixed3), pl.BlockSpec((blocks, bw, bw), fixed3),
                  vec, vec, vec, vec],
        out_specs=[pl.BlockSpec((1, tl, width), lambda i, j: (i, j, 0)),
                   pl.BlockSpec((1, 1, width), lambda i, j: (i, 0, 0)),
                   pl.BlockSpec((1, kw - 1, width), lambda i, j: (i, 0, 0))],
        out_shape=[jax.ShapeDtypeStruct((b, l, width), BF16),
                   jax.ShapeDtypeStruct((b, 1, width), F32),
                   jax.ShapeDtypeStruct((b, kw - 1, width), F32)],
        scratch_shapes=[pltpu.VMEM((SUBLANES + tl, width), F32), pltpu.VMEM((tl, width), F32),
                        pltpu.VMEM((tl, width), F32)],
        compiler_params=_params(2), name="rglru")(
            main3d, main3d, conv0, h0, cw, cb, wa, wx, ba, bx, lam, ng)


def _outproj_kernel(x_ref, og_ref, of_ref, ol_ref, w_ref, y_ref):
    wg, wf = og_ref.shape[1], of_ref.shape[1]
    y = x_ref[...] + jnp.dot(og_ref[...], w_ref[0:wg, :], preferred_element_type=F32)
    y = y + jnp.dot(of_ref[...], w_ref[wg:wg + wf, :], preferred_element_type=F32)
    y_ref[...] = y + jnp.dot(ol_ref[...], w_ref[wg + wf:, :], preferred_element_type=F32)


def _outproj(x2d, o_gdn, o_fox, o_lru, w, *, tm, layer):
    m, d = x2d.shape
    row = lambda i: (i, 0)
    return pl.pallas_call(
        _outproj_kernel, grid=(m // tm,),
        in_specs=[pl.BlockSpec((tm, d), row), pl.BlockSpec((tm, o_gdn.shape[1]), row),
                  pl.BlockSpec((tm, o_fox.shape[1]), row), pl.BlockSpec((tm, o_lru.shape[1]), row),
                  _resident((None,) + w.shape[1:], lambda i: (layer, 0, 0))],
        out_specs=pl.BlockSpec((tm, d), row),
        out_shape=jax.ShapeDtypeStruct((m, d), F32),
        compiler_params=_params(1), name="outproj")(x2d, o_gdn, o_fox, o_lru, w)


def _ffn_kernel(x_ref, g_ref, wg_ref, wv_ref, wd_ref, cw_ref, c0_ref, fg_ref, y_ref, cnew_ref,
                h_scr, gp_scr, *, nb, tiles_per_batch, tm, conv_w, final_norm):
    gi = pl.program_id(0)
    f = pl.program_id(1)
    hist = conv_w - 1
    lo = SUBLANES - hist

    rows = lambda j: slice(j * tm, (j + 1) * tm)

    @pl.when(f == 0)
    def _():
        for j in range(nb):
            x = x_ref[0, j]
            h_scr[rows(j), :] = _rms(x, g_ref[...]).astype(BF16)
            y_ref[0, j] = x

    if tiles_per_batch > 1:
        first = gi % tiles_per_batch == 0

        @pl.when(first)
        def _():
            gp_scr[0, lo:SUBLANES, :] = c0_ref[0]

        @pl.when(jnp.logical_not(first))
        def _():
            gp_scr[0, lo:SUBLANES, :] = cnew_ref[0, f]
    else:
        for j in range(nb):
            gp_scr[j, lo:SUBLANES, :] = c0_ref[j]

    tf = wg_ref.shape[1]
    half = tf // FFN_SPLIT
    h = h_scr[...]
    ups = []
    for c in range(FFN_SPLIT):
        cs = slice(c * half, (c + 1) * half)
        ups.append((jnp.dot(h, wg_ref[:, cs], preferred_element_type=F32),
                    jnp.dot(h, wv_ref[:, cs], preferred_element_type=F32)))
    contrib = None
    for c in range(FFN_SPLIT):
        cs = slice(c * half, (c + 1) * half)
        gp, val = ups[c]
        acts = [jax.nn.silu(_causal_conv(gp_scr.at[j], gp[rows(j)], cw_ref, conv_w, tm, cs))
                * val[rows(j)] for j in range(nb)]
        act = (acts[0] if nb == 1 else jnp.concatenate(acts, axis=0)).astype(BF16)
        part = jnp.dot(act, wd_ref[cs, :], preferred_element_type=F32)
        contrib = part if contrib is None else contrib + part
    for j in range(nb):
        cnew_ref[j, f] = gp_scr[j, lo + tm:SUBLANES + tm, :]
        y_ref[0, j] += contrib[rows(j)]

    if final_norm:
        @pl.when(f == pl.num_programs(1) - 1)
        def _():
            for j in range(nb):
                y_ref[0, j] = _rms(y_ref[0, j], fg_ref[...])


def _ffn(x3d, g, w_up, w_down, cw, c0, fg, *, tm, tf, final_norm, layer):
    b, l, d = x3d.shape
    dff = w_down.shape[1]
    kw = cw.shape[1]
    tiles_per_batch = l // tm
    nb = b if tiles_per_batch == 1 else 1
    groups = b * l // (nb * tm)
    x4 = x3d.reshape(groups, nb, tm, d)
    nf = dff // tf
    if tiles_per_batch == 1:
        c_idx = lambda i, f: (0, 0, f)
        cnew_idx = lambda i, f: (0, 0, 0, 0)
    else:
        c_idx = lambda i, f: (i // tiles_per_batch, 0, f)
        cnew_idx = lambda i, f: (i // tiles_per_batch, 0, 0, 0)
    y, cnew = pl.pallas_call(
        functools.partial(_ffn_kernel, nb=nb, tiles_per_batch=tiles_per_batch, tm=tm, conv_w=kw,
                          final_norm=final_norm),
        grid=(groups, dff // tf),
        in_specs=[pl.BlockSpec((1, nb, tm, d), lambda i, f: (i, 0, 0, 0)),
                  pl.BlockSpec((None, 1, d), lambda i, f: (layer, 0, 0)),
                  pl.BlockSpec((None, d, tf), lambda i, f: (layer, 0, f)),
                  pl.BlockSpec((None, d, tf), lambda i, f: (layer, 0, nf + f)),
                  pl.BlockSpec((None, tf, d), lambda i, f: (layer, f, 0)),
                  pl.BlockSpec((None, kw, tf), lambda i, f: (layer, 0, f)),
                  pl.BlockSpec((nb, kw - 1, tf), c_idx),
                  pl.BlockSpec((1, d), lambda i, f: (0, 0))],
        out_specs=[pl.BlockSpec((1, nb, tm, d), lambda i, f: (i, 0, 0, 0)),
                   pl.BlockSpec((nb, nf, kw - 1, tf), cnew_idx)],
        out_shape=[jax.ShapeDtypeStruct((groups, nb, tm, d), F32),
                   jax.ShapeDtypeStruct((b, nf, kw - 1, tf), F32)],
        scratch_shapes=[pltpu.VMEM((nb * tm, d), BF16), pltpu.VMEM((nb, SUBLANES + tm, tf), F32)],
        compiler_params=_params(2), name="ffn")(x4, g, w_up, w_up, w_down, cw, c0, fg)
    return y.reshape(b, l, d), jnp.swapaxes(cnew, 1, 2).reshape(b, kw - 1, dff)


def _pad_cols(a, width):
    return jnp.pad(a, ((0, 0), (0, width - a.shape[1])))


def _matmul_weights(w_in, w_out, ffn_w_up, ffn_w_down, sizes):
    gdn_w, gdn_h, fox_w, fox_h, lru_w = sizes
    offs = [0]
    for s in (gdn_w, gdn_w, gdn_w, gdn_w, gdn_h, gdn_h, fox_w, fox_w, fox_w, fox_h, lru_w, lru_w):
        offs.append(offs[-1] + s)
    cols = lambda i, j: w_in[:, :, offs[i]:offs[j]]
    pad = jnp.zeros(w_in.shape[:2] + (GATE_COLS - fox_h - 2 * gdn_h,), w_in.dtype)
    slabs = (cols(0, 4), cols(10, 12), cols(6, 9), jnp.concatenate([cols(9, 10), cols(4, 6), pad], axis=2))
    return dict(w_in=tuple(s.astype(BF16) for s in slabs), w_out=w_out.astype(BF16),
                w_up=ffn_w_up.astype(BF16), w_down=ffn_w_down.astype(BF16))


def _run_group(x, cache, states, p, weights, dims, tiles):
    gdn_w, gdn_h, fox_w, fox_h, lru_w = dims
    b, l, d = x.shape
    depth = weights['w_out'].shape[0]
    n_main = 4 * gdn_w + 2 * lru_w
    gdn_s, gdn_conv, lru_h, lru_conv, ffn_conv = states
    beta_col, a_col = GATE_F + fox_h, GATE_F + fox_h + gdn_h
    outs = [[] for _ in range(6)]
    kv = None
    for li in range(depth):
        main, small, fk, fv, qkv = _inproj(
            x.reshape(b * l, d), p['norm_mix_g'][:, None, :], weights['w_in'], kv,
            n_main=n_main, n_fox=fox_w, tm=tiles['proj'], layer=li)
        kv = (fk, fv)
        main3 = main.reshape(b, l, n_main)
        small3 = small.reshape(b, l, GATE_COLS)
        qkv3 = qkv.reshape(b, l, 3 * fox_w)
        bias_row = _pad_cols(p['fox_f_bias'][li][None], GATE_COLS)
        if cache is None:
            logf, ft = _gate(small3, bias_row, heads=fox_h, cumulative=True)
            o_fox = _fox_prompt(qkv3, ft, p['fox_norm_g'][li], heads=fox_h, tq=tiles['attn'])
        else:
            logf, = _gate(small3, bias_row, heads=fox_h, cumulative=False)
            clt = jnp.swapaxes(cache[2][li], 1, 2)
            lnt = jnp.pad(jnp.swapaxes(logf, 1, 2), ((0, 0), (0, 0), (0, LANES - l)))
            o_fox = _fox_sample(qkv3, cache[0], cache[1], clt, lnt, p['fox_norm_g'][li],
                                heads=fox_h, layer=li, tp=min(1024, cache[0].shape[2]))
        row = lambda v, c0: jnp.pad(v[None], ((0, 0), (c0, GATE_COLS - c0 - v.shape[0])))
        o_gdn, s_new, gconv_new = _gdn(
            main3, small3, gdn_conv[li], gdn_s[li], p['gdn_conv_w'][li],
            row(p['gdn_a_log'][li], a_col), row(p['gdn_dt_bias'][li], a_col),
            p['gdn_norm_g'][li][None], nbat=GDN_SEQS, rows=tiles['gdn'], c=min(l, CHUNK), heads=gdn_h,
            beta_col=beta_col, a_col=a_col)
        o_lru, h_new, lconv_new = _lru(
            main3, lru_conv[li], lru_h[li][:, None, :], p['lru_conv_w'][li], p['lru_conv_b'][li][None],
            p['lru_w_a'][li].astype(BF16), p['lru_w_x'][li].astype(BF16), p['lru_b_a'][li][None],
            p['lru_b_x'][li][None], p['lru_lambda'][li][None], p['lru_norm_g'][li][None],
            tl=tiles['lru'], x_block=4 * gdn_w // lru_w, gate_block=4 * gdn_w // lru_w + 1)
        x1 = _outproj(x.reshape(b * l, d), o_gdn.reshape(b * l, gdn_w), o_fox.reshape(b * l, fox_w),
                      o_lru.reshape(b * l, lru_w), weights['w_out'], tm=tiles['out'], layer=li)
        x, fconv_new = _ffn(
            x1.reshape(b, l, d), p['norm_ffn_g'][:, None, :], weights['w_up'], weights['w_down'],
            p['ffn_conv_w'], ffn_conv[li], p['final_norm_g'][None],
            tm=tiles['ffn'], tf=512, final_norm=li == depth - 1, layer=li)
        new = [logf, s_new, gconv_new, h_new[:, 0, :], lconv_new, fconv_new]
        for lst, arr in zip(outs, new):
            lst.append(arr)
    kv_shape = (depth, b, l, fox_h, HEAD_DIM)
    return x, [kv[0].reshape(kv_shape), kv[1].reshape(kv_shape)] + [jnp.stack(lst) for lst in outs]


def kernel(x_prompt, x_sample, cache_fox_k, cache_fox_v, cache_fox_logf, state_gdn, state_gdn_conv,
           state_lru, state_lru_conv, state_ffn_conv, norm_mix_g, w_in, gdn_conv_w, gdn_a_log,
           gdn_dt_bias, gdn_norm_g, fox_f_bias, fox_norm_g, lru_conv_w, lru_conv_b, lru_w_a, lru_b_a,
           lru_w_x, lru_b_x, lru_lambda, lru_norm_g, w_out, norm_ffn_g, ffn_w_up, ffn_conv_w,
           ffn_w_down, final_norm_g):
    p = dict(norm_mix_g=norm_mix_g, gdn_conv_w=gdn_conv_w, gdn_a_log=gdn_a_log,
             gdn_dt_bias=gdn_dt_bias, gdn_norm_g=gdn_norm_g, fox_f_bias=fox_f_bias,
             fox_norm_g=fox_norm_g, lru_conv_w=lru_conv_w, lru_conv_b=lru_conv_b, lru_w_a=lru_w_a,
             lru_b_a=lru_b_a, lru_w_x=lru_w_x, lru_b_x=lru_b_x, lru_lambda=lru_lambda,
             lru_norm_g=lru_norm_g, norm_ffn_g=norm_ffn_g, ffn_conv_w=ffn_conv_w,
             final_norm_g=final_norm_g)
    depth = w_in.shape[0]
    bp, lp, _ = x_prompt.shape
    gdn_h = state_gdn.shape[2]
    fox_h = cache_fox_k.shape[3]
    lru_w = state_lru.shape[2]
    dims = (gdn_h * HEAD_DIM, gdn_h, fox_h * HEAD_DIM, fox_h, lru_w)
    dff = ffn_w_down.shape[1]
    weights = _matmul_weights(w_in, w_out, ffn_w_up, ffn_w_down, dims)

    zeros = lambda *s: jnp.zeros((depth, bp) + s, F32)
    prompt_states = (zeros(gdn_h, HEAD_DIM, HEAD_DIM), zeros(gdn_conv_w.shape[1] - 1, 3 * dims[0]),
                     zeros(lru_w), zeros(lru_conv_w.shape[1] - 1, lru_w),
                     zeros(ffn_conv_w.shape[1] - 1, dff))
    prompt_tiles = dict(proj=min(256, bp * lp), out=min(512, bp * lp), attn=min(512, lp),
                        lru=min(512, lp), ffn=min(512, lp), gdn=min(256, lp))
    y_prompt, sp = _run_group(x_prompt, None, prompt_states, p, weights, dims, prompt_tiles)

    bs, ls, _ = x_sample.shape
    sample_tiles = dict(proj=bs * ls, out=bs * ls, lru=ls, ffn=ls, gdn=ls)
    y_sample, ss = _run_group(
        x_sample, (cache_fox_k, cache_fox_v, cache_fox_logf),
        (state_gdn, state_gdn_conv, state_lru, state_lru_conv, state_ffn_conv), p, weights, dims,
        sample_tiles)
    return (y_prompt, y_sample, *sp, *ss)
```

```python
import functools
import math

import jax
import jax.numpy as jnp
from jax import lax
from jax.experimental import pallas as pl
from jax.experimental.pallas import tpu as pltpu

F32 = jnp.float32
BF16 = jnp.bfloat16

HEAD_DIM = 128
CHUNK = 64
LRU_C = 8.0
EPS = 1e-6
LANES = 128
SUBLANES = 8
VMEM_LIMIT = 56 * 1024 * 1024

GATE_COLS = LANES
GATE_F = 0
FFN_SPLIT = 2
GDN_SEQS = 2
NEG_INF = float("-inf")
LOG2E = math.log2(math.e)


def _params(n_axes):
    return pltpu.CompilerParams(dimension_semantics=("arbitrary",) * n_axes,
                                vmem_limit_bytes=VMEM_LIMIT)


def _resident(block_shape, index_map):
    return pl.BlockSpec(block_shape, index_map, pipeline_mode=pl.Buffered(1))


def _dot(a, b):
    return jnp.dot(a.astype(BF16), b.astype(BF16), preferred_element_type=F32)


def _dot_nt(a, b):
    return lax.dot_general(a.astype(BF16), b.astype(BF16), (((1,), (1,)), ((), ())),
                           preferred_element_type=F32)


def _dot_tn(a, b):
    return lax.dot_general(a.astype(BF16), b.astype(BF16), (((0,), (0,)), ((), ())),
                           preferred_element_type=F32)


def _rms(x, g):
    return x * lax.rsqrt(jnp.mean(x * x, axis=-1, keepdims=True) + EPS) * g


def _expm1(x):
    u = jnp.exp(x)
    um1 = u - 1.0
    near = jnp.where(u == 1.0, x, um1 * x / jnp.log(u))
    return jnp.where(jnp.abs(x) > 0.5, um1, near)


def _cumsum_lanes(x):
    lane = lax.broadcasted_iota(jnp.int32, x.shape, x.ndim - 1)
    s = 1
    while s < x.shape[-1]:
        x = x + jnp.where(lane >= s, pltpu.roll(x, s, x.ndim - 1), 0.0)
        s *= 2
    return x


def _cumsum_rows(x, seg):
    pos = lax.broadcasted_iota(jnp.int32, x.shape, 0) % seg
    s = 1
    while s < seg:
        x = x + jnp.where(pos >= s, pltpu.roll(x, s, 0), 0.0)
        s *= 2
    return x


def _causal_conv(buf, x, w_ref, width, rows, cols=slice(None)):
    lo = SUBLANES - (width - 1)
    buf[SUBLANES:SUBLANES + rows, cols] = x
    y = w_ref[width - 1:width, cols] * x
    for j in range(width - 1):
        y = y + w_ref[j:j + 1, cols] * buf[lo + j:lo + j + rows, cols]
    return y


def _inproj_kernel(*refs, n_fox, seg, layer):
    x_ref, g_ref, wa_ref, wb_ref, wf_ref, ws_ref = refs[:6]
    n_in = 8 if layer else 6
    main_ref, small_ref, fk_ref, fv_ref, qkv_ref, h_scr = refs[n_in:]
    if layer:
        fk_ref[:layer] = refs[6][...]
        fv_ref[:layer] = refs[7][...]
    h_scr[...] = _rms(x_ref[...], g_ref[...]).astype(BF16)

    def cols(w_ref, c0, width):
        return jnp.dot(h_scr[...], w_ref[:, c0:c0 + width], preferred_element_type=F32)

    n_a = wa_ref.shape[1]
    for c0 in range(0, n_a, seg):
        main_ref[:, c0:c0 + seg] = cols(wa_ref, c0, seg)
    for c0 in range(0, wb_ref.shape[1], seg):
        main_ref[:, n_a + c0:n_a + c0 + seg] = cols(wb_ref, c0, seg)
    for j in range(3):
        for c in range(0, n_fox, seg):
            p = cols(wf_ref, j * n_fox + c, seg)
            qkv_ref[:, j * n_fox + c:j * n_fox + c + seg] = p.astype(BF16)
            if j == 1:
                fk_ref[layer, :, c:c + seg] = p
            if j == 2:
                fv_ref[layer, :, c:c + seg] = p
    small_ref[...] = cols(ws_ref, 0, GATE_COLS)


def _inproj(x2d, g, w_slabs, kv_prev, *, n_main, n_fox, tm, layer):
    m, d = x2d.shape
    row = lambda i: (i, 0)
    kv_row = lambda i: (0, i, 0)
    pick = lambda i: (layer, 0, 0)
    in_specs = [pl.BlockSpec((tm, d), row), _resident((None, 1, d), pick)]
    in_specs += [_resident((None, d, w.shape[2]), pick) for w in w_slabs]
    operands = [x2d, g, *w_slabs]
    if layer:
        in_specs += [pl.BlockSpec((layer, tm, n_fox), kv_row)] * 2
        operands += list(kv_prev)
    kv_out = jax.ShapeDtypeStruct((layer + 1, m, n_fox), F32)
    return pl.pallas_call(
        functools.partial(_inproj_kernel, n_fox=n_fox, seg=512, layer=layer),
        grid=(m // tm,),
        in_specs=in_specs,
        out_specs=[pl.BlockSpec((tm, n_main), row), pl.BlockSpec((tm, GATE_COLS), row),
                   pl.BlockSpec((layer + 1, tm, n_fox), kv_row),
                   pl.BlockSpec((layer + 1, tm, n_fox), kv_row),
                   pl.BlockSpec((tm, 3 * n_fox), row)],
        out_shape=[jax.ShapeDtypeStruct((m, n_main), F32), jax.ShapeDtypeStruct((m, GATE_COLS), F32),
                   kv_out, kv_out, jax.ShapeDtypeStruct((m, 3 * n_fox), BF16)],
        scratch_shapes=[pltpu.VMEM((tm, d), BF16)],
        compiler_params=_params(1), name="inproj")(*operands)


def _gate_kernel(sm_ref, b_ref, logf_ref, *rest, heads, cumulative):
    ls = jax.nn.log_sigmoid(sm_ref[0] + b_ref[...])
    logf_ref[0] = ls[:, GATE_F:GATE_F + heads]
    if cumulative:
        ft_ref, = rest
        ft_ref[0] = _cumsum_lanes(ls.T[GATE_F:GATE_F + heads, :])


def _gate(small3d, bias_row, *, heads, cumulative):
    b, rows, _ = small3d.shape
    out_specs = [pl.BlockSpec((1, rows, heads), lambda i: (i, 0, 0))]
    out_shape = [jax.ShapeDtypeStruct((b, rows, heads), F32)]
    if cumulative:
        out_specs.append(pl.BlockSpec((1, heads, rows), lambda i: (i, 0, 0)))
        out_shape.append(jax.ShapeDtypeStruct((b, heads, rows), F32))
    return pl.pallas_call(
        functools.partial(_gate_kernel, heads=heads, cumulative=cumulative),
        grid=(b,),
        in_specs=[pl.BlockSpec((1, rows, GATE_COLS), lambda i: (i, 0, 0)),
                  pl.BlockSpec((1, GATE_COLS), lambda i: (0, 0))],
        out_specs=out_specs, out_shape=out_shape,
        compiler_params=_params(1), name="fox_gate")(small3d, bias_row)


def _split_bf16(x):
    hi = x.astype(BF16)
    return hi, (x - hi.astype(F32)).astype(BF16)


def _dot_split(a, b):
    a_hi, a_lo = _split_bf16(a)
    b_hi, b_lo = _split_bf16(b)
    mm = lambda x, y: jnp.dot(x, y, preferred_element_type=F32)
    return mm(a_hi, b_hi) + (mm(a_lo, b_hi) + mm(a_hi, b_lo))


def _unit_lower_solve(a, rhs, c):
    r = a[0].shape[0]
    row = lax.broadcasted_iota(jnp.int32, (r, r), 0)
    col = lax.broadcasted_iota(jnp.int32, (r, r), 1)
    eye = jnp.where(row == col, 1.0, 0.0)
    p = [-x for x in a]
    t = [eye + x for x in p]
    for _ in range(int(math.log2(c)) - 1):
        p = [_dot(x, x) for x in p]
        t = [ti + _dot(ti, pi) for ti, pi in zip(t, p)]
    x = [_dot(ti, bi) for ti, bi in zip(t, rhs)]
    resid = [bi - xi - _dot_split(ai, xi) for ai, bi, xi in zip(a, rhs, x)]
    return [xi + _dot(ti, ri) for ti, xi, ri in zip(t, x, resid)]


def _gdn_kernel(qkv_ref, z_ref, sm_ref, conv0_ref, s0_ref, cw_ref, alog_ref, dt_ref, ng_ref,
                o_ref, s_ref, conv_ref, buf, *, nbat, rows, c, heads, conv_w, beta_col, a_col):
    ti = pl.program_id(1)
    nc = rows // c
    width = heads * HEAD_DIM
    hist = conv_w - 1
    lo = SUBLANES - hist

    @pl.when(ti == 0)
    def _():
        buf[:, lo:SUBLANES, :] = conv0_ref[...]
        s_ref[...] = s0_ref[...]

    row = lax.broadcasted_iota(jnp.int32, (rows, rows), 0)
    col = lax.broadcasted_iota(jnp.int32, (rows, rows), 1)
    lower = row >= col
    if nc > 1:
        lower = lower & (row // c == col // c)
    strict = lower & (row != col)
    pad = max(LANES - rows, 0)

    ys, beta, gcum, g_end, gcum_t = [], [], [], [], []
    for bi in range(nbat):
        y = _causal_conv(buf.at[bi], qkv_ref[bi], cw_ref, conv_w, rows)
        new_hist = buf[bi, lo + rows:SUBLANES + rows, :]
        conv_ref[bi] = new_hist
        buf[bi, lo:SUBLANES, :] = new_hist
        ys.append(jax.nn.silu(y))
        sm = sm_ref[bi]
        beta.append(jax.nn.sigmoid(sm))
        g = -jnp.exp(alog_ref[...]) * jax.nn.softplus(sm + dt_ref[...])
        gc_b = _cumsum_rows(g, c)
        gcum.append(gc_b)
        g_end.append(jnp.concatenate(
            [jnp.broadcast_to(gc_b[(i + 1) * c - 1:(i + 1) * c, :], (c, GATE_COLS))
             for i in range(nc)], axis=0))
        gcum_t.append(jnp.concatenate([gc_b, jnp.zeros((pad, GATE_COLS), F32)], axis=0).T
                      if pad else gc_b.T)

    chains = [(bi, h) for bi in range(nbat) for h in range(heads)]
    ids = range(len(chains))
    head = lambda bi, base, h: ys[bi][:, base + h * HEAD_DIM:base + (h + 1) * HEAD_DIM]
    l2 = lambda t: t * lax.rsqrt(jnp.sum(t * t, axis=-1, keepdims=True) + EPS)
    q = [l2(head(bi, 0, h)) * HEAD_DIM ** -0.5 for bi, h in chains]
    k = [l2(head(bi, width, h)) for bi, h in chains]
    v = [head(bi, 2 * width, h) for bi, h in chains]
    bh = [beta[bi][:, beta_col + h:beta_col + h + 1] for bi, h in chains]
    gc = [gcum[bi][:, a_col + h:a_col + h + 1] for bi, h in chains]
    ge = [g_end[bi][:, a_col + h:a_col + h + 1] for bi, h in chains]
    gr = [gcum_t[bi][a_col + h:a_col + h + 1, :rows] for bi, h in chains]
    decay = [jnp.exp(jnp.where(lower, gc[n] - gr[n], NEG_INF)) for n in ids]
    kb = [k[n] * bh[n] for n in ids]
    eg = [jnp.exp(gc[n]) for n in ids]
    a = [_dot_nt(kb[n], k[n]) * jnp.where(strict, decay[n], 0.0) for n in ids]
    attn = [_dot_nt(q[n], k[n]) * decay[n] for n in ids]
    sol = _unit_lower_solve(a, [jnp.concatenate([v[n] * bh[n], kb[n] * eg[n]], axis=-1) for n in ids], c)
    qe = [q[n] * eg[n] for n in ids]
    kx = [k[n] * jnp.exp(ge[n] - gc[n]) for n in ids]
    s = [s_ref[bi, h] for bi, h in chains]
    deltas = [[] for _ in ids]
    inter = [[] for _ in ids]
    for i in range(nc):
        rs = slice(i * c, (i + 1) * c)
        d_i = [sol[n][rs, :HEAD_DIM] - _dot(sol[n][rs, HEAD_DIM:], s[n]) for n in ids]
        for n in ids:
            inter[n].append(_dot(qe[n][rs], s[n]))
            deltas[n].append(d_i[n])
        s = [jnp.exp(ge[n][(i + 1) * c - 1:(i + 1) * c, :]) * s[n] + _dot_tn(kx[n][rs], d_i[n])
             for n in ids]
    cat = lambda parts: parts[0] if nc == 1 else jnp.concatenate(parts, axis=0)
    for n, (bi, h) in enumerate(chains):
        s_ref[bi, h] = s[n]
        o = cat(inter[n]) + _dot(attn[n], cat(deltas[n]))
        zh = z_ref[bi, :, h * HEAD_DIM:(h + 1) * HEAD_DIM]
        o_ref[bi, :, h * HEAD_DIM:(h + 1) * HEAD_DIM] = (
            _rms(o, ng_ref[...]) * jax.nn.silu(zh)).astype(o_ref.dtype)


def _gdn(main3d, small3d, conv0, s0, conv_w, alog_row, dt_row, norm_g, *, nbat, rows, c, heads,
         beta_col, a_col):
    b, l, _ = main3d.shape
    width = heads * HEAD_DIM
    kw = conv_w.shape[0]
    fixed2 = lambda i, j: (0, 0)
    return pl.pallas_call(
        functools.partial(_gdn_kernel, nbat=nbat, rows=rows, c=c, heads=heads, conv_w=kw,
                          beta_col=beta_col, a_col=a_col),
        grid=(b // nbat, l // rows),
        in_specs=[pl.BlockSpec((nbat, rows, 3 * width), lambda i, j: (i, j, 0)),
                  pl.BlockSpec((nbat, rows, width), lambda i, j: (i, j, 3)),
                  pl.BlockSpec((nbat, rows, GATE_COLS), lambda i, j: (i, j, 0)),
                  pl.BlockSpec((nbat, kw - 1, 3 * width), lambda i, j: (i, 0, 0)),
                  pl.BlockSpec((nbat, heads, HEAD_DIM, HEAD_DIM), lambda i, j: (i, 0, 0, 0)),
                  pl.BlockSpec((kw, 3 * width), fixed2),
                  pl.BlockSpec((1, GATE_COLS), fixed2), pl.BlockSpec((1, GATE_COLS), fixed2),
                  pl.BlockSpec((1, HEAD_DIM), fixed2)],
        out_specs=[pl.BlockSpec((nbat, rows, width), lambda i, j: (i, j, 0)),
                   pl.BlockSpec((nbat, heads, HEAD_DIM, HEAD_DIM), lambda i, j: (i, 0, 0, 0)),
                   pl.BlockSpec((nbat, kw - 1, 3 * width), lambda i, j: (i, 0, 0))],
        out_shape=[jax.ShapeDtypeStruct((b, l, width), BF16),
                   jax.ShapeDtypeStruct((b, heads, HEAD_DIM, HEAD_DIM), F32),
                   jax.ShapeDtypeStruct((b, kw - 1, 3 * width), F32)],
        scratch_shapes=[pltpu.VMEM((nbat, SUBLANES + rows, 3 * width), F32)],
        compiler_params=_params(2), name="gdn")(
            main3d, main3d, small3d, conv0, s0, conv_w, alog_row, dt_row, norm_g)


def _fox_prompt_kernel(q_ref, k_ref, v_ref, ft_ref, g_ref, o_ref, vaug, m_scr, acc_scr, s_a, s_b,
                       *, tq):
    nq = q_ref.shape[1] // tq
    vaug[:, :HEAD_DIM] = v_ref[0]
    vaug[:, HEAD_DIM:] = jnp.ones((vaug.shape[0], HEAD_DIM), vaug.dtype)
    scale = HEAD_DIM ** -0.5

    def block_start(j):
        return j * tq if isinstance(j, int) else pl.multiple_of(j * tq, tq)

    def scores(buf, qi, j):
        buf[...] = _dot_nt(q_ref[0, qi * tq:(qi + 1) * tq, :], k_ref[0, pl.ds(block_start(j), tq), :])

    def update(buf, j, r0, nrows, ncols, masked):
        start = block_start(j)
        rs = slice(r0, r0 + nrows)
        s = buf[rs, :ncols] * (scale * LOG2E) - ft_ref[0, 0, pl.ds(j, 1), :ncols] * LOG2E
        if masked:
            row = lax.broadcasted_iota(jnp.int32, (nrows, ncols), 0) + r0
            col = lax.broadcasted_iota(jnp.int32, (nrows, ncols), 1)
            s = jnp.where(col <= row, s, NEG_INF)
        m_prev = m_scr[rs, :]
        m_new = jnp.maximum(m_prev, jnp.max(s, axis=-1, keepdims=True))
        alpha = jnp.exp2(m_prev - m_new)
        p = jnp.exp2(s - jnp.concatenate([m_new] * (ncols // LANES), axis=1))
        pv = _dot(p, vaug[pl.ds(start, ncols), :])
        acc_scr[rs, :] = (jnp.concatenate([alpha] * (2 * HEAD_DIM // LANES), axis=1) * acc_scr[rs, :]
                          + pv)
        m_scr[rs, :] = m_new

    def accumulate(buf, j, masked):
        if masked and tq % (2 * LANES) == 0:
            update(buf, j, 0, tq // 2, tq // 2, True)
            update(buf, j, tq // 2, tq // 2, tq, True)
        else:
            update(buf, j, 0, tq, tq, masked)

    cur, other = s_a, s_b
    scores(cur, 0, 0)
    for qi in range(nq):
        m_scr[...] = jnp.full(m_scr.shape, NEG_INF, F32)
        acc_scr[...] = jnp.zeros(acc_scr.shape, F32)

        def pair(jj, carry, qi=qi, cur=cur, other=other):
            j = 2 * jj
            scores(other, qi, j + 1)
            accumulate(cur, j, False)
            scores(cur, qi, j + 2)
            accumulate(other, j + 1, False)
            return carry

        if qi >= 2:
            lax.fori_loop(0, qi // 2, pair, 0)
        if qi % 2 == 0:
            if qi + 1 < nq:
                scores(other, qi + 1, 0)
            accumulate(cur, qi, True)
            cur, other = other, cur
        else:
            scores(other, qi, qi)
            accumulate(cur, qi - 1, False)
            if qi + 1 < nq:
                scores(cur, qi + 1, 0)
            accumulate(other, qi, True)
        o = acc_scr[:, :HEAD_DIM] / acc_scr[:, HEAD_DIM:]
        o_ref[0, qi * tq:(qi + 1) * tq, :] = _rms(o, g_ref[0]).astype(o_ref.dtype)


def _fox_prompt(qkv3d, ft, norm_g, *, heads, tq):
    b, l, _ = qkv3d.shape
    ft4 = ft.reshape(b, heads, l // tq, tq)
    g3 = norm_g.reshape(heads, 1, HEAD_DIM)
    head_block = lambda part: pl.BlockSpec((1, l, HEAD_DIM), lambda i, h: (i, 0, part * heads + h))
    return pl.pallas_call(
        functools.partial(_fox_prompt_kernel, tq=tq),
        grid=(b, heads),
        in_specs=[head_block(0), head_block(1), head_block(2),
                  pl.BlockSpec((1, 1, l // tq, tq), lambda i, h: (i, h, 0, 0)),
                  pl.BlockSpec((1, 1, HEAD_DIM), lambda i, h: (h, 0, 0))],
        out_specs=pl.BlockSpec((1, l, HEAD_DIM), lambda i, h: (i, 0, h)),
        out_shape=jax.ShapeDtypeStruct((b, l, heads * HEAD_DIM), BF16),
        scratch_shapes=[pltpu.VMEM((l, 2 * HEAD_DIM), BF16), pltpu.VMEM((tq, LANES), F32),
                        pltpu.VMEM((tq, 2 * HEAD_DIM), F32), pltpu.VMEM((tq, tq), F32),
                        pltpu.VMEM((tq, tq), F32)],
        compiler_params=_params(2), name="fox_prompt")(qkv3d, qkv3d, qkv3d, ft4, g3)


def _fox_sample_kernel(q_ref, kn_ref, vn_ref, ck_ref, cv_ref, clt_ref, lnt_ref, g_ref, o_ref,
                       fc_scr, m_scr, l_scr, acc_scr, *, ls, heads, tp):
    j = pl.program_id(1)
    last = pl.num_programs(1) - 1
    scale = HEAD_DIM ** -0.5

    @pl.when(j == 0)
    def _():
        fc = _cumsum_lanes(clt_ref[0])
        for c in range(fc_scr.shape[0]):
            fc_scr[c] = fc[:, c * tp:(c + 1) * tp]
        m_scr[...] = jnp.full(m_scr.shape, NEG_INF, F32)
        l_scr[...] = jnp.zeros(l_scr.shape, F32)
        acc_scr[...] = jnp.zeros(acc_scr.shape, F32)

    def update(h, s, v):
        m_prev = m_scr[h]
        m_new = jnp.maximum(m_prev, jnp.max(s, axis=-1, keepdims=True))
        alpha = jnp.exp(m_prev - m_new)
        p = jnp.exp(s - m_new)
        l_scr[h] = alpha * l_scr[h] + jnp.sum(p, axis=-1, keepdims=True)
        acc_scr[h] = alpha * acc_scr[h] + _dot(p, v)
        m_scr[h] = m_new

    fcj = fc_scr[j]
    for h in range(heads):
        cs = slice(h * HEAD_DIM, (h + 1) * HEAD_DIM)
        k_h = ck_ref[0, pl.ds(h, tp, stride=heads), :]
        v_h = cv_ref[0, pl.ds(h, tp, stride=heads), :]
        update(h, _dot_nt(q_ref[0, :, cs], k_h) * scale - fcj[h:h + 1, :], v_h)

    @pl.when(j == last)
    def _():
        fn = fcj[:, tp - 1:tp] + _cumsum_lanes(lnt_ref[0])[:, :ls]
        row = lax.broadcasted_iota(jnp.int32, (ls, ls), 0)
        col = lax.broadcasted_iota(jnp.int32, (ls, ls), 1)
        for h in range(heads):
            cs = slice(h * HEAD_DIM, (h + 1) * HEAD_DIM)
            s_n = _dot_nt(q_ref[0, :, cs], kn_ref[0, :, cs]) * scale - fn[h:h + 1, :]
            update(h, jnp.where(col <= row, s_n, NEG_INF), vn_ref[0, :, cs])
            o = acc_scr[h] / l_scr[h]
            o_ref[0, :, cs] = _rms(o, g_ref[h:h + 1, :]).astype(o_ref.dtype)


def _fox_sample(qkv3d, cache_k, cache_v, cache_logf_t, logf_new_t, norm_g, *, heads, layer, tp):
    b, ls, _ = qkv3d.shape
    depth, _, p_len = cache_k.shape[:3]
    width = heads * HEAD_DIM
    ck = cache_k.reshape(depth * b, p_len * heads, HEAD_DIM)
    cv = cache_v.reshape(depth * b, p_len * heads, HEAD_DIM)
    cache_idx = lambda i, j: (layer * b + i, j, 0)
    return pl.pallas_call(
        functools.partial(_fox_sample_kernel, ls=ls, heads=heads, tp=tp),
        grid=(b, p_len // tp),
        in_specs=[pl.BlockSpec((1, ls, width), lambda i, j: (i, 0, 0)),
                  pl.BlockSpec((1, ls, width), lambda i, j: (i, 0, 1)),
                  pl.BlockSpec((1, ls, width), lambda i, j: (i, 0, 2)),
                  pl.BlockSpec((1, tp * heads, HEAD_DIM), cache_idx),
                  pl.BlockSpec((1, tp * heads, HEAD_DIM), cache_idx),
                  pl.BlockSpec((1, heads, p_len), lambda i, j: (i, 0, 0)),
                  pl.BlockSpec((1, heads, LANES), lambda i, j: (i, 0, 0)),
                  pl.BlockSpec((heads, HEAD_DIM), lambda i, j: (0, 0))],
        out_specs=pl.BlockSpec((1, ls, width), lambda i, j: (i, 0, 0)),
        out_shape=jax.ShapeDtypeStruct((b, ls, width), BF16),
        scratch_shapes=[pltpu.VMEM((p_len // tp, heads, tp), F32), pltpu.VMEM((heads, ls, 1), F32),
                        pltpu.VMEM((heads, ls, 1), F32), pltpu.VMEM((heads, ls, HEAD_DIM), F32)],
        compiler_params=_params(2), name="fox_sample")(
            qkv3d, qkv3d, qkv3d, ck, cv, cache_logf_t, logf_new_t, norm_g)


def _lru_kernel(x_ref, gate_ref, conv0_ref, h0_ref, cw_ref, cb_ref, wa_ref, wx_ref, ba_ref, bx_ref,
                lam_ref, ng_ref, o_ref, h_ref, conv_ref, buf, a_scr, hs_scr, *, tl, blocks, conv_w):
    ti = pl.program_id(1)
    hist = conv_w - 1
    lo = SUBLANES - hist
    bw = x_ref.shape[2] // blocks

    @pl.when(ti == 0)
    def _():
        buf[lo:SUBLANES, :] = conv0_ref[0]
        h_ref[0] = h0_ref[0]

    xc = _causal_conv(buf, x_ref[0], cw_ref, conv_w, tl) + cb_ref[...]
    new_hist = buf[lo + tl:SUBLANES + tl, :]
    conv_ref[0] = new_hist
    buf[lo:SUBLANES, :] = new_hist

    for n in range(blocks):
        sl = slice(n * bw, (n + 1) * bw)
        xb = xc[:, sl]
        r = jax.nn.sigmoid(_dot(xb, wa_ref[n]) + ba_ref[:, sl])
        i = jax.nn.sigmoid(_dot(xb, wx_ref[n]) + bx_ref[:, sl])
        log_a = -LRU_C * r * jax.nn.softplus(-lam_ref[:, sl])
        a_scr[:, sl] = jnp.exp(log_a)
        hs_scr[:, sl] = jnp.sqrt(-_expm1(2.0 * log_a)) * (i * xb)

    row = lax.broadcasted_iota(jnp.int32, (SUBLANES, a_scr.shape[1]), 0)

    def group(gi, h):
        start = pl.multiple_of(gi * SUBLANES, SUBLANES)
        a = a_scr[pl.ds(start, SUBLANES), :]
        u = hs_scr[pl.ds(start, SUBLANES), :]
        s = 1
        while s < SUBLANES:
            keep = row >= s
            u = jnp.where(keep, u + a * pltpu.roll(u, s, 0), u)
            a = jnp.where(keep, a * pltpu.roll(a, s, 0), a)
            s *= 2
        hg = u + a * h
        hs_scr[pl.ds(start, SUBLANES), :] = hg
        return hg[SUBLANES - 1:SUBLANES, :]

    h_ref[0] = lax.fori_loop(0, tl // SUBLANES, group, h_ref[0], unroll=4 if tl >= 32 else True)

    gate = gate_ref[0]
    for n in range(blocks):
        sl = slice(n * bw, (n + 1) * bw)
        o_ref[0, :, sl] = (_rms(hs_scr[:, sl], ng_ref[:, sl])
                           * jax.nn.gelu(gate[:, sl])).astype(o_ref.dtype)


def _lru(main3d, conv0, h0, cw, cb, wa, wx, ba, bx, lam, ng, *, tl, x_block, gate_block):
    b, l, _ = main3d.shape
    blocks, bw, _ = wa.shape
    width = blocks * bw
    kw = cw.shape[0]
    fixed2 = lambda i, j: (0, 0)
    fixed3 = lambda i, j: (0, 0, 0)
    vec = pl.BlockSpec((1, width), fixed2)
    return pl.pallas_call(
        functools.partial(_lru_kernel, tl=tl, blocks=blocks, conv_w=kw),
        grid=(b, l // tl),
        in_specs=[pl.BlockSpec((1, tl, width), lambda i, j: (i, j, x_block)),
                  pl.BlockSpec((1, tl, width), lambda i, j: (i, j, gate_block)),
                  pl.BlockSpec((1, kw - 1, width), lambda i, j: (i, 0, 0)),
                  pl.BlockSpec((1, 1, width), lambda i, j: (i, 0, 0)),
                  pl.BlockSpec((kw, width), fixed2), vec,
                  pl.BlockSpec((blocks, bw, bw), fixed3), pl.BlockSpec((blocks, bw, bw), fixed3),
                  vec, vec, vec, vec],
        out_specs=[pl.BlockSpec((1, tl, width), lambda i, j: (i, j, 0)),
                   pl.BlockSpec((1, 1, width), lambda i, j: (i, 0, 0)),
                   pl.BlockSpec((1, kw - 1, width), lambda i, j: (i, 0, 0))],
        out_shape=[jax.ShapeDtypeStruct((b, l, width), BF16),
                   jax.ShapeDtypeStruct((b, 1, width), F32),
                   jax.ShapeDtypeStruct((b, kw - 1, width), F32)],
        scratch_shapes=[pltpu.VMEM((SUBLANES + tl, width), F32), pltpu.VMEM((tl, width), F32),
                        pltpu.VMEM((tl, width), F32)],
        compiler_params=_params(2), name="rglru")(
            main3d, main3d, conv0, h0, cw, cb, wa, wx, ba, bx, lam, ng)


def _outproj_kernel(x_ref, og_ref, of_ref, ol_ref, w_ref, y_ref):
    wg, wf = og_ref.shape[1], of_ref.shape[1]
    y = x_ref[...] + jnp.dot(og_ref[...], w_ref[0:wg, :], preferred_element_type=F32)
    y = y + jnp.dot(of_ref[...], w_ref[wg:wg + wf, :], preferred_element_type=F32)
    y_ref[...] = y + jnp.dot(ol_ref[...], w_ref[wg + wf:, :], preferred_element_type=F32)


def _outproj(x2d, o_gdn, o_fox, o_lru, w, *, tm, layer):
    m, d = x2d.shape
    row = lambda i: (i, 0)
    return pl.pallas_call(
        _outproj_kernel, grid=(m // tm,),
        in_specs=[pl.BlockSpec((tm, d), row), pl.BlockSpec((tm, o_gdn.shape[1]), row),
                  pl.BlockSpec((tm, o_fox.shape[1]), row), pl.BlockSpec((tm, o_lru.shape[1]), row),
                  _resident((None,) + w.shape[1:], lambda i: (layer, 0, 0))],
        out_specs=pl.BlockSpec((tm, d), row),
        out_shape=jax.ShapeDtypeStruct((m, d), F32),
        compiler_params=_params(1), name="outproj")(x2d, o_gdn, o_fox, o_lru, w)


def _ffn_kernel(x_ref, g_ref, wg_ref, wv_ref, wd_ref, cw_ref, c0_ref, fg_ref, y_ref, cnew_ref,
                h_scr, gp_scr, *, nb, tiles_per_batch, tm, conv_w, final_norm):
    gi = pl.program_id(0)
    f = pl.program_id(1)
    hist = conv_w - 1
    lo = SUBLANES - hist

    rows = lambda j: slice(j * tm, (j + 1) * tm)

    @pl.when(f == 0)
    def _():
        for j in range(nb):
            x = x_ref[0, j]
            h_scr[rows(j), :] = _rms(x, g_ref[...]).astype(BF16)
            y_ref[0, j] = x

    if tiles_per_batch > 1:
        first = gi % tiles_per_batch == 0

        @pl.when(first)
        def _():
            gp_scr[0, lo:SUBLANES, :] = c0_ref[0]

        @pl.when(jnp.logical_not(first))
        def _():
            gp_scr[0, lo:SUBLANES, :] = cnew_ref[0, f]
    else:
        for j in range(nb):
            gp_scr[j, lo:SUBLANES, :] = c0_ref[j]

    tf = wg_ref.shape[1]
    half = tf // FFN_SPLIT
    h = h_scr[...]
    ups = []
    for c in range(FFN_SPLIT):
        cs = slice(c * half, (c + 1) * half)
        ups.append((jnp.dot(h, wg_ref[:, cs], preferred_element_type=F32),
                    jnp.dot(h, wv_ref[:, cs], preferred_element_type=F32)))
    for c in range(FFN_SPLIT):
        cs = slice(c * half, (c + 1) * half)
        gp, val = ups[c]
        acts = [jax.nn.silu(_causal_conv(gp_scr.at[j], gp[rows(j)], cw_ref, conv_w, tm, cs))
                * val[rows(j)] for j in range(nb)]
        act = (acts[0] if nb == 1 else jnp.concatenate(acts, axis=0)).astype(BF16)
        part = jnp.dot(act, wd_ref[cs, :], preferred_element_type=F32)
        for j in range(nb):
            y_ref[0, j] += part[rows(j)]
    for j in range(nb):
        cnew_ref[j, f] = gp_scr[j, lo + tm:SUBLANES + tm, :]

    if final_norm:
        @pl.when(f == pl.num_programs(1) - 1)
        def _():
            for j in range(nb):
                y_ref[0, j] = _rms(y_ref[0, j], fg_ref[...])


def _ffn(x3d, g, w_up, w_down, cw, c0, fg, *, tm, tf, final_norm, layer):
    b, l, d = x3d.shape
    dff = w_down.shape[1]
    kw = cw.shape[1]
    tiles_per_batch = l // tm
    nb = b if tiles_per_batch == 1 else 1
    groups = b * l // (nb * tm)
    x4 = x3d.reshape(groups, nb, tm, d)
    nf = dff // tf
    if tiles_per_batch == 1:
        c_idx = lambda i, f: (0, 0, f)
        cnew_idx = lambda i, f: (0, 0, 0, 0)
    else:
        c_idx = lambda i, f: (i // tiles_per_batch, 0, f)
        cnew_idx = lambda i, f: (i // tiles_per_batch, 0, 0, 0)
    y, cnew = pl.pallas_call(
        functools.partial(_ffn_kernel, nb=nb, tiles_per_batch=tiles_per_batch, tm=tm, conv_w=kw,
                          final_norm=final_norm),
        grid=(groups, dff // tf),
        in_specs=[pl.BlockSpec((1, nb, tm, d), lambda i, f: (i, 0, 0, 0)),
                  pl.BlockSpec((None, 1, d), lambda i, f: (layer, 0, 0)),
                  pl.BlockSpec((None, d, tf), lambda i, f: (layer, 0, f)),
                  pl.BlockSpec((None, d, tf), lambda i, f: (layer, 0, nf + f)),
                  pl.BlockSpec((None, tf, d), lambda i, f: (layer, f, 0)),
                  pl.BlockSpec((None, kw, tf), lambda i, f: (layer, 0, f)),
                  pl.BlockSpec((nb, kw - 1, tf), c_idx),
                  pl.BlockSpec((1, d), lambda i, f: (0, 0))],
        out_specs=[pl.BlockSpec((1, nb, tm, d), lambda i, f: (i, 0, 0, 0)),
                   pl.BlockSpec((nb, nf, kw - 1, tf), cnew_idx)],
        out_shape=[jax.ShapeDtypeStruct((groups, nb, tm, d), F32),
                   jax.ShapeDtypeStruct((b, nf, kw - 1, tf), F32)],
        scratch_shapes=[pltpu.VMEM((nb * tm, d), BF16), pltpu.VMEM((nb, SUBLANES + tm, tf), F32)],
        compiler_params=_params(2), name="ffn")(x4, g, w_up, w_up, w_down, cw, c0, fg)
    return y.reshape(b, l, d), jnp.swapaxes(cnew, 1, 2).reshape(b, kw - 1, dff)


def _pad_cols(a, width):
    return jnp.pad(a, ((0, 0), (0, width - a.shape[1])))


def _matmul_weights(w_in, w_out, ffn_w_up, ffn_w_down, sizes):
    gdn_w, gdn_h, fox_w, fox_h, lru_w = sizes
    offs = [0]
    for s in (gdn_w, gdn_w, gdn_w, gdn_w, gdn_h, gdn_h, fox_w, fox_w, fox_w, fox_h, lru_w, lru_w):
        offs.append(offs[-1] + s)
    cols = lambda i, j: w_in[:, :, offs[i]:offs[j]]
    pad = jnp.zeros(w_in.shape[:2] + (GATE_COLS - fox_h - 2 * gdn_h,), w_in.dtype)
    slabs = (cols(0, 4), cols(10, 12), cols(6, 9), jnp.concatenate([cols(9, 10), cols(4, 6), pad], axis=2))
    return dict(w_in=tuple(s.astype(BF16) for s in slabs), w_out=w_out.astype(BF16),
                w_up=ffn_w_up.astype(BF16), w_down=ffn_w_down.astype(BF16))


def _run_group(x, cache, states, p, weights, dims, tiles):
    gdn_w, gdn_h, fox_w, fox_h, lru_w = dims
    b, l, d = x.shape
    depth = weights['w_out'].shape[0]
    n_main = 4 * gdn_w + 2 * lru_w
    gdn_s, gdn_conv, lru_h, lru_conv, ffn_conv = states
    beta_col, a_col = GATE_F + fox_h, GATE_F + fox_h + gdn_h
    outs = [[] for _ in range(6)]
    kv = None
    for li in range(depth):
        main, small, fk, fv, qkv = _inproj(
            x.reshape(b * l, d), p['norm_mix_g'][:, None, :], weights['w_in'], kv,
            n_main=n_main, n_fox=fox_w, tm=tiles['proj'], layer=li)
        kv = (fk, fv)
        main3 = main.reshape(b, l, n_main)
        small3 = small.reshape(b, l, GATE_COLS)
        qkv3 = qkv.reshape(b, l, 3 * fox_w)
        bias_row = _pad_cols(p['fox_f_bias'][li][None], GATE_COLS)
        if cache is None:
            logf, ft = _gate(small3, bias_row, heads=fox_h, cumulative=True)
            o_fox = _fox_prompt(qkv3, ft, p['fox_norm_g'][li], heads=fox_h, tq=tiles['attn'])
        else:
            logf, = _gate(small3, bias_row, heads=fox_h, cumulative=False)
            clt = jnp.swapaxes(cache[2][li], 1, 2)
            lnt = jnp.pad(jnp.swapaxes(logf, 1, 2), ((0, 0), (0, 0), (0, LANES - l)))
            o_fox = _fox_sample(qkv3, cache[0], cache[1], clt, lnt, p['fox_norm_g'][li],
                                heads=fox_h, layer=li, tp=min(1024, cache[0].shape[2]))
        row = lambda v, c0: jnp.pad(v[None], ((0, 0), (c0, GATE_COLS - c0 - v.shape[0])))
        o_gdn, s_new, gconv_new = _gdn(
            main3, small3, gdn_conv[li], gdn_s[li], p['gdn_conv_w'][li],
            row(p['gdn_a_log'][li], a_col), row(p['gdn_dt_bias'][li], a_col),
            p['gdn_norm_g'][li][None], nbat=math.gcd(b, GDN_SEQS), rows=tiles['gdn'],
            c=min(l, CHUNK), heads=gdn_h,
            beta_col=beta_col, a_col=a_col)
        o_lru, h_new, lconv_new = _lru(
            main3, lru_conv[li], lru_h[li][:, None, :], p['lru_conv_w'][li], p['lru_conv_b'][li][None],
            p['lru_w_a'][li].astype(BF16), p['lru_w_x'][li].astype(BF16), p['lru_b_a'][li][None],
            p['lru_b_x'][li][None], p['lru_lambda'][li][None], p['lru_norm_g'][li][None],
            tl=tiles['lru'], x_block=4 * gdn_w // lru_w, gate_block=4 * gdn_w // lru_w + 1)
        x1 = _outproj(x.reshape(b * l, d), o_gdn.reshape(b * l, gdn_w), o_fox.reshape(b * l, fox_w),
                      o_lru.reshape(b * l, lru_w), weights['w_out'], tm=tiles['out'], layer=li)
        x, fconv_new = _ffn(
            x1.reshape(b, l, d), p['norm_ffn_g'][:, None, :], weights['w_up'], weights['w_down'],
            p['ffn_conv_w'], ffn_conv[li], p['final_norm_g'][None],
            tm=tiles['ffn'], tf=512, final_norm=li == depth - 1, layer=li)
        new = [logf, s_new, gconv_new, h_new[:, 0, :], lconv_new, fconv_new]
        for lst, arr in zip(outs, new):
            lst.append(arr)
    kv_shape = (depth, b, l, fox_h, HEAD_DIM)
    return x, [kv[0].reshape(kv_shape), kv[1].reshape(kv_shape)] + [jnp.stack(lst) for lst in outs]


def kernel(x_prompt, x_sample, cache_fox_k, cache_fox_v, cache_fox_logf, state_gdn, state_gdn_conv,
           state_lru, state_lru_conv, state_ffn_conv, norm_mix_g, w_in, gdn_conv_w, gdn_a_log,
           gdn_dt_bias, gdn_norm_g, fox_f_bias, fox_norm_g, lru_conv_w, lru_conv_b, lru_w_a, lru_b_a,
           lru_w_x, lru_b_x, lru_lambda, lru_norm_g, w_out, norm_ffn_g, ffn_w_up, ffn_conv_w,
           ffn_w_down, final_norm_g):
    p = dict(norm_mix_g=norm_mix_g, gdn_conv_w=gdn_conv_w, gdn_a_log=gdn_a_log,
             gdn_dt_bias=gdn_dt_bias, gdn_norm_g=gdn_norm_g, fox_f_bias=fox_f_bias,
             fox_norm_g=fox_norm_g, lru_conv_w=lru_conv_w, lru_conv_b=lru_conv_b, lru_w_a=lru_w_a,
             lru_b_a=lru_b_a, lru_w_x=lru_w_x, lru_b_x=lru_b_x, lru_lambda=lru_lambda,
             lru_norm_g=lru_norm_g, norm_ffn_g=norm_ffn_g, ffn_conv_w=ffn_conv_w,
             final_norm_g=final_norm_g)
    depth = w_in.shape[0]
    bp, lp, _ = x_prompt.shape
    gdn_h = state_gdn.shape[2]
    fox_h = cache_fox_k.shape[3]
    lru_w = state_lru.shape[2]
    dims = (gdn_h * HEAD_DIM, gdn_h, fox_h * HEAD_DIM, fox_h, lru_w)
    dff = ffn_w_down.shape[1]
    weights = _matmul_weights(w_in, w_out, ffn_w_up, ffn_w_down, dims)

    zeros = lambda *s: jnp.zeros((depth, bp) + s, F32)
    prompt_states = (zeros(gdn_h, HEAD_DIM, HEAD_DIM), zeros(gdn_conv_w.shape[1] - 1, 3 * dims[0]),
                     zeros(lru_w), zeros(lru_conv_w.shape[1] - 1, lru_w),
                     zeros(ffn_conv_w.shape[1] - 1, dff))
    prompt_tiles = dict(proj=min(256, bp * lp), out=min(512, bp * lp), attn=min(512, lp),
                        lru=min(512, lp), ffn=min(1024, lp), gdn=min(256, lp))
    y_prompt, sp = _run_group(x_prompt, None, prompt_states, p, weights, dims, prompt_tiles)

    bs, ls, _ = x_sample.shape
    sample_tiles = dict(proj=bs * ls, out=bs * ls, lru=ls, ffn=ls, gdn=ls)
    y_sample, ss = _run_group(
        x_sample, (cache_fox_k, cache_fox_v, cache_fox_logf),
        (state_gdn, state_gdn_conv, state_lru, state_lru_conv, state_ffn_conv), p, weights, dims,
        sample_tiles)
    return (y_prompt, y_sample, *sp, *ss)
```
